```python
import math
import jax
import jax.numpy as jnp
from jax import lax
import numpy as np

D_MODEL = 1024
BATCH = 16
SEQ = 4096
DEPTH = 4

CTX_LEN = 256
GRID_W = 64
N_MOD = 6
GDN_HEADS = 4
GDN_DK = 128
GDN_DV = 128
GDN_CONV = 5
GDN_CHUNK = 64
SC_WIDTH = 512
SC_CONV = 3
SG_CHUNK = 128
SG_GROUPS = 4
SG_GROUP_W = 128
SG_WIDTH = SG_GROUPS * SG_GROUP_W
ATT_HEADS = 8
ATT_KV_HEADS = 2
HEAD_DIM = 64
WINDOW = 128
ROPE_BASE = 10000.0
N_BRANCH = 4
BRANCH_W = 512
N_EXPERTS = 32
TOP_K = 4
D_EXPERT = 1024
SWIGLU_ALPHA = 1.702
SWIGLU_LIMIT = 7.0
MOE_BLOCK = 256
NORM_EPS = 1e-6
NEG_INF = -1e30
GDN_QKV_COLS = 2 * GDN_HEADS * GDN_DK + GDN_HEADS * GDN_DV
GDN_Z_COLS = GDN_HEADS * GDN_DV
GDN_DIR_COLS = 2 * GDN_HEADS
SC_COLS = 3 * SC_WIDTH
SG_COLS = 2 * SG_WIDTH
ATT_COLS = (ATT_HEADS + 2 * ATT_KV_HEADS) * HEAD_DIM
MERGE_COLS = N_BRANCH * D_MODEL
IN_SPLITS = (GDN_QKV_COLS, GDN_Z_COLS, GDN_DIR_COLS, GDN_DIR_COLS, SC_COLS, SG_COLS, ATT_COLS, MERGE_COLS)
IN_COLS = sum(IN_SPLITS)

kernel_name = "hybrid_parallel_deltanet_conv_sgmlp_swa_moe"


def rms_norm(x, w, eps=NORM_EPS):
    xf = x.astype(jnp.float32)
    y = xf * lax.rsqrt(jnp.mean(xf * xf, axis=-1, keepdims=True) + eps)
    return (y * w.astype(jnp.float32)).astype(x.dtype)


def layer_norm(x, w, b, eps=1e-5):
    xf = x.astype(jnp.float32)
    mu = jnp.mean(xf, axis=-1, keepdims=True)
    xc = xf - mu
    y = xc * lax.rsqrt(jnp.mean(xc * xc, axis=-1, keepdims=True) + eps)
    return (y * w.astype(jnp.float32) + b.astype(jnp.float32)).astype(x.dtype)


def l2_normalize(x, eps=NORM_EPS):
    return x * lax.rsqrt(jnp.sum(x * x, axis=-1, keepdims=True) + eps)


def dwconv_centred(x, w):
    k, ch = w.shape
    return lax.conv_general_dilated(x, w[:, None, :], window_strides=(1,), padding=[(k // 2, k // 2)],
                                    dimension_numbers=('NWC', 'WIO', 'NWC'), feature_group_count=ch)


def axial_rope_tables(rows):
    row = jnp.repeat(jnp.arange(rows, dtype=jnp.float32), GRID_W)
    col = jnp.tile(jnp.arange(GRID_W, dtype=jnp.float32), rows)
    n_freq = HEAD_DIM // 4
    inv = ROPE_BASE ** (-jnp.arange(n_freq, dtype=jnp.float32) / n_freq)
    ang = jnp.concatenate([row[:, None] * inv, col[:, None] * inv], axis=-1)
    return jnp.cos(ang), jnp.sin(ang)


def apply_axial_rope(x, cos, sin):
    nf = HEAD_DIM // 4
    half = HEAD_DIM // 2
    cos = cos[None, :, None, :].astype(x.dtype)
    sin = sin[None, :, None, :].astype(x.dtype)

    def rot(t, c, s):
        t1, t2 = t[..., :nf], t[..., nf:]
        return jnp.concatenate([t1 * c - t2 * s, t2 * c + t1 * s], axis=-1)

    return jnp.concatenate([rot(x[..., :half], cos[..., :nf], sin[..., :nf]),
                            rot(x[..., half:], cos[..., nf:], sin[..., nf:])], axis=-1)


def gated_delta_chunked(q, k, v, g, beta, state0):
    bn, seq_len, nh, _ = q.shape
    dv = v.shape[-1]
    cs = GDN_CHUNK
    n = seq_len // cs

    def chunk(t):
        return t.reshape(bn, n, cs, nh, -1).transpose(0, 3, 1, 2, 4)

    q, k, v = chunk(q), chunk(k), chunk(v)
    g = g.reshape(bn, n, cs, nh).transpose(0, 3, 1, 2)
    beta = beta.reshape(bn, n, cs, nh).transpose(0, 3, 1, 2)
    gc = jnp.cumsum(g, axis=-1)
    tri = jnp.tril(jnp.ones((cs, cs), bool))
    strict = jnp.tril(jnp.ones((cs, cs), bool), -1)
    decay = jnp.exp(jnp.where(tri, gc[..., :, None] - gc[..., None, :], NEG_INF))
    kk = jnp.einsum('bhnid,bhnjd->bhnij', k, k)
    a_mat = jnp.where(strict, kk * decay, 0.0) * beta[..., :, None] + jnp.eye(cs, dtype=q.dtype)
    rhs = jnp.concatenate([v * beta[..., None], k * (beta * jnp.exp(gc))[..., None]], axis=-1)
    sol = lax.linalg.triangular_solve(a_mat, rhs, left_side=True, lower=True, unit_diagonal=True)
    u, w = sol[..., :dv], sol[..., dv:]
    qk = jnp.einsum('bhnid,bhnjd->bhnij', q, k) * decay
    q_dec = q * jnp.exp(gc)[..., None]
    k_dec = k * jnp.exp(gc[..., -1:] - gc)[..., None]
    g_last = jnp.exp(gc[..., -1])

    def step(s, xs):
        u_c, w_c, qk_c, qd_c, kd_c, gl_c = xs
        v_new = u_c - jnp.einsum('bhcd,bhde->bhce', w_c, s)
        o_c = jnp.einsum('bhcd,bhde->bhce', qd_c, s) + jnp.einsum('bhcs,bhse->bhce', qk_c, v_new)
        s = s * gl_c[..., None, None] + jnp.einsum('bhcd,bhce->bhde', kd_c, v_new)
        return s, o_c

    xs = tuple(jnp.moveaxis(t, 2, 0) for t in (u, w, qk, q_dec, k_dec, g_last))
    s_final, o = lax.scan(step, state0, xs)
    o = jnp.moveaxis(o, 0, 2).transpose(0, 2, 3, 1, 4).reshape(bn, seq_len, nh, dv)
    return o, s_final


def _flip(t, rev):
    return jnp.flip(t, axis=1) if rev else t


def gdn_mixer(qkv_l, qkv_c, z_l, z_c, a_l, a_c, b_l, b_c, conv_w, a_log, dt_bias, out_norm, with_ctx_out):
    def prep(qkv, a, b):
        bn, seq_len, _ = qkv.shape
        qkv = jax.nn.silu(dwconv_centred(qkv, conv_w)).astype(jnp.float32)
        q, k, v = jnp.split(qkv, [GDN_HEADS * GDN_DK, 2 * GDN_HEADS * GDN_DK], axis=-1)
        q = l2_normalize(q.reshape(bn, seq_len, GDN_HEADS, GDN_DK)) * (GDN_DK ** -0.5)
        k = l2_normalize(k.reshape(bn, seq_len, GDN_HEADS, GDN_DK))
        v = v.reshape(bn, seq_len, GDN_HEADS, GDN_DV)
        g = -jnp.exp(a_log.astype(jnp.float32)) * jax.nn.softplus(
            a.astype(jnp.float32).reshape(bn, seq_len, 2, GDN_HEADS) + dt_bias.astype(jnp.float32))
        beta = jax.nn.sigmoid(b.astype(jnp.float32).reshape(bn, seq_len, 2, GDN_HEADS))
        return q, k, v, g, beta

    def run_direction(inp, d, s0):
        q, k, v, g, beta = inp
        rev = d == 1
        o, s = gated_delta_chunked(_flip(q, rev), _flip(k, rev), _flip(v, rev),
                                   _flip(g[:, :, d], rev), _flip(beta[:, :, d], rev), s0)
        return _flip(o, rev), s

    def out_gate(o, z):
        bn, seq_len, _ = z.shape
        y = rms_norm(o, out_norm) * jax.nn.silu(z.astype(jnp.float32).reshape(bn, seq_len, GDN_HEADS, GDN_DV))
        return y.reshape(bn, seq_len, GDN_HEADS * GDN_DV).astype(z.dtype)

    ctx_in = prep(qkv_c, a_c, b_c)
    lat_in = prep(qkv_l, a_l, b_l)
    state0 = jnp.zeros((qkv_l.shape[0], GDN_HEADS, GDN_DK, GDN_DV), jnp.float32)
    o_ctx = 0.0
    o_lat = 0.0
    for d in range(2):
        oc, s_ctx = run_direction(ctx_in, d, state0)
        ol, _ = run_direction(lat_in, d, s_ctx)
        o_ctx = o_ctx + oc
        o_lat = o_lat + ol
    y_ctx = out_gate(o_ctx, z_c) if with_ctx_out else None
    return out_gate(o_lat, z_l), y_ctx


def short_conv_mixer(p, conv_w):
    b_gate, c_gate, xin = jnp.split(p, 3, axis=-1)
    return b_gate * dwconv_centred(c_gate * xin, conv_w)


def spatial_gating_mixer(p, ln_w, ln_b, w_s, b_s):
    bn, seq_len, _ = p.shape
    n = seq_len // SG_CHUNK
    u, v = jnp.split(jax.nn.gelu(p, approximate=False), 2, axis=-1)
    v = layer_norm(v, ln_w, ln_b).reshape(bn, n, SG_CHUNK, SG_GROUPS, SG_GROUP_W)
    s = jnp.einsum('gts,bnsgc->bntgc', w_s, v) + b_s.T[:, :, None]
    return u * s.reshape(bn, seq_len, SG_WIDTH)


def window_attention(p_l, p_c, q_norm, k_norm, sink, cos, sin, with_ctx_out):
    grp = ATT_HEADS // ATT_KV_HEADS
    scale = HEAD_DIM ** -0.5
    f32 = jnp.float32

    def heads(p):
        bn, seq_len, _ = p.shape
        q, k, v = jnp.split(p, [ATT_HEADS * HEAD_DIM, (ATT_HEADS + ATT_KV_HEADS) * HEAD_DIM], axis=-1)
        q = rms_norm(q.reshape(bn, seq_len, ATT_HEADS, HEAD_DIM), q_norm)
        k = rms_norm(k.reshape(bn, seq_len, ATT_KV_HEADS, HEAD_DIM), k_norm)
        return q, k, v.reshape(bn, seq_len, ATT_KV_HEADS, HEAD_DIM)

    out_dtype = p_l.dtype
    qc, kc, vc = heads(p_c)
    ql, kl, vl = heads(p_l)
    ql = apply_axial_rope(ql, cos, sin)
    kl = apply_axial_rope(kl, cos, sin)
    kc = kc.astype(f32)
    vc = vc.astype(f32)
    sink = sink.astype(f32).reshape(ATT_KV_HEADS, grp)[:, :, None]
    bn, seq_len = ql.shape[:2]

    o_ctx = None
    if with_ctx_out:
        lc = qc.shape[1]
        qcg = qc.reshape(bn, lc, ATT_KV_HEADS, grp, HEAD_DIM).astype(f32) * scale
        s = jnp.einsum('bqkgd,bskd->bkgqs', qcg, kc)
        m = jnp.maximum(s.max(-1), sink)
        e = jnp.exp(s - m[..., None])
        den = e.sum(-1) + jnp.exp(sink - m)
        o_ctx = jnp.einsum('bkgqs,bskd->bqkgd', e / den[..., None], vc).reshape(bn, lc, -1).astype(out_dtype)

    n = seq_len // WINDOW
    qlg = ql.reshape(bn, n, WINDOW, ATT_KV_HEADS, grp, HEAD_DIM).astype(f32) * scale

    def band(t):
        tp = jnp.pad(t.astype(f32), ((0, 0), (WINDOW, WINDOW), (0, 0), (0, 0)))
        tp = tp.reshape(bn, n + 2, WINDOW, ATT_KV_HEADS, HEAD_DIM)
        return jnp.concatenate([tp[:, :-2], tp[:, 1:-1], tp[:, 2:]], axis=2)

    kw, vw = band(kl), band(vl)
    blk = jnp.arange(n)[:, None, None]
    qpos = blk * WINDOW + jnp.arange(WINDOW)[None, :, None]
    kpos = (blk - 1) * WINDOW + jnp.arange(3 * WINDOW)[None, None, :]
    valid = (jnp.abs(qpos - kpos) <= WINDOW) & (kpos >= 0) & (kpos < seq_len)
    s_loc = jnp.einsum('bnqkgd,bnskd->bnkgqs', qlg, kw)
    s_loc = jnp.where(valid[None, :, None, None], s_loc, NEG_INF)
    s_ctx = jnp.einsum('bnqkgd,bskd->bnkgqs', qlg, kc)
    m = jnp.maximum(jnp.maximum(s_loc.max(-1), s_ctx.max(-1)), sink)
    e_loc = jnp.exp(s_loc - m[..., None])
    e_ctx = jnp.exp(s_ctx - m[..., None])
    den = e_loc.sum(-1) + e_ctx.sum(-1) + jnp.exp(sink - m)
    o = jnp.einsum('bnkgqs,bnskd->bnqkgd', e_loc, vw) + jnp.einsum('bnkgqs,bskd->bnqkgd', e_ctx, vc)
    o = o / jnp.moveaxis(den, -1, 2)[..., None]
    return o.reshape(bn, seq_len, ATT_HEADS * HEAD_DIM).astype(out_dtype), o_ctx


def merge_branches(ys, gate_cols, w_branch, w_out):
    bn, seq_len, _ = gate_cols.shape
    gates = jax.nn.sigmoid(gate_cols.reshape(bn, seq_len, N_BRANCH, D_MODEL))
    m = gates[:, :, 0] * (ys[0] @ w_branch[0])
    for i in range(1, N_BRANCH):
        m = m + gates[:, :, i] * (ys[i] @ w_branch[i])
    return m @ w_out


def token_mixing(h_lat, h_ctx, with_ctx_out, w_in, gdn_conv, gdn_a_log, gdn_dt_bias, gdn_out_norm, sc_conv,
                 sg_ln_w, sg_ln_b, sg_w, sg_b, q_norm, k_norm, sink, w_branch, w_out, cos, sin):
    idx = np.cumsum(IN_SPLITS)[:-1].tolist()
    pl = jnp.split(h_lat @ w_in, idx, axis=-1)
    pc = jnp.split(h_ctx @ w_in, idx, axis=-1)
    gdn_l, gdn_c = gdn_mixer(pl[0], pc[0], pl[1], pc[1], pl[2], pc[2], pl[3], pc[3],
                             gdn_conv, gdn_a_log, gdn_dt_bias, gdn_out_norm, with_ctx_out)
    att_l, att_c = window_attention(pl[6], pc[6], q_norm, k_norm, sink, cos, sin, with_ctx_out)
    ys_lat = (gdn_l, short_conv_mixer(pl[4], sc_conv),
              spatial_gating_mixer(pl[5], sg_ln_w, sg_ln_b, sg_w, sg_b), att_l)
    out_lat = merge_branches(ys_lat, pl[7], w_branch, w_out)
    out_ctx = None
    if with_ctx_out:
        ys_ctx = (gdn_c, short_conv_mixer(pc[4], sc_conv),
                  spatial_gating_mixer(pc[5], sg_ln_w, sg_ln_b, sg_w, sg_b), att_c)
        out_ctx = merge_branches(ys_ctx, pc[7], w_branch, w_out)
    return out_lat, out_ctx


def clamped_swiglu(h):
    h_glu = jnp.minimum(h[..., ::2], SWIGLU_LIMIT)
    h_lin = jnp.clip(h[..., 1::2], -SWIGLU_LIMIT, SWIGLU_LIMIT)
    return h_glu * jax.nn.sigmoid(SWIGLU_ALPHA * h_glu) * (h_lin + 1.0)


def moe_ffn(xf, router_w, router_b, w1, b1, w2, b2):
    n_tok, dm = xf.shape
    logits = (xf @ router_w + router_b).astype(jnp.float32)
    top_val, top_idx = lax.top_k(logits, TOP_K)
    gates = jax.nn.softmax(top_val, axis=-1)
    n_assign = n_tok * TOP_K
    flat_e = top_idx.reshape(-1)
    order = jnp.argsort(flat_e)
    sorted_e = flat_e[order]
    tok = order // TOP_K
    counts = jnp.zeros((N_EXPERTS,), jnp.int32).at[flat_e].add(1)
    padded = (counts + MOE_BLOCK - 1) // MOE_BLOCK * MOE_BLOCK
    pad_end = jnp.cumsum(padded)
    pad_start = pad_end - padded
    start = jnp.cumsum(counts) - counts
    dest = pad_start[sorted_e] + jnp.arange(n_assign, dtype=jnp.int32) - start[sorted_e]
    n_blocks = -(-n_assign // MOE_BLOCK) + N_EXPERTS
    buf = jnp.zeros((n_blocks * MOE_BLOCK, dm), xf.dtype).at[dest].set(xf[tok])
    block_expert = jnp.minimum(
        jnp.searchsorted(pad_end, jnp.arange(n_blocks, dtype=jnp.int32) * MOE_BLOCK, side='right'), N_EXPERTS - 1)

    def expert_block(args):
        xb, e = args
        return clamped_swiglu(xb @ w1[e] + b1[e]) @ w2[e] + b2[e]

    out_buf = lax.map(expert_block, (buf.reshape(n_blocks, MOE_BLOCK, dm), block_expert)).reshape(-1, dm)
    g_sorted = gates.reshape(-1)[order].astype(xf.dtype)
    return jnp.zeros_like(xf).at[tok].add(out_buf[dest] * g_sorted[:, None])


def setup_inputs(seed: int = 0) -> dict:
    key = jax.random.key(seed)
    ks = jax.random.split(key, 32)
    f32 = jnp.float32
    dm = D_MODEL

    def nrm(k, shape, s):
        return s * jax.random.normal(k, shape, f32)

    dt = jnp.exp(jax.random.uniform(ks[10], (DEPTH, 2, GDN_HEADS), f32, math.log(1e-3), math.log(1e-1)))
    return {
        "x": nrm(ks[0], (BATCH, SEQ, dm), 1.0),
        "c": nrm(ks[1], (BATCH, dm), 1.0),
        "ctx": nrm(ks[2], (BATCH, CTX_LEN, dm), 1.0),
        "c_ctx": nrm(ks[3], (dm,), 1.0),
        "norm1": 1.0 + nrm(ks[4], (DEPTH, dm), 0.02),
        "norm2": 1.0 + nrm(ks[5], (DEPTH, dm), 0.02),
        "w_mod": nrm(ks[6], (DEPTH, dm, N_MOD * dm), 0.5 * dm ** -0.5),
        "b_mod": nrm(ks[7], (DEPTH, N_MOD * dm), 0.01),
        "w_in": nrm(ks[8], (DEPTH, dm, IN_COLS), dm ** -0.5),
        "gdn_conv": nrm(ks[9], (DEPTH, GDN_CONV, GDN_QKV_COLS), GDN_CONV ** -0.5),
        "gdn_a_log": jnp.log(jax.random.uniform(ks[11], (DEPTH, 2, GDN_HEADS), f32, 1.0, 16.0)),
        "gdn_dt_bias": dt + jnp.log(-jnp.expm1(-dt)),
        "gdn_out_norm": 1.0 + nrm(ks[12], (DEPTH, GDN_DV), 0.02),
        "sc_conv": nrm(ks[13], (DEPTH, SC_CONV, SC_WIDTH), SC_CONV ** -0.5),
        "sg_ln_w": 1.0 + nrm(ks[14], (DEPTH, SG_WIDTH), 0.02),
        "sg_ln_b": nrm(ks[15], (DEPTH, SG_WIDTH), 0.02),
        "sg_w": nrm(ks[16], (DEPTH, SG_GROUPS, SG_CHUNK, SG_CHUNK), SG_CHUNK ** -0.5),
        "sg_b": 1.0 + nrm(ks[17], (DEPTH, SG_GROUPS, SG_CHUNK), 0.02),
        "attn_q_norm": 1.0 + nrm(ks[18], (DEPTH, HEAD_DIM), 0.02),
        "attn_k_norm": 1.0 + nrm(ks[19], (DEPTH, HEAD_DIM), 0.02),
        "attn_sink": nrm(ks[20], (DEPTH, ATT_HEADS), 0.5),
        "w_branch": nrm(ks[21], (DEPTH, N_BRANCH, BRANCH_W, dm), BRANCH_W ** -0.5),
        "w_out": nrm(ks[22], (DEPTH, dm, dm), dm ** -0.5),
        "router_w": nrm(ks[23], (DEPTH, dm, N_EXPERTS), dm ** -0.5),
        "router_b": nrm(ks[24], (DEPTH, N_EXPERTS), 0.01),
        "exp_w1": nrm(ks[25], (DEPTH, N_EXPERTS, dm, 2 * D_EXPERT), dm ** -0.5),
        "exp_b1": nrm(ks[26], (DEPTH, N_EXPERTS, 2 * D_EXPERT), 0.01),
        "exp_w2": nrm(ks[27], (DEPTH, N_EXPERTS, D_EXPERT, dm), D_EXPERT ** -0.5),
        "exp_b2": nrm(ks[28], (DEPTH, N_EXPERTS, dm), 0.01),
    }


def reference(x, c, ctx, c_ctx, norm1, norm2, w_mod, b_mod, w_in, gdn_conv, gdn_a_log, gdn_dt_bias,
              gdn_out_norm, sc_conv, sg_ln_w, sg_ln_b, sg_w, sg_b, attn_q_norm, attn_k_norm, attn_sink,
              w_branch, w_out, router_w, router_b, exp_w1, exp_b1, exp_w2, exp_b2):
    bn, n_tok, dm = x.shape
    rows = n_tok // GRID_W
    cos, sin = axial_rope_tables(rows)
    silu_c = jax.nn.silu(c)
    silu_cc = jax.nn.silu(c_ctx)
    for l in range(DEPTH):
        last = l == DEPTH - 1
        mod_lat = (silu_c @ w_mod[l] + b_mod[l])[:, None, :]
        mod_ctx = silu_cc @ w_mod[l] + b_mod[l]
        sh1, sc1, g1, sh2, sc2, g2 = jnp.split(mod_lat, N_MOD, axis=-1)
        csh1, csc1, cg1, csh2, csc2, cg2 = jnp.split(mod_ctx, N_MOD, axis=-1)
        h_lat = rms_norm(x, norm1[l]) * (1.0 + sc1) + sh1
        h_ctx = rms_norm(ctx, norm1[l]) * (1.0 + csc1) + csh1
        y_lat, y_ctx = token_mixing(h_lat, h_ctx, not last, w_in[l], gdn_conv[l], gdn_a_log[l], gdn_dt_bias[l],
                                    gdn_out_norm[l], sc_conv[l], sg_ln_w[l], sg_ln_b[l], sg_w[l], sg_b[l],
                                    attn_q_norm[l], attn_k_norm[l], attn_sink[l], w_branch[l], w_out[l], cos, sin)
        x = x + g1 * y_lat
        h2_lat = (rms_norm(x, norm2[l]) * (1.0 + sc2) + sh2).reshape(-1, dm)
        if last:
            f_lat = moe_ffn(h2_lat, router_w[l], router_b[l], exp_w1[l], exp_b1[l], exp_w2[l], exp_b2[l])
        else:
            ctx = ctx + cg1 * y_ctx
            h2_ctx = (rms_norm(ctx, norm2[l]) * (1.0 + csc2) + csh2).reshape(-1, dm)
            n_ctx = h2_ctx.shape[0]
            f = moe_ffn(jnp.concatenate([h2_ctx, h2_lat], axis=0), router_w[l], router_b[l],
                        exp_w1[l], exp_b1[l], exp_w2[l], exp_b2[l])
            ctx = ctx + cg2 * f[:n_ctx].reshape(ctx.shape)
            f_lat = f[n_ctx:]
        x = x + g2 * f_lat.reshape(x.shape)
    return x
```

```python
import functools
import math

import jax
import jax.numpy as jnp
from jax import lax
from jax.experimental import pallas as pl
from jax.experimental.pallas import tpu as pltpu

F32 = jnp.float32
BF16 = jnp.bfloat16
I32 = jnp.int32
U32 = jnp.uint32
HIGHEST = lax.Precision.HIGHEST

D_MODEL = 1024
GRID_W = 64
N_MOD = 6
GDN_HEADS = 4
GDN_DK = 128
GDN_DV = 128
GDN_CONV = 5
SC_WIDTH = 512
SC_CONV = 3
SG_CHUNK = 128
SG_GROUPS = 4
SG_WIDTH = 512
ATT_HEADS = 8
ATT_KV_HEADS = 2
HEAD_DIM = 64
WINDOW = 128
ROPE_BASE = 10000.0
N_BRANCH = 4
BRANCH_W = 512
N_EXPERTS = 32
TOP_K = 4
D_EXPERT = 1024
SWIGLU_ALPHA = 1.702
SWIGLU_LIMIT = 7.0
NORM_EPS = 1e-6
LN_EPS = 1e-5
NEG_INF = -1e30

LANES = 128
SUBLANES = 8
VMEM_LIMIT_BYTES = 56 * 2**20

COL_Q = 0
COL_K = 512
COL_V = 1024
COL_Z = 1536
COL_SB = 2048
COL_SC = 2560
COL_SX = 3072
COL_GU = 3584
COL_GV = 4096
COL_AQ = 4608
COL_MG = 5120
COL_AK = 9216
COL_AV = 9344
COL_AB = 9472
PROJ_COLS = 10240
PROJ_TN = 2048
PROJ_CHUNK = 512

GDN_CHUNK = 64
MOE_BM = 512


def _cparams(sem):
    return pltpu.CompilerParams(dimension_semantics=sem, vmem_limit_bytes=VMEM_LIMIT_BYTES)


def _silu(x):
    return x * jax.nn.sigmoid(x)


def _dot(a, b, precision=None):
    return jnp.dot(a, b, preferred_element_type=F32, precision=precision)


def _dot_nt(a, b):
    return lax.dot_general(a, b, (((1,), (1,)), ((), ())), preferred_element_type=F32)


def _dot_tn(a, b):
    return lax.dot_general(a, b, (((0,), (0,)), ((), ())), preferred_element_type=F32)


def _pack_bf16_pair(lo, hi):
    lo_b = pltpu.bitcast(lo.astype(BF16).astype(F32), U32)
    hi_b = pltpu.bitcast(hi.astype(BF16).astype(F32), U32)
    return (hi_b & jnp.uint32(0xFFFF0000)) | (lo_b >> 16)


def _unpack_bf16_pair(u):
    lo = pltpu.bitcast(u << 16, F32)
    hi = pltpu.bitcast(u & jnp.uint32(0xFFFF0000), F32)
    return lo, hi


def _mod_kernel(c_ref, w_ref, b_ref, o_ref):
    s = _silu(c_ref[...])
    o_ref[...] = _dot(s, w_ref[...], precision=HIGHEST) + b_ref[...]


def _modulation(c_all, w_mod, b_mod):
    depth, dm, nm = w_mod.shape
    rows = c_all.shape[0]
    tn = 1536
    return pl.pallas_call(
        _mod_kernel,
        out_shape=jax.ShapeDtypeStruct((depth, rows, nm), F32),
        grid=(depth, nm // tn),
        in_specs=[
            pl.BlockSpec((rows, dm), lambda l, j: (0, 0)),
            pl.BlockSpec((None, dm, tn), lambda l, j: (l, 0, j)),
            pl.BlockSpec((None, 1, tn), lambda l, j: (l, 0, j)),
        ],
        out_specs=pl.BlockSpec((None, rows, tn), lambda l, j: (l, 0, j)),
        compiler_params=_cparams(("arbitrary", "arbitrary")),
        name="modulation",
    )(c_all, w_mod, b_mod.reshape(depth, 1, nm))


def _in_proj_kernel(x_ref, mod_ref, nw_ref, w_ref, o_ref):
    x = x_ref[...]
    ms = jnp.mean(x * x, axis=-1, keepdims=True)
    y = x * lax.rsqrt(ms + NORM_EPS) * nw_ref[...]
    h = (y * (1.0 + mod_ref[1:2, :]) + mod_ref[0:1, :]).astype(BF16)
    for c in range(PROJ_TN // PROJ_CHUNK):
        sl = slice(c * PROJ_CHUNK, (c + 1) * PROJ_CHUNK)
        o_ref[:, sl] = _dot(h, w_ref[:, sl]).astype(BF16)


def _in_proj(x, mod, norm_w, w_p, l, geo):
    t, dm = x.shape
    tm = geo["tm_proj"]
    mod_row = geo["mod_row"]
    return pl.pallas_call(
        _in_proj_kernel,
        out_shape=jax.ShapeDtypeStruct((t, PROJ_COLS), BF16),
        grid=(PROJ_COLS // PROJ_TN, t // tm),
        in_specs=[
            pl.BlockSpec((tm, dm), lambda j, i: (i, 0)),
            pl.BlockSpec((None, None, N_MOD, dm), lambda j, i: (l, mod_row(i * tm), 0, 0)),
            pl.BlockSpec((None, 1, dm), lambda j, i: (l, 0, 0)),
            pl.BlockSpec((None, dm, PROJ_TN), lambda j, i: (l, 0, j)),
        ],
        out_specs=pl.BlockSpec((tm, PROJ_TN), lambda j, i: (i, j)),
        compiler_params=_cparams(("arbitrary", "arbitrary")),
        name="in_proj",
    )(x, mod, norm_w, w_p)


def _softplus(x):
    return jnp.maximum(x, 0.0) + jnp.log1p(jnp.exp(-jnp.abs(x)))


def _gdn_kernel(alog_ref, dtb_ref,
                ql_ref, kl_ref, vl_ref, zl_ref, abl_ref,
                qc_ref, kc_ref, vc_ref, zc_ref, abc_ref,
                cwq_ref, cwk_ref, cwv_ref, onorm_ref,
                yl_ref, yc_ref,
                xpad, qs, ks, vs, ab32, q16, k16,
                qd16, kw16, kb16, vb16, kdk16, dec, glast, oacc,
                *, n_ctx, n_lat):
    cs = GDN_CHUNK
    hd = GDN_DK
    n_all = n_ctx + n_lat
    n_chunks = n_all // cs
    n_ctx_chunks = n_ctx // cs
    h = pl.program_id(1)
    pad = SUBLANES
    lat_off = n_ctx + 3 * pad

    zeros_pad = jnp.zeros((pad, hd), F32)
    xpad[0:pad, :] = zeros_pad
    xpad[pad + n_ctx:pad + n_ctx + 2 * pad, :] = jnp.zeros((2 * pad, hd), F32)
    xpad[lat_off + n_lat:lat_off + n_lat + pad, :] = zeros_pad

    half = GDN_CONV // 2
    tile = 256

    def conv_into(src_c, src_l, cw_ref, dst, mode):
        xpad[pad:pad + n_ctx, :] = src_c[...].astype(F32)
        xpad[lat_off:lat_off + n_lat, :] = src_l[...].astype(F32)
        w = cw_ref[...]
        for seg_off, dst_off, seg_len in ((pad, 0, n_ctx), (lat_off, n_ctx, n_lat)):
            for r0 in range(0, seg_len, tile):
                acc = jnp.zeros((tile, hd), F32)
                for j in range(GDN_CONV):
                    s = seg_off + r0 + j - half
                    acc = acc + xpad[s:s + tile, :] * w[j:j + 1, :]
                y = _silu(acc)
                if mode != "v":
                    y = y * lax.rsqrt(jnp.sum(y * y, axis=-1, keepdims=True) + NORM_EPS)
                if mode == "q":
                    y = y * (hd ** -0.5)
                dst[dst_off + r0:dst_off + r0 + tile, :] = y

    conv_into(qc_ref, ql_ref, cwq_ref, qs, "q")
    conv_into(kc_ref, kl_ref, cwk_ref, ks, "k")
    conv_into(vc_ref, vl_ref, cwv_ref, vs, "v")
    ab32[0:n_ctx, :] = abc_ref[...].astype(F32)
    ab32[n_ctx:n_all, :] = abl_ref[...].astype(F32)

    ri = lax.broadcasted_iota(I32, (cs, cs), 0)
    ci = lax.broadcasted_iota(I32, (cs, cs), 1)
    low = (ri >= ci)
    up = (ri <= ci)
    low_f = low.astype(F32)
    up_f = up.astype(F32)
    ones_cc = jnp.ones((cs, cs), F32)
    sel_r = lax.broadcasted_iota(I32, (LANES, LANES), 0)

    def prep(c, carry):
        r0 = pl.multiple_of(c * cs, cs)
        rows = pl.ds(r0, cs)
        ab = ab32[rows, :].astype(BF16)
        q = qs[rows, :]
        k = ks[rows, :]
        v = vs[rows, :]
        q16[rows, :] = q.astype(BF16)
        k16[rows, :] = k.astype(BF16)
        for d in range(2):
            col = d * GDN_HEADS + h
            a_b = _dot(ab, (sel_r == col).astype(BF16))
            b_b = _dot(ab, (sel_r == (2 * GDN_HEADS + col)).astype(BF16))
            neg_a = -jnp.exp(jnp.full((1, LANES), alog_ref[d, h], F32))
            g_b = neg_a * _softplus(a_b + dtb_ref[d, h])
            beta_b = jax.nn.sigmoid(b_b)
            cum_m, msk_f, msk = (low_f, up_f, low) if d == 0 else (up_f, low_f, up)
            last = cs - 1 if d == 0 else 0
            gc_col = _dot(cum_m, g_b, precision=HIGHEST)
            gc_row = _dot(ones_cc, g_b[:, :cs] * msk_f, precision=HIGHEST)
            gl = gc_col[last:last + 1, :]
            eg = jnp.exp(gc_col)
            dec[d, rows, :] = jnp.where(msk, jnp.exp(gc_col[:, :cs] - gc_row), 0.0)
            kb = k * beta_b
            qd16[d, rows, :] = (q * eg).astype(BF16)
            kw16[d, rows, :] = (kb * eg).astype(BF16)
            kb16[d, rows, :] = kb.astype(BF16)
            vb16[d, rows, :] = (v * beta_b).astype(BF16)
            kdk16[d, rows, :] = (k * jnp.exp(gl - gc_col)).astype(BF16)
            g0 = pl.multiple_of(c * SUBLANES, SUBLANES)
            glast[d, pl.ds(g0, SUBLANES), :] = jnp.broadcast_to(jnp.exp(gl), (SUBLANES, LANES))
        return carry

    lax.fori_loop(0, n_chunks, prep, 0)

    oacc[...] = jnp.zeros_like(oacc)
    eye = (ri == ci).astype(F32)
    offdiag = (ri != ci).astype(F32)

    def chunk_step(d, c, s):
        r0 = pl.multiple_of(c * cs, cs)
        rows = pl.ds(r0, cs)
        kk = k16[rows, :]
        dmat = dec[d, rows, :]
        n = -(_dot_nt(kb16[d, rows, :], kk) * (dmat * offdiag))
        t = eye + n
        p = n
        for _ in range(int(math.log2(cs)) - 1):
            pb = p.astype(BF16)
            p = _dot(pb, pb)
            t = t + _dot(p.astype(BF16), t.astype(BF16))
        tb = t.astype(BF16)
        u = _dot(tb, vb16[d, rows, :])
        w = _dot(tb, kw16[d, rows, :])
        qk = _dot_nt(q16[rows, :], kk) * dmat
        sb = s.astype(BF16)
        v_new = u - _dot(w.astype(BF16), sb)
        vnb = v_new.astype(BF16)
        o = _dot(qd16[d, rows, :], sb) + _dot(qk.astype(BF16), vnb)
        g0 = pl.multiple_of(c * SUBLANES, SUBLANES)
        gl = glast[d, pl.ds(g0, 1), :]
        s_new = s * gl + _dot_tn(kdk16[d, rows, :], vnb)
        oacc[rows, :] = oacc[rows, :] + o
        return s_new

    def step(i, carry):
        s_f, s_b = carry
        s_f = chunk_step(0, i, s_f)
        c_b = jnp.where(i < n_ctx_chunks, n_ctx_chunks - 1 - i, n_chunks + n_ctx_chunks - 1 - i)
        s_b = chunk_step(1, c_b, s_b)
        return s_f, s_b

    s0 = jnp.zeros((hd, GDN_DV), F32)
    lax.fori_loop(0, n_chunks, step, (s0, s0))

    onw = onorm_ref[...]

    def out_gate(o, z):
        y = o * lax.rsqrt(jnp.mean(o * o, axis=-1, keepdims=True) + NORM_EPS) * onw
        return (y * _silu(z.astype(F32))).astype(BF16)

    yc_ref[...] = out_gate(oacc[0:n_ctx, :], zc_ref[...])
    for r0 in range(0, n_lat, tile):
        yl_ref[r0:r0 + tile, :] = out_gate(oacc[n_ctx + r0:n_ctx + r0 + tile, :], zl_ref[r0:r0 + tile, :])


def _gdn(p, conv_w, a_log, dt_bias, out_norm, l, geo):
    b, n_ctx, n_lat = geo["b"], geo["ctx"], geo["lat"]
    n_all = n_ctx + n_lat
    lat0 = geo["tc"] // n_lat
    hd = GDN_DK

    def lat_spec(col):
        return pl.BlockSpec((n_lat, hd), lambda bi, h: (lat0 + bi, col // hd + h))

    def ctx_spec(col):
        return pl.BlockSpec((n_ctx, hd), lambda bi, h: (bi, col // hd + h))

    def fixed_lat(col):
        return pl.BlockSpec((n_lat, hd), lambda bi, h: (lat0 + bi, col // hd))

    def fixed_ctx(col):
        return pl.BlockSpec((n_ctx, hd), lambda bi, h: (bi, col // hd))

    def cw_spec(col):
        return pl.BlockSpec((None, GDN_CONV, hd), lambda bi, h: (l, 0, col // hd + h))

    smem = pl.BlockSpec(memory_space=pltpu.SMEM)
    kern = functools.partial(_gdn_kernel, n_ctx=n_ctx, n_lat=n_lat)
    n_chunks = n_all // GDN_CHUNK
    scratch = [
        pltpu.VMEM((n_all + 4 * SUBLANES, hd), F32),
        pltpu.VMEM((n_all, hd), F32),
        pltpu.VMEM((n_all, hd), F32),
        pltpu.VMEM((n_all, hd), F32),
        pltpu.VMEM((n_all, LANES), F32),
        pltpu.VMEM((n_all, hd), BF16),
        pltpu.VMEM((n_all, hd), BF16),
        pltpu.VMEM((2, n_all, hd), BF16),
        pltpu.VMEM((2, n_all, hd), BF16),
        pltpu.VMEM((2, n_all, hd), BF16),
        pltpu.VMEM((2, n_all, hd), BF16),
        pltpu.VMEM((2, n_all, hd), BF16),
        pltpu.VMEM((2, n_all, GDN_CHUNK), F32),
        pltpu.VMEM((2, n_chunks * SUBLANES, LANES), F32),
        pltpu.VMEM((n_all, GDN_DV), F32),
    ]
    y_lat, y_ctx = pl.pallas_call(
        kern,
        out_shape=(jax.ShapeDtypeStruct((b * n_lat, GDN_HEADS * GDN_DV), BF16),
                   jax.ShapeDtypeStruct((b * n_ctx, GDN_HEADS * GDN_DV), BF16)),
        grid=(b, GDN_HEADS),
        in_specs=[smem, smem,
                  lat_spec(COL_Q), lat_spec(COL_K), lat_spec(COL_V), lat_spec(COL_Z), fixed_lat(COL_AB),
                  ctx_spec(COL_Q), ctx_spec(COL_K), ctx_spec(COL_V), ctx_spec(COL_Z), fixed_ctx(COL_AB),
                  cw_spec(0), cw_spec(GDN_HEADS * hd), cw_spec(2 * GDN_HEADS * hd),
                  pl.BlockSpec((None, 1, GDN_DV), lambda bi, h: (l, 0, 0))],
        out_specs=(pl.BlockSpec((n_lat, GDN_DV), lambda bi, h: (bi, h)),
                   pl.BlockSpec((n_ctx, GDN_DV), lambda bi, h: (bi, h))),
        scratch_shapes=scratch,
        compiler_params=_cparams(("arbitrary", "arbitrary")),
        name="gdn",
    )(a_log[l], dt_bias[l], p, p, p, p, p, p, p, p, p, p, conv_w, conv_w, conv_w, out_norm)
    return y_lat, y_ctx


def _head_rms(x, bd, w):
    ss = _dot(x * x, bd, precision=HIGHEST) * (1.0 / HEAD_DIM)
    return x * lax.rsqrt(ss + NORM_EPS) * w


def _rope(x, cos, sin_signed, lane):
    nf = HEAD_DIM // 4
    partner = jnp.where((lane % (2 * nf)) < nf, pltpu.roll(x, LANES - nf, 1), pltpu.roll(x, nf, 1))
    return x * cos + partner * sin_signed


def _attn_kernel(sink_ref, q_ref, kq_ref, vq_ref, kc_ref, vc_ref,
                 cosq_ref, sinq_ref, cosk_ref, sink_k_ref, qw_ref, kw_ref,
                 o_ref, kp, vp, kcp, vcp, *, n_keys, n_ctx, local):
    n = pl.program_id(1)
    w = WINDOW
    lane = lax.broadcasted_iota(I32, (1, LANES), 1)
    lo_mask = (lane < HEAD_DIM).astype(F32)
    hi_mask = 1.0 - lo_mask
    bd_r = lax.broadcasted_iota(I32, (LANES, LANES), 0) // HEAD_DIM
    bd_c = lax.broadcasted_iota(I32, (LANES, LANES), 1) // HEAD_DIM
    bd = (bd_r == bd_c).astype(F32)
    kw = kw_ref[...]

    def store_variants(dst, row0, x):
        xr = pltpu.roll(x, HEAD_DIM, 1)
        rows = x.shape[0]
        dst[0, row0:row0 + rows, :] = (x * lo_mask).astype(BF16)
        dst[1, row0:row0 + rows, :] = (xr * hi_mask).astype(BF16)
        dst[2, row0:row0 + rows, :] = (xr * lo_mask).astype(BF16)
        dst[3, row0:row0 + rows, :] = (x * hi_mask).astype(BF16)

    @pl.when(n == 0)
    def _prepare_keys():
        tile = 256
        for r0 in range(0, n_ctx, tile):
            rr = min(tile, n_ctx - r0)
            kc = _head_rms(kc_ref[r0:r0 + rr, :].astype(F32), bd, kw)
            store_variants(kcp, r0, kc)
            store_variants(vcp, r0, vc_ref[r0:r0 + rr, :].astype(F32))
        if local:
            zero = jnp.zeros((w, LANES), BF16)
            for t in range(4):
                kp[t, 0:w, :] = zero
                kp[t, w + n_keys:2 * w + n_keys, :] = zero
                vp[t, 0:w, :] = zero
                vp[t, w + n_keys:2 * w + n_keys, :] = zero
            for r0 in range(0, n_keys, tile):
                k = _head_rms(kq_ref[r0:r0 + tile, :].astype(F32), bd, kw)
                k = _rope(k, cosk_ref[r0:r0 + tile, :], sink_k_ref[r0:r0 + tile, :], lane)
                store_variants(kp, w + r0, k)
                store_variants(vp, w + r0, vq_ref[r0:r0 + tile, :].astype(F32))

    qw = qw_ref[...]
    q2 = []
    for g in range(ATT_HEADS // 2):
        qg = _head_rms(q_ref[:, g * LANES:(g + 1) * LANES].astype(F32), bd, qw)
        if local:
            qg = _rope(qg, cosq_ref[...], sinq_ref[...], lane)
        q2.append((qg * (HEAD_DIM ** -0.5)).astype(BF16))

    rows2 = 2 * w
    row_i = lax.broadcasted_iota(I32, (rows2, 1), 0)
    first = row_i < w
    if local:
        n_blk = n_keys // w
        qq = lax.broadcasted_iota(I32, (rows2, 3 * w), 0) % w
        kcol = lax.broadcasted_iota(I32, (rows2, 3 * w), 1)
        kk = kcol % w
        blk = kcol // w
        valid = ((blk == 1)
                 | ((blk == 0) & (kk >= qq) & (n >= 1))
                 | ((blk == 2) & (kk <= qq) & (n < n_blk - 1)))
        k0 = pl.multiple_of(n * w, w)

    for j in range(ATT_KV_HEADS):
        lhs = jnp.concatenate([q2[2 * j], q2[2 * j + 1]], axis=0)
        pair_out = [None, None]
        for t in range(2):
            var = 2 * j + t
            sink = jnp.where(first, sink_ref[4 * j + t], sink_ref[4 * j + 2 + t])
            s_ctx = _dot_nt(lhs, kcp[var])
            m = jnp.maximum(jnp.max(s_ctx, axis=-1, keepdims=True), sink)
            if local:
                s_loc = _dot_nt(lhs, kp[var, pl.ds(k0, 3 * w), :])
                s_loc = jnp.where(valid, s_loc, NEG_INF)
                m = jnp.maximum(m, jnp.max(s_loc, axis=-1, keepdims=True))
            e_ctx = jnp.exp(s_ctx - m)
            den = jnp.sum(e_ctx, axis=-1, keepdims=True) + jnp.exp(sink - m)
            pv = _dot(e_ctx.astype(BF16), vcp[var])
            if local:
                e_loc = jnp.exp(s_loc - m)
                den = den + jnp.sum(e_loc, axis=-1, keepdims=True)
                pv = pv + _dot(e_loc.astype(BF16), vp[var, pl.ds(k0, 3 * w), :])
            o = pv / den
            for half_i in range(2):
                part = o[half_i * w:(half_i + 1) * w, :]
                pair_out[half_i] = part if pair_out[half_i] is None else pair_out[half_i] + part
        for half_i in range(2):
            c0 = (2 * j + half_i) * LANES
            o_ref[:, c0:c0 + LANES] = pair_out[half_i].astype(BF16)


def _attention(p, sink, q_norm_w, k_norm_w, rope_cos, rope_sin, l, geo, ctx_queries):
    b, n_ctx, n_lat = geo["b"], geo["ctx"], geo["lat"]
    w = WINDOW
    lat0 = geo["tc"] // n_lat
    local = not ctx_queries
    n_q = n_ctx if ctx_queries else n_lat
    nqb = n_q // w
    q_row0 = 0 if ctx_queries else geo["tc"] // w
    kern = functools.partial(_attn_kernel, n_keys=n_lat, n_ctx=n_ctx, local=local)
    in_specs = [
        pl.BlockSpec(memory_space=pltpu.SMEM),
        pl.BlockSpec((w, ATT_HEADS * HEAD_DIM), lambda bi, n: (q_row0 + bi * nqb + n, COL_AQ // 512)),
        pl.BlockSpec((n_lat, LANES), lambda bi, n: (lat0 + bi, COL_AK // LANES)),
        pl.BlockSpec((n_lat, LANES), lambda bi, n: (lat0 + bi, COL_AV // LANES)),
        pl.BlockSpec((n_ctx, LANES), lambda bi, n: (bi, COL_AK // LANES)),
        pl.BlockSpec((n_ctx, LANES), lambda bi, n: (bi, COL_AV // LANES)),
        pl.BlockSpec((w, LANES), lambda bi, n: (n if local else 0, 0)),
        pl.BlockSpec((w, LANES), lambda bi, n: (n if local else 0, 0)),
        pl.BlockSpec((n_lat, LANES), lambda bi, n: (0, 0)),
        pl.BlockSpec((n_lat, LANES), lambda bi, n: (0, 0)),
        pl.BlockSpec((None, 1, LANES), lambda bi, n: (l, 0, 0)),
        pl.BlockSpec((None, 1, LANES), lambda bi, n: (l, 0, 0)),
    ]
    scratch = [
        pltpu.VMEM((4, n_lat + 2 * w, LANES), BF16),
        pltpu.VMEM((4, n_lat + 2 * w, LANES), BF16),
        pltpu.VMEM((4, n_ctx, LANES), BF16),
        pltpu.VMEM((4, n_ctx, LANES), BF16),
    ]
    return pl.pallas_call(
        kern,
        out_shape=jax.ShapeDtypeStruct((b * n_q, ATT_HEADS * HEAD_DIM), BF16),
        grid=(b, nqb),
        in_specs=in_specs,
        out_specs=pl.BlockSpec((w, ATT_HEADS * HEAD_DIM), lambda bi, n: (bi * nqb + n, 0)),
        scratch_shapes=scratch,
        compiler_params=_cparams(("arbitrary", "arbitrary")),
        name="attn_ctx" if ctx_queries else "attn_lat",
    )(sink[l], p, p, p, p, p, rope_cos, rope_sin, rope_cos, rope_sin, q_norm_w, k_norm_w)


def _merge_kernel(x_ref, mod_ref, n2_ref, y0_ref, y3_ref,
                  sb_ref, sc_ref, sx_ref, scp_ref, sxp_ref, scn_ref, sxn_ref,
                  gu_ref, gv_ref, mg0_ref, mg1_ref, mg2_ref, mg3_ref,
                  scw_ref, lnw_ref, lnb_ref, sgw_ref, sgb_ref, wb_ref, wo_ref, rw_ref, rb_ref,
                  xo_ref, h2_ref, idx_ref, gate_ref, rank_ref, cnt_ref,
                  carry, *, tm, tc, n_ctx, n_lat):
    i = pl.program_id(0)
    r0 = i * tm

    row = lax.broadcasted_iota(I32, (tm, 1), 0)
    g_row = r0 + row
    in_ctx = g_row < tc
    seg_pos = jnp.where(in_ctx, g_row % n_ctx, (g_row - tc) % n_lat)
    seg_len = jnp.where(in_ctx, n_ctx, n_lat)
    cx = sc_ref[...].astype(F32) * sx_ref[...].astype(F32)
    cx_prev_halo = scp_ref[SUBLANES - 1:SUBLANES, :].astype(F32) * sxp_ref[SUBLANES - 1:SUBLANES, :].astype(F32)
    cx_next_halo = scn_ref[0:1, :].astype(F32) * sxn_ref[0:1, :].astype(F32)
    prev = jnp.where(row == 0, cx_prev_halo, pltpu.roll(cx, 1, 0))
    prev = jnp.where(seg_pos == 0, 0.0, prev)
    nxt = jnp.where(row == tm - 1, cx_next_halo, pltpu.roll(cx, tm - 1, 0))
    nxt = jnp.where(seg_pos == seg_len - 1, 0.0, nxt)
    scw = scw_ref[...]
    y1 = sb_ref[...].astype(F32) * (prev * scw[0:1, :] + cx * scw[1:2, :] + nxt * scw[2:3, :])

    inv_sqrt2 = 1.0 / math.sqrt(2.0)

    def gelu(t):
        return 0.5 * t * (1.0 + lax.erf(t * inv_sqrt2))

    u = gelu(gu_ref[...].astype(F32))
    v = gelu(gv_ref[...].astype(F32))
    mu = jnp.mean(v, axis=-1, keepdims=True)
    vc = v - mu
    v = vc * lax.rsqrt(jnp.mean(vc * vc, axis=-1, keepdims=True) + LN_EPS) * lnw_ref[...] + lnb_ref[...]
    vb = v.astype(BF16)
    gw = SG_WIDTH // SG_GROUPS
    chunks = []
    for c in range(tm // SG_CHUNK):
        groups = []
        for g in range(SG_GROUPS):
            groups.append(_dot(sgw_ref[g], vb[c * SG_CHUNK:(c + 1) * SG_CHUNK, g * gw:(g + 1) * gw]))
        chunks.append(jnp.concatenate(groups, axis=1) + sgb_ref[...])
    y2 = u * jnp.concatenate(chunks, axis=0)

    ys = (y0_ref[...], y1.astype(BF16), y2.astype(BF16), y3_ref[...])
    gates = (mg0_ref, mg1_ref, mg2_ref, mg3_ref)
    m = None
    for br in range(N_BRANCH):
        term = jax.nn.sigmoid(gates[br][...].astype(F32)) * _dot(ys[br], wb_ref[br])
        m = term if m is None else m + term
    y = _dot(m.astype(BF16), wo_ref[...])
    x_new = x_ref[...] + mod_ref[2:3, :] * y
    xo_ref[...] = x_new

    ms = jnp.mean(x_new * x_new, axis=-1, keepdims=True)
    h2 = x_new * lax.rsqrt(ms + NORM_EPS) * n2_ref[...] * (1.0 + mod_ref[4:5, :]) + mod_ref[3:4, :]
    half = D_MODEL // 2
    h2_ref[...] = _pack_bf16_pair(h2[:, :half], h2[:, half:])

    logits = _dot(h2, rw_ref[...], precision=HIGHEST) + rb_ref[...]
    lane = lax.broadcasted_iota(I32, (tm, LANES), 1)
    lane_f = lane.astype(F32)
    work = logits
    topv = jnp.full((tm, LANES), NEG_INF, F32)
    topi = jnp.zeros((tm, LANES), I32)
    onehot = jnp.zeros((tm, LANES), F32)
    firsts = []
    for k in range(TOP_K):
        mx = jnp.max(work, axis=-1, keepdims=True)
        first_f = jnp.min(jnp.where(work == mx, lane_f, float(LANES)), axis=-1, keepdims=True)
        first = first_f.astype(I32)
        hit = lane == first
        topv = jnp.where(lane == k, mx, topv)
        topi = jnp.where(lane == k, first, topi)
        onehot = jnp.where(hit, 1.0, onehot)
        work = jnp.where(hit, -jnp.inf, work)
        firsts.append(first)
    e = jnp.where(lane < TOP_K, jnp.exp(topv - jnp.max(topv, axis=-1, keepdims=True)), 0.0)
    gate = e / jnp.sum(e, axis=-1, keepdims=True)

    @pl.when(i == 0)
    def _init():
        carry[...] = jnp.zeros_like(carry)

    tri = (lax.broadcasted_iota(I32, (tm, tm), 0) > lax.broadcasted_iota(I32, (tm, tm), 1)).astype(BF16)
    before = carry[0:1, :] + _dot(tri, onehot.astype(BF16))
    rank = jnp.zeros((tm, LANES), F32)
    for k in range(TOP_K):
        r_k = jnp.sum(jnp.where(lane == firsts[k], before, 0.0), axis=-1, keepdims=True)
        rank = jnp.where(lane == k, r_k, rank)
    new_carry = carry[0:1, :] + jnp.sum(onehot, axis=0, keepdims=True)
    carry[...] = jnp.broadcast_to(new_carry, carry.shape)
    cnt_ref[...] = jnp.broadcast_to(new_carry, cnt_ref.shape).astype(I32)
    idx_ref[...] = topi[:, :SUBLANES]
    gate_ref[...] = gate[:, :SUBLANES]
    rank_ref[...] = rank[:, :SUBLANES].astype(I32)


def _merge(x, mod, norm2, y0, y3, p, w, l, geo):
    t, dm = x.shape
    tm = geo["tm_merge"]
    mod_row = geo["mod_row"]
    kern = functools.partial(_merge_kernel, tm=tm, tc=geo["tc"], n_ctx=geo["ctx"], n_lat=geo["lat"])
    halo = tm // SUBLANES
    n_halo = t // SUBLANES

    def col(c, width=512):
        return pl.BlockSpec((tm, width), lambda i: (i, c // width))

    def prev_halo(c):
        return pl.BlockSpec((SUBLANES, 512), lambda i: (jnp.maximum(i * halo - 1, 0), c // 512))

    def next_halo(c):
        return pl.BlockSpec((SUBLANES, 512), lambda i: (jnp.minimum((i + 1) * halo, n_halo - 1), c // 512))

    def layer(shape):
        nd = len(shape)
        return pl.BlockSpec((None,) + shape, lambda i: (l,) + (0,) * nd)

    in_specs = [
        pl.BlockSpec((tm, dm), lambda i: (i, 0)),
        pl.BlockSpec((None, None, N_MOD, dm), lambda i: (l, mod_row(i * tm), 0, 0)),
        layer((1, dm)),
        pl.BlockSpec((tm, 512), lambda i: (i, 0)),
        pl.BlockSpec((tm, 512), lambda i: (i, 0)),
        col(COL_SB), col(COL_SC), col(COL_SX),
        prev_halo(COL_SC), prev_halo(COL_SX), next_halo(COL_SC), next_halo(COL_SX),
        col(COL_GU), col(COL_GV),
        col(COL_MG, 1024), col(COL_MG + 1024, 1024), col(COL_MG + 2048, 1024), col(COL_MG + 3072, 1024),
        layer((SC_CONV, SC_WIDTH)), layer((1, SG_WIDTH)), layer((1, SG_WIDTH)),
        layer((SG_GROUPS, SG_CHUNK, SG_CHUNK)), layer((SG_CHUNK, SG_WIDTH)),
        layer((N_BRANCH, BRANCH_W, dm)), layer((dm, dm)), layer((dm, LANES)), layer((1, LANES)),
    ]
    out_shape = (
        jax.ShapeDtypeStruct((t, dm), F32),
        jax.ShapeDtypeStruct((t, dm // 2), U32),
        jax.ShapeDtypeStruct((t, SUBLANES), I32),
        jax.ShapeDtypeStruct((t, SUBLANES), F32),
        jax.ShapeDtypeStruct((t, SUBLANES), I32),
        jax.ShapeDtypeStruct((SUBLANES, LANES), I32),
    )
    out_specs = (
        pl.BlockSpec((tm, dm), lambda i: (i, 0)),
        pl.BlockSpec((tm, dm // 2), lambda i: (i, 0)),
        pl.BlockSpec((tm, SUBLANES), lambda i: (i, 0)),
        pl.BlockSpec((tm, SUBLANES), lambda i: (i, 0)),
        pl.BlockSpec((tm, SUBLANES), lambda i: (i, 0)),
        pl.BlockSpec((SUBLANES, LANES), lambda i: (0, 0)),
    )
    return pl.pallas_call(
        kern,
        out_shape=out_shape,
        grid=(t // tm,),
        in_specs=in_specs,
        out_specs=out_specs,
        scratch_shapes=[pltpu.VMEM((SUBLANES, LANES), F32)],
        compiler_params=_cparams(("arbitrary",)),
        name="merge",
    )(x, mod, norm2, y0, y3, p, p, p, p, p, p, p, p, p, p, p, p, p,
      w["sc_conv"], w["sg_ln_w"], w["sg_ln_b"], w["sg_w"], w["sg_b"], w["w_branch"], w["w_out"],
      w["router_w"], w["router_b"])


def _dispatch_kernel(dest_ref, h_ref, xs_in_ref, xs_ref, sem, *, tm):
    del xs_in_ref

    def row_copy(t, d):
        return pltpu.make_async_copy(h_ref.at[pl.ds(t, 1), :], xs_ref.at[pl.ds(d, 1), :], sem)

    def issue(t, carry):
        for k in range(TOP_K):
            row_copy(t, dest_ref[t * TOP_K + k]).start()
        return carry

    lax.fori_loop(0, tm, issue, 0)
    for _ in range(TOP_K):
        pltpu.make_async_copy(h_ref, xs_ref.at[pl.ds(0, tm), :], sem).wait()


def _dispatch(h2p, dest_flat, n_slots, geo):
    t, half = h2p.shape
    tm = geo["tm_moe"]
    xs0 = jnp.zeros((n_slots, half), U32)
    return pl.pallas_call(
        functools.partial(_dispatch_kernel, tm=tm),
        out_shape=jax.ShapeDtypeStruct((n_slots, half), U32),
        grid=(t // tm,),
        in_specs=[
            pl.BlockSpec((tm * TOP_K,), lambda i: (i,), memory_space=pltpu.SMEM),
            pl.BlockSpec((tm, half), lambda i: (i, 0)),
            pl.BlockSpec(memory_space=pl.ANY),
        ],
        out_specs=pl.BlockSpec(memory_space=pl.ANY),
        scratch_shapes=[pltpu.SemaphoreType.DMA],
        input_output_aliases={2: 0},
        compiler_params=_cparams(("arbitrary",)),
        name="moe_dispatch",
    )(dest_flat, h2p, xs0)


def _expert_kernel(be_ref, xs_ref, w1_ref, b1_ref, w2_ref, b2_ref, ys_ref):
    del be_ref
    lo, hi = _unpack_bf16_pair(xs_ref[...])
    half = D_MODEL // 2
    h = (_dot(lo.astype(BF16), w1_ref[0:half, :]) + _dot(hi.astype(BF16), w1_ref[half:, :]) + b1_ref[...])
    glu = jnp.minimum(h[:, :D_EXPERT], SWIGLU_LIMIT)
    lin = jnp.clip(h[:, D_EXPERT:], -SWIGLU_LIMIT, SWIGLU_LIMIT)
    act = glu * jax.nn.sigmoid(SWIGLU_ALPHA * glu) * (lin + 1.0)
    y = _dot(act.astype(BF16), w2_ref[...]) + b2_ref[...]
    ys_ref[...] = _pack_bf16_pair(y[:, :half], y[:, half:])


def _experts(xs, block_expert, w1p, b1p, w2, b2, l):
    n_slots, half = xs.shape
    bm = MOE_BM
    grid_spec = pltpu.PrefetchScalarGridSpec(
        num_scalar_prefetch=1,
        grid=(n_slots // bm,),
        in_specs=[
            pl.BlockSpec((bm, half), lambda i, be: (i, 0)),
            pl.BlockSpec((None, None, D_MODEL, 2 * D_EXPERT), lambda i, be: (l, be[i], 0, 0)),
            pl.BlockSpec((None, None, 1, 2 * D_EXPERT), lambda i, be: (l, be[i], 0, 0)),
            pl.BlockSpec((None, None, D_EXPERT, D_MODEL), lambda i, be: (l, be[i], 0, 0)),
            pl.BlockSpec((None, None, 1, D_MODEL), lambda i, be: (l, be[i], 0, 0)),
        ],
        out_specs=pl.BlockSpec((bm, half), lambda i, be: (i, 0)),
    )
    return pl.pallas_call(
        _expert_kernel,
        out_shape=jax.ShapeDtypeStruct((n_slots, half), U32),
        grid_spec=grid_spec,
        compiler_params=_cparams(("arbitrary",)),
        name="moe_experts",
    )(block_expert, xs, w1p, b1p, w2, b2)


def _combine_kernel(dest_ref, gate_ref, x_ref, mod_ref, ys_ref, xo_ref, buf, sem, *, tm):
    def row_copy(t, k, d):
        return pltpu.make_async_copy(ys_ref.at[pl.ds(d, 1), :], buf.at[k, pl.ds(t, 1), :], sem)

    def issue(t, carry):
        for k in range(TOP_K):
            row_copy(t, k, dest_ref[t * TOP_K + k]).start()
        return carry

    lax.fori_loop(0, tm, issue, 0)
    for k in range(TOP_K):
        pltpu.make_async_copy(ys_ref.at[pl.ds(0, tm), :], buf.at[k], sem).wait()

    gate = gate_ref[...]
    half = D_MODEL // 2
    f_lo = jnp.zeros((tm, half), F32)
    f_hi = jnp.zeros((tm, half), F32)
    for k in range(TOP_K):
        lo, hi = _unpack_bf16_pair(buf[k])
        g = gate[:, k:k + 1]
        f_lo = f_lo + g * lo
        f_hi = f_hi + g * hi
    g2 = mod_ref[5:6, :]
    xo_ref[:, :half] = x_ref[:, :half] + g2[:, :half] * f_lo
    xo_ref[:, half:] = x_ref[:, half:] + g2[:, half:] * f_hi


def _combine(x, mod, gate, dest_flat, ys, l, geo):
    t, dm = x.shape
    tm = geo["tm_moe"]
    mod_row = geo["mod_row"]
    return pl.pallas_call(
        functools.partial(_combine_kernel, tm=tm),
        out_shape=jax.ShapeDtypeStruct((t, dm), F32),
        grid=(t // tm,),
        in_specs=[
            pl.BlockSpec((tm * TOP_K,), lambda i: (i,), memory_space=pltpu.SMEM),
            pl.BlockSpec((tm, SUBLANES), lambda i: (i, 0)),
            pl.BlockSpec((tm, dm), lambda i: (i, 0)),
            pl.BlockSpec((None, None, N_MOD, dm), lambda i: (l, mod_row(i * tm), 0, 0)),
            pl.BlockSpec(memory_space=pl.ANY),
        ],
        out_specs=pl.BlockSpec((tm, dm), lambda i: (i, 0)),
        scratch_shapes=[pltpu.VMEM((TOP_K, tm, dm // 2), U32), pltpu.SemaphoreType.DMA],
        compiler_params=_cparams(("arbitrary",)),
        name="moe_combine",
    )(dest_flat, gate, x, mod, ys)


def _moe_plan(idx, rank, counts, n_blocks):
    bm = MOE_BM
    padded = (counts + bm - 1) // bm * bm
    pad_end = jnp.cumsum(padded)
    pad_start = pad_end - padded
    expert = idx[:, :TOP_K]
    onehot = expert[:, :, None] == jnp.arange(N_EXPERTS, dtype=I32)[None, None, :]
    dest = rank[:, :TOP_K] + jnp.sum(jnp.where(onehot, pad_start[None, None, :], 0), axis=-1)
    block_start = jnp.arange(n_blocks, dtype=I32) * bm
    block_expert = jnp.minimum(
        jnp.sum((block_start[:, None] >= pad_end[None, :]).astype(I32), axis=-1), N_EXPERTS - 1)
    return dest.reshape(-1).astype(I32), block_expert.astype(I32)


def _rope_tables(n_lat):
    rows = n_lat // GRID_W
    row = jnp.repeat(jnp.arange(rows, dtype=F32), GRID_W)
    colp = jnp.tile(jnp.arange(GRID_W, dtype=F32), rows)
    n_freq = HEAD_DIM // 4
    inv = ROPE_BASE ** (-jnp.arange(n_freq, dtype=F32) / n_freq)
    ang = jnp.concatenate([row[:, None] * inv, colp[:, None] * inv], axis=-1)
    cos, sin = jnp.cos(ang), jnp.sin(ang)
    cr, cc = cos[:, :n_freq], cos[:, n_freq:]
    sr, sc = sin[:, :n_freq], sin[:, n_freq:]
    cos64 = jnp.concatenate([cr, cr, cc, cc], axis=-1)
    sin64 = jnp.concatenate([-sr, sr, -sc, sc], axis=-1)
    return jnp.tile(cos64, (1, 2)), jnp.tile(sin64, (1, 2))


def _prep_weights(w_in, w_branch, w_out, router_w, router_b, sg_b, exp_w1, exp_b1, exp_w2, exp_b2,
                  attn_q_norm, attn_k_norm):
    depth, dm, _ = w_in.shape
    qkvz = 2048
    ab0, sc0, sg0, at0, mg0 = 2048, 2064, 3600, 4624, 5392
    pad = PROJ_COLS - (COL_AB + 16)
    w_p = jnp.concatenate([
        w_in[:, :, 0:qkvz],
        w_in[:, :, sc0:sg0],
        w_in[:, :, sg0:at0],
        w_in[:, :, at0:at0 + 512],
        w_in[:, :, mg0:mg0 + 4096],
        w_in[:, :, at0 + 512:at0 + 768],
        w_in[:, :, ab0:ab0 + 16],
        jnp.zeros((depth, dm, pad), w_in.dtype),
    ], axis=-1).astype(BF16)
    rw = jnp.concatenate([router_w, jnp.zeros((depth, dm, LANES - N_EXPERTS), F32)], axis=-1)
    rb = jnp.concatenate([router_b, jnp.full((depth, LANES - N_EXPERTS), NEG_INF, F32)], axis=-1)
    w1p = jnp.concatenate([exp_w1[..., 0::2], exp_w1[..., 1::2]], axis=-1).astype(BF16)
    b1p = jnp.concatenate([exp_b1[..., 0::2], exp_b1[..., 1::2]], axis=-1)
    return {
        "w_in": w_p,
        "w_branch": w_branch.astype(BF16),
        "w_out": w_out.astype(BF16),
        "router_w": rw,
        "router_b": rb[:, None, :],
        "sg_b": jnp.repeat(jnp.swapaxes(sg_b, 1, 2), SG_WIDTH // SG_GROUPS, axis=2),
        "w1": w1p,
        "b1": b1p[:, :, None, :],
        "w2": exp_w2.astype(BF16),
        "b2": exp_b2[:, :, None, :],
        "q_norm": jnp.tile(attn_q_norm, (1, 2))[:, None, :],
        "k_norm": jnp.tile(attn_k_norm, (1, 2))[:, None, :],
    }


def kernel(x, c, ctx, c_ctx, norm1, norm2, w_mod, b_mod, w_in, gdn_conv, gdn_a_log, gdn_dt_bias, gdn_out_norm,
           sc_conv, sg_ln_w, sg_ln_b, sg_w, sg_b, attn_q_norm, attn_k_norm, attn_sink, w_branch, w_out,
           router_w, router_b, exp_w1, exp_b1, exp_w2, exp_b2):
    b, n_lat, dm = x.shape
    n_ctx = ctx.shape[1]
    depth = w_in.shape[0]
    tc = b * n_ctx
    t = tc + b * n_lat
    assert dm == D_MODEL and tc % n_lat == 0 and n_lat % 256 == 0 and n_ctx % 256 == 0
    tile_cap = math.gcd(tc, n_lat)

    def mod_row(r0):
        return jnp.where(r0 < tc, 0, 1 + (r0 - tc) // n_lat)

    geo = {
        "b": b, "ctx": n_ctx, "lat": n_lat, "tc": tc, "mod_row": mod_row,
        "tm_proj": min(1024, tile_cap), "tm_merge": min(256, tile_cap), "tm_moe": min(512, tile_cap),
    }

    wts = _prep_weights(w_in, w_branch, w_out, router_w, router_b, sg_b, exp_w1, exp_b1, exp_w2, exp_b2,
                        attn_q_norm, attn_k_norm)
    rope_cos, rope_sin = _rope_tables(n_lat)

    mod_rows = -(-(1 + b) // SUBLANES) * SUBLANES
    c_all = jnp.concatenate([c_ctx[None, :], c, jnp.zeros((mod_rows - 1 - b, dm), F32)], axis=0)
    mod = _modulation(c_all, w_mod, b_mod).reshape(depth, mod_rows, N_MOD, dm)

    xs_flat = jnp.concatenate([ctx.reshape(tc, dm), x.reshape(b * n_lat, dm)], axis=0)
    n_blocks = -(-(t * TOP_K) // MOE_BM) + N_EXPERTS
    n_slots = n_blocks * MOE_BM

    for l in range(depth):
        last = l == depth - 1
        p = _in_proj(xs_flat, mod, norm1[:, None, :], wts["w_in"], l, geo)
        g_lat, g_ctx = _gdn(p, gdn_conv, gdn_a_log, gdn_dt_bias, gdn_out_norm[:, None, :], l, geo)
        a_lat = _attention(p, attn_sink, wts["q_norm"], wts["k_norm"], rope_cos, rope_sin, l, geo, False)
        if last:
            a_ctx = jnp.zeros((tc, ATT_HEADS * HEAD_DIM), BF16)
        else:
            a_ctx = _attention(p, attn_sink, wts["q_norm"], wts["k_norm"], rope_cos, rope_sin, l, geo, True)
        y0 = jnp.concatenate([g_ctx, g_lat], axis=0)
        y3 = jnp.concatenate([a_ctx, a_lat], axis=0)
        layer_w = {
            "sc_conv": sc_conv, "sg_ln_w": sg_ln_w[:, None, :], "sg_ln_b": sg_ln_b[:, None, :], "sg_w": sg_w.astype(BF16),
            "sg_b": wts["sg_b"], "w_branch": wts["w_branch"], "w_out": wts["w_out"],
            "router_w": wts["router_w"], "router_b": wts["router_b"],
        }
        x_mid, h2p, idx, gate, rank, counts = _merge(xs_flat, mod, norm2[:, None, :], y0, y3, p, layer_w, l, geo)
        dest, block_expert = _moe_plan(idx, rank, counts[0, :N_EXPERTS], n_blocks)
        xs_sorted = _dispatch(h2p, dest, n_slots, geo)
        ys_sorted = _experts(xs_sorted, block_expert, wts["w1"], wts["b1"], wts["w2"], wts["b2"], l)
        xs_flat = _combine(x_mid, mod, gate, dest, ys_sorted, l, geo)

    return xs_flat[tc:].reshape(b, n_lat, dm)
```

```python
import functools
import math

import jax
import jax.numpy as jnp
from jax import lax
from jax.experimental import pallas as pl
from jax.experimental.pallas import tpu as pltpu

F32 = jnp.float32
BF16 = jnp.bfloat16
I32 = jnp.int32
U32 = jnp.uint32
HIGHEST = lax.Precision.HIGHEST

D_MODEL = 1024
GRID_W = 64
N_MOD = 6
GDN_HEADS = 4
GDN_DK = 128
GDN_DV = 128
GDN_CONV = 5
SC_WIDTH = 512
SC_CONV = 3
SG_CHUNK = 128
SG_GROUPS = 4
SG_WIDTH = 512
ATT_HEADS = 8
ATT_KV_HEADS = 2
HEAD_DIM = 64
WINDOW = 128
ROPE_BASE = 10000.0
N_BRANCH = 4
BRANCH_W = 512
N_EXPERTS = 32
TOP_K = 4
D_EXPERT = 1024
SWIGLU_ALPHA = 1.702
SWIGLU_LIMIT = 7.0
NORM_EPS = 1e-6
LN_EPS = 1e-5
NEG_INF = -1e30

LANES = 128
SUBLANES = 8
VMEM_LIMIT_BYTES = 56 * 2**20

COL_Q = 0
COL_K = 512
COL_V = 1024
COL_Z = 1536
COL_SB = 2048
COL_SC = 2560
COL_SX = 3072
COL_GU = 3584
COL_GV = 4096
COL_AQ = 4608
COL_MG = 5120
COL_AK = 9216
COL_AV = 9344
COL_AB = 9472
PROJ_COLS = 10240
PROJ_TN = 2048
PROJ_CHUNK = 512

GDN_CHUNK = 64
GDN_PREP_UNROLL = 4
MOE_BM = 512
W1_GROUP = 256


def _cparams(sem):
    return pltpu.CompilerParams(dimension_semantics=sem, vmem_limit_bytes=VMEM_LIMIT_BYTES)


def _silu(x):
    return x * jax.nn.sigmoid(x)


def _dot(a, b, precision=None):
    return jnp.dot(a, b, preferred_element_type=F32, precision=precision)


def _dot_nt(a, b):
    return lax.dot_general(a, b, (((1,), (1,)), ((), ())), preferred_element_type=F32)


def _dot_tn(a, b):
    return lax.dot_general(a, b, (((0,), (0,)), ((), ())), preferred_element_type=F32)


def _pack_bf16_pair(lo, hi):
    lo_b = pltpu.bitcast(lo.astype(BF16).astype(F32), U32)
    hi_b = pltpu.bitcast(hi.astype(BF16).astype(F32), U32)
    return (hi_b & jnp.uint32(0xFFFF0000)) | (lo_b >> 16)


def _unpack_bf16_pair(u):
    lo = pltpu.bitcast(u << 16, F32)
    hi = pltpu.bitcast(u & jnp.uint32(0xFFFF0000), F32)
    return lo, hi


def _mod_kernel(c_ref, w_ref, b_ref, o_ref):
    s = _silu(c_ref[...])
    o_ref[...] = _dot(s, w_ref[...], precision=HIGHEST) + b_ref[...]


def _modulation(c_all, w_mod, b_mod):
    depth, dm, nm = w_mod.shape
    rows = c_all.shape[0]
    tn = 1536
    return pl.pallas_call(
        _mod_kernel,
        out_shape=jax.ShapeDtypeStruct((depth, rows, nm), F32),
        grid=(depth, nm // tn),
        in_specs=[
            pl.BlockSpec((rows, dm), lambda l, j: (0, 0)),
            pl.BlockSpec((None, dm, tn), lambda l, j: (l, 0, j)),
            pl.BlockSpec((None, 1, tn), lambda l, j: (l, 0, j)),
        ],
        out_specs=pl.BlockSpec((None, rows, tn), lambda l, j: (l, 0, j)),
        compiler_params=_cparams(("arbitrary", "arbitrary")),
        name="modulation",
    )(c_all, w_mod, b_mod.reshape(depth, 1, nm))


def _in_proj_kernel(x_ref, mod_ref, nw_ref, w_ref, o_ref):
    x = x_ref[...]
    ms = jnp.mean(x * x, axis=-1, keepdims=True)
    y = x * lax.rsqrt(ms + NORM_EPS) * nw_ref[...]
    h = (y * (1.0 + mod_ref[1:2, :]) + mod_ref[0:1, :]).astype(BF16)
    for c in range(PROJ_TN // PROJ_CHUNK):
        sl = slice(c * PROJ_CHUNK, (c + 1) * PROJ_CHUNK)
        o_ref[:, sl] = _dot(h, w_ref[:, sl]).astype(BF16)


def _in_proj(x, mod, norm_w, w_p, l, geo):
    t, dm = x.shape
    tm = geo["tm_proj"]
    mod_row = geo["mod_row"]
    return pl.pallas_call(
        _in_proj_kernel,
        out_shape=jax.ShapeDtypeStruct((t, PROJ_COLS), BF16),
        grid=(PROJ_COLS // PROJ_TN, t // tm),
        in_specs=[
            pl.BlockSpec((tm, dm), lambda j, i: (i, 0)),
            pl.BlockSpec((None, None, N_MOD, dm), lambda j, i: (l, mod_row(i * tm), 0, 0)),
            pl.BlockSpec((None, 1, dm), lambda j, i: (l, 0, 0)),
            pl.BlockSpec((None, dm, PROJ_TN), lambda j, i: (l, 0, j)),
        ],
        out_specs=pl.BlockSpec((tm, PROJ_TN), lambda j, i: (i, j)),
        compiler_params=_cparams(("arbitrary", "arbitrary")),
        name="in_proj",
    )(x, mod, norm_w, w_p)


def _softplus(x):
    return jnp.maximum(x, 0.0) + jnp.log1p(jnp.exp(-jnp.abs(x)))


def _gdn_kernel(alog_ref, dtb_ref,
                ql_ref, kl_ref, vl_ref, zl_ref, abl_ref,
                qc_ref, kc_ref, vc_ref, zc_ref, abc_ref,
                cwq_ref, cwk_ref, cwv_ref, onorm_ref,
                yl_ref, yc_ref,
                xpad, qs, ks, vs, gsc, bsc,
                u_sc, w16, qk16, qd16, kdk16, glast, oacc,
                *, n_ctx, n_lat):
    cs = GDN_CHUNK
    hd = GDN_DK
    n_all = n_ctx + n_lat
    n_chunks = n_all // cs
    n_ctx_chunks = n_ctx // cs
    h = pl.program_id(1)
    pad = SUBLANES
    lat_off = n_ctx + 3 * pad

    zeros_pad = jnp.zeros((pad, hd), F32)
    xpad[0:pad, :] = zeros_pad
    xpad[pad + n_ctx:pad + n_ctx + 2 * pad, :] = jnp.zeros((2 * pad, hd), F32)
    xpad[lat_off + n_lat:lat_off + n_lat + pad, :] = zeros_pad

    half = GDN_CONV // 2
    tile = 256

    def conv_into(src_c, src_l, cw_ref, dst, mode):
        xpad[pad:pad + n_ctx, :] = src_c[...].astype(F32)
        xpad[lat_off:lat_off + n_lat, :] = src_l[...].astype(F32)
        w = cw_ref[...]
        for seg_off, dst_off, seg_len in ((pad, 0, n_ctx), (lat_off, n_ctx, n_lat)):
            for r0 in range(0, seg_len, tile):
                acc = jnp.zeros((tile, hd), F32)
                for j in range(GDN_CONV):
                    s = seg_off + r0 + j - half
                    acc = acc + xpad[s:s + tile, :] * w[j:j + 1, :]
                y = _silu(acc)
                if mode != "v":
                    y = y * lax.rsqrt(jnp.sum(y * y, axis=-1, keepdims=True) + NORM_EPS)
                if mode == "q":
                    y = y * (hd ** -0.5)
                dst[dst_off + r0:dst_off + r0 + tile, :] = y

    conv_into(qc_ref, ql_ref, cwq_ref, qs, "q")
    conv_into(kc_ref, kl_ref, cwk_ref, ks, "k")
    conv_into(vc_ref, vl_ref, cwv_ref, vs, "v")

    sel_r = lax.broadcasted_iota(I32, (LANES, LANES), 0)
    pos = lax.broadcasted_iota(I32, (tile, LANES), 0) % cs
    for d in range(2):
        col = d * GDN_HEADS + h
        sel_a = (sel_r == col).astype(BF16)
        sel_b = (sel_r == (2 * GDN_HEADS + col)).astype(BF16)
        neg_a = -jnp.exp(jnp.full((1, LANES), alog_ref[d, h], F32))
        dtb = dtb_ref[d, h]
        for src, r_off, r_len in ((abc_ref, 0, n_ctx), (abl_ref, n_ctx, n_lat)):
            for r0 in range(0, r_len, tile):
                ab = src[r0:r0 + tile, :]
                a_b = _dot(ab, sel_a)
                b_b = _dot(ab, sel_b)
                gc = neg_a * _softplus(a_b + dtb)
                s = 1
                while s < cs:
                    if d == 0:
                        gc = gc + jnp.where(pos >= s, pltpu.roll(gc, s, 0), 0.0)
                    else:
                        gc = gc + jnp.where(pos < cs - s, pltpu.roll(gc, tile - s, 0), 0.0)
                    s *= 2
                gsc[d, r_off + r0:r_off + r0 + tile, :] = gc
                bsc[d, r_off + r0:r_off + r0 + tile, :] = jax.nn.sigmoid(b_b)

    ri = lax.broadcasted_iota(I32, (cs, cs), 0)
    ci = lax.broadcasted_iota(I32, (cs, cs), 1)
    low = (ri >= ci)
    up = (ri <= ci)
    eye = (ri == ci).astype(F32)
    offdiag = (ri != ci).astype(F32)
    n_levels = int(math.log2(cs)) - 1

    def prep(gi, carry):
        chunk_ids = [gi * GDN_PREP_UNROLL + uu for uu in range(GDN_PREP_UNROLL)]
        rows = [pl.ds(pl.multiple_of(c * cs, cs), cs) for c in chunk_ids]
        q = [qs[r, :] for r in rows]
        k = [ks[r, :] for r in rows]
        v = [vs[r, :] for r in rows]
        kb = [x.astype(BF16) for x in k]
        kkt = [_dot_nt(x, x) for x in kb]
        qkt = [_dot_nt(a.astype(BF16), b) for a, b in zip(q, kb)]
        probs = [(ui, d) for ui in range(GDN_PREP_UNROLL) for d in range(2)]
        gc_col, beta_b, dmat, t, p = {}, {}, {}, {}, {}
        for ui, d in probs:
            gc = gsc[d, rows[ui], :]
            bt = bsc[d, rows[ui], :]
            gsq = gc[:, :cs]
            dm = jnp.where(low if d == 0 else up, jnp.exp(gsq - gsq.T), 0.0)
            n = -(kkt[ui] * bt[:, :cs] * dm * offdiag)
            gc_col[ui, d], beta_b[ui, d], dmat[ui, d] = gc, bt, dm
            t[ui, d] = eye + n
            p[ui, d] = n.astype(BF16)
        p = {key: _dot(x, x).astype(BF16) for key, x in p.items()}
        for lvl in range(n_levels):
            t_next = {key: t[key] + _dot(p[key], t[key].astype(BF16)) for key in probs}
            if lvl + 1 < n_levels:
                p = {key: _dot(x, x).astype(BF16) for key, x in p.items()}
            t = t_next
        tb = {key: x.astype(BF16) for key, x in t.items()}
        eg = {key: jnp.exp(x) for key, x in gc_col.items()}
        u = {(ui, d): _dot(tb[ui, d], (v[ui] * beta_b[ui, d]).astype(BF16)) for ui, d in probs}
        w = {(ui, d): _dot(tb[ui, d], (k[ui] * beta_b[ui, d] * eg[ui, d]).astype(BF16)) for ui, d in probs}
        for ui, d in probs:
            r = rows[ui]
            last = cs - 1 if d == 0 else 0
            gl = gc_col[ui, d][last:last + 1, :]
            u_sc[d, r, :] = u[ui, d]
            w16[d, r, :] = w[ui, d].astype(BF16)
            qk16[d, r, :] = (qkt[ui] * dmat[ui, d]).astype(BF16)
            qd16[d, r, :] = (q[ui] * eg[ui, d]).astype(BF16)
            kdk16[d, r, :] = (k[ui] * jnp.exp(gl - gc_col[ui, d])).astype(BF16)
            g0 = pl.multiple_of(chunk_ids[ui] * SUBLANES, SUBLANES)
            glast[d, pl.ds(g0, SUBLANES), :] = jnp.broadcast_to(jnp.exp(gl), (SUBLANES, LANES))
        return carry

    lax.fori_loop(0, n_chunks // GDN_PREP_UNROLL, prep, 0)

    oacc[...] = jnp.zeros_like(oacc)

    def step(i, carry):
        c_b = jnp.where(i < n_ctx_chunks, n_ctx_chunks - 1 - i, n_chunks + n_ctx_chunks - 1 - i)
        cid = (i, c_b)
        rows = [pl.ds(pl.multiple_of(c * cs, cs), cs) for c in cid]
        sb = [s.astype(BF16) for s in carry]
        ws = [_dot(w16[d, rows[d], :], sb[d]) for d in range(2)]
        qs_s = [_dot(qd16[d, rows[d], :], sb[d]) for d in range(2)]
        vnb = [(u_sc[d, rows[d], :] - ws[d]).astype(BF16) for d in range(2)]
        o = [qs_s[d] + _dot(qk16[d, rows[d], :], vnb[d]) for d in range(2)]
        kv = [_dot_tn(kdk16[d, rows[d], :], vnb[d]) for d in range(2)]
        s_new = []
        for d in range(2):
            g0 = pl.multiple_of(cid[d] * SUBLANES, SUBLANES)
            s_new.append(carry[d] * glast[d, pl.ds(g0, 1), :] + kv[d])
            oacc[rows[d], :] = oacc[rows[d], :] + o[d]
        return tuple(s_new)

    s0 = jnp.zeros((hd, GDN_DV), F32)
    lax.fori_loop(0, n_chunks, step, (s0, s0))

    onw = onorm_ref[...]

    def out_gate(o, z):
        y = o * lax.rsqrt(jnp.mean(o * o, axis=-1, keepdims=True) + NORM_EPS) * onw
        return (y * _silu(z.astype(F32))).astype(BF16)

    yc_ref[...] = out_gate(oacc[0:n_ctx, :], zc_ref[...])
    for r0 in range(0, n_lat, tile):
        yl_ref[r0:r0 + tile, :] = out_gate(oacc[n_ctx + r0:n_ctx + r0 + tile, :], zl_ref[r0:r0 + tile, :])


def _gdn(p, conv_w, a_log, dt_bias, out_norm, l, geo):
    b, n_ctx, n_lat = geo["b"], geo["ctx"], geo["lat"]
    n_all = n_ctx + n_lat
    lat0 = geo["tc"] // n_lat
    hd = GDN_DK

    def lat_spec(col):
        return pl.BlockSpec((n_lat, hd), lambda bi, h: (lat0 + bi, col // hd + h))

    def ctx_spec(col):
        return pl.BlockSpec((n_ctx, hd), lambda bi, h: (bi, col // hd + h))

    def fixed_lat(col):
        return pl.BlockSpec((n_lat, hd), lambda bi, h: (lat0 + bi, col // hd))

    def fixed_ctx(col):
        return pl.BlockSpec((n_ctx, hd), lambda bi, h: (bi, col // hd))

    def cw_spec(col):
        return pl.BlockSpec((None, GDN_CONV, hd), lambda bi, h: (l, 0, col // hd + h))

    smem = pl.BlockSpec(memory_space=pltpu.SMEM)
    kern = functools.partial(_gdn_kernel, n_ctx=n_ctx, n_lat=n_lat)
    n_chunks = n_all // GDN_CHUNK
    assert n_chunks % GDN_PREP_UNROLL == 0
    scratch = [
        pltpu.VMEM((n_all + 4 * SUBLANES, hd), F32),
        pltpu.VMEM((n_all, hd), F32),
        pltpu.VMEM((n_all, hd), F32),
        pltpu.VMEM((n_all, hd), F32),
        pltpu.VMEM((2, n_all, LANES), F32),
        pltpu.VMEM((2, n_all, LANES), F32),
        pltpu.VMEM((2, n_all, GDN_DV), F32),
        pltpu.VMEM((2, n_all, hd), BF16),
        pltpu.VMEM((2, n_all, GDN_CHUNK), BF16),
        pltpu.VMEM((2, n_all, hd), BF16),
        pltpu.VMEM((2, n_all, hd), BF16),
        pltpu.VMEM((2, n_chunks * SUBLANES, LANES), F32),
        pltpu.VMEM((n_all, GDN_DV), F32),
    ]
    y_lat, y_ctx = pl.pallas_call(
        kern,
        out_shape=(jax.ShapeDtypeStruct((b * n_lat, GDN_HEADS * GDN_DV), BF16),
                   jax.ShapeDtypeStruct((b * n_ctx, GDN_HEADS * GDN_DV), BF16)),
        grid=(b, GDN_HEADS),
        in_specs=[smem, smem,
                  lat_spec(COL_Q), lat_spec(COL_K), lat_spec(COL_V), lat_spec(COL_Z), fixed_lat(COL_AB),
                  ctx_spec(COL_Q), ctx_spec(COL_K), ctx_spec(COL_V), ctx_spec(COL_Z), fixed_ctx(COL_AB),
                  cw_spec(0), cw_spec(GDN_HEADS * hd), cw_spec(2 * GDN_HEADS * hd),
                  pl.BlockSpec((None, 1, GDN_DV), lambda bi, h: (l, 0, 0))],
        out_specs=(pl.BlockSpec((n_lat, GDN_DV), lambda bi, h: (bi, h)),
                   pl.BlockSpec((n_ctx, GDN_DV), lambda bi, h: (bi, h))),
        scratch_shapes=scratch,
        compiler_params=_cparams(("arbitrary", "arbitrary")),
        name="gdn",
    )(a_log[l], dt_bias[l], p, p, p, p, p, p, p, p, p, p, conv_w, conv_w, conv_w, out_norm)
    return y_lat, y_ctx


def _head_rms(x, bd, w):
    ss = _dot(x * x, bd, precision=HIGHEST) * (1.0 / HEAD_DIM)
    return x * lax.rsqrt(ss + NORM_EPS) * w


def _rope(x, cos, sin_signed, lane):
    nf = HEAD_DIM // 4
    partner = jnp.where((lane % (2 * nf)) < nf, pltpu.roll(x, LANES - nf, 1), pltpu.roll(x, nf, 1))
    return x * cos + partner * sin_signed


def _attn_kernel(sink_ref, q_ref, kq_ref, vq_ref, kc_ref, vc_ref,
                 cosq_ref, sinq_ref, cosk_ref, sink_k_ref, qw_ref, kw_ref,
                 o_ref, kp, vp, kcp, vcp, *, n_keys, n_ctx, local):
    n = pl.program_id(1)
    w = WINDOW
    lane = lax.broadcasted_iota(I32, (1, LANES), 1)
    lo_mask = (lane < HEAD_DIM).astype(F32)
    hi_mask = 1.0 - lo_mask
    bd_r = lax.broadcasted_iota(I32, (LANES, LANES), 0) // HEAD_DIM
    bd_c = lax.broadcasted_iota(I32, (LANES, LANES), 1) // HEAD_DIM
    bd = (bd_r == bd_c).astype(F32)
    kw = kw_ref[...]

    def store_variants(dst, row0, x):
        xr = pltpu.roll(x, HEAD_DIM, 1)
        rows = x.shape[0]
        dst[0, row0:row0 + rows, :] = (x * lo_mask).astype(BF16)
        dst[1, row0:row0 + rows, :] = (xr * hi_mask).astype(BF16)
        dst[2, row0:row0 + rows, :] = (xr * lo_mask).astype(BF16)
        dst[3, row0:row0 + rows, :] = (x * hi_mask).astype(BF16)

    @pl.when(n == 0)
    def _prepare_keys():
        tile = 256
        for r0 in range(0, n_ctx, tile):
            rr = min(tile, n_ctx - r0)
            kc = _head_rms(kc_ref[r0:r0 + rr, :].astype(F32), bd, kw)
            store_variants(kcp, r0, kc)
            store_variants(vcp, r0, vc_ref[r0:r0 + rr, :].astype(F32))
        if local:
            zero = jnp.zeros((w, LANES), BF16)
            for t in range(4):
                kp[t, 0:w, :] = zero
                kp[t, w + n_keys:2 * w + n_keys, :] = zero
                vp[t, 0:w, :] = zero
                vp[t, w + n_keys:2 * w + n_keys, :] = zero
            for r0 in range(0, n_keys, tile):
                k = _head_rms(kq_ref[r0:r0 + tile, :].astype(F32), bd, kw)
                k = _rope(k, cosk_ref[r0:r0 + tile, :], sink_k_ref[r0:r0 + tile, :], lane)
                store_variants(kp, w + r0, k)
                store_variants(vp, w + r0, vq_ref[r0:r0 + tile, :].astype(F32))

    qw = qw_ref[...]
    q2 = []
    for g in range(ATT_HEADS // 2):
        qg = _head_rms(q_ref[:, g * LANES:(g + 1) * LANES].astype(F32), bd, qw)
        if local:
            qg = _rope(qg, cosq_ref[...], sinq_ref[...], lane)
        q2.append((qg * (HEAD_DIM ** -0.5)).astype(BF16))

    rows2 = 2 * w
    row_i = lax.broadcasted_iota(I32, (rows2, 1), 0)
    first = row_i < w
    if local:
        n_blk = n_keys // w
        qq = lax.broadcasted_iota(I32, (rows2, 3 * w), 0) % w
        kcol = lax.broadcasted_iota(I32, (rows2, 3 * w), 1)
        kk = kcol % w
        blk = kcol // w
        valid = ((blk == 1)
                 | ((blk == 0) & (kk >= qq) & (n >= 1))
                 | ((blk == 2) & (kk <= qq) & (n < n_blk - 1)))
        k0 = pl.multiple_of(n * w, w)

    for j in range(ATT_KV_HEADS):
        lhs = jnp.concatenate([q2[2 * j], q2[2 * j + 1]], axis=0)
        pair_out = [None, None]
        for t in range(2):
            var = 2 * j + t
            sink = jnp.where(first, sink_ref[4 * j + t], sink_ref[4 * j + 2 + t])
            s_ctx = _dot_nt(lhs, kcp[var])
            m = jnp.maximum(jnp.max(s_ctx, axis=-1, keepdims=True), sink)
            if local:
                s_loc = _dot_nt(lhs, kp[var, pl.ds(k0, 3 * w), :])
                s_loc = jnp.where(valid, s_loc, NEG_INF)
                m = jnp.maximum(m, jnp.max(s_loc, axis=-1, keepdims=True))
            e_ctx = jnp.exp(s_ctx - m)
            den = jnp.sum(e_ctx, axis=-1, keepdims=True) + jnp.exp(sink - m)
            pv = _dot(e_ctx.astype(BF16), vcp[var])
            if local:
                e_loc = jnp.exp(s_loc - m)
                den = den + jnp.sum(e_loc, axis=-1, keepdims=True)
                pv = pv + _dot(e_loc.astype(BF16), vp[var, pl.ds(k0, 3 * w), :])
            o = pv / den
            for half_i in range(2):
                part = o[half_i * w:(half_i + 1) * w, :]
                pair_out[half_i] = part if pair_out[half_i] is None else pair_out[half_i] + part
        for half_i in range(2):
            c0 = (2 * j + half_i) * LANES
            o_ref[:, c0:c0 + LANES] = pair_out[half_i].astype(BF16)


def _attention(p, sink, q_norm_w, k_norm_w, rope_cos, rope_sin, l, geo, ctx_queries):
    b, n_ctx, n_lat = geo["b"], geo["ctx"], geo["lat"]
    w = WINDOW
    lat0 = geo["tc"] // n_lat
    local = not ctx_queries
    n_q = n_ctx if ctx_queries else n_lat
    nqb = n_q // w
    q_row0 = 0 if ctx_queries else geo["tc"] // w
    kern = functools.partial(_attn_kernel, n_keys=n_lat, n_ctx=n_ctx, local=local)
    in_specs = [
        pl.BlockSpec(memory_space=pltpu.SMEM),
        pl.BlockSpec((w, ATT_HEADS * HEAD_DIM), lambda bi, n: (q_row0 + bi * nqb + n, COL_AQ // 512)),
        pl.BlockSpec((n_lat, LANES), lambda bi, n: (lat0 + bi, COL_AK // LANES)),
        pl.BlockSpec((n_lat, LANES), lambda bi, n: (lat0 + bi, COL_AV // LANES)),
        pl.BlockSpec((n_ctx, LANES), lambda bi, n: (bi, COL_AK // LANES)),
        pl.BlockSpec((n_ctx, LANES), lambda bi, n: (bi, COL_AV // LANES)),
        pl.BlockSpec((w, LANES), lambda bi, n: (n if local else 0, 0)),
        pl.BlockSpec((w, LANES), lambda bi, n: (n if local else 0, 0)),
        pl.BlockSpec((n_lat, LANES), lambda bi, n: (0, 0)),
        pl.BlockSpec((n_lat, LANES), lambda bi, n: (0, 0)),
        pl.BlockSpec((None, 1, LANES), lambda bi, n: (l, 0, 0)),
        pl.BlockSpec((None, 1, LANES), lambda bi, n: (l, 0, 0)),
    ]
    scratch = [
        pltpu.VMEM((4, n_lat + 2 * w, LANES), BF16),
        pltpu.VMEM((4, n_lat + 2 * w, LANES), BF16),
        pltpu.VMEM((4, n_ctx, LANES), BF16),
        pltpu.VMEM((4, n_ctx, LANES), BF16),
    ]
    return pl.pallas_call(
        kern,
        out_shape=jax.ShapeDtypeStruct((b * n_q, ATT_HEADS * HEAD_DIM), BF16),
        grid=(b, nqb),
        in_specs=in_specs,
        out_specs=pl.BlockSpec((w, ATT_HEADS * HEAD_DIM), lambda bi, n: (bi * nqb + n, 0)),
        scratch_shapes=scratch,
        compiler_params=_cparams(("arbitrary", "arbitrary")),
        name="attn_ctx" if ctx_queries else "attn_lat",
    )(sink[l], p, p, p, p, p, rope_cos, rope_sin, rope_cos, rope_sin, q_norm_w, k_norm_w)


def _merge_kernel(x_ref, mod_ref, n2_ref, y0_ref, y3_ref,
                  sb_ref, sc_ref, sx_ref, scp_ref, sxp_ref, scn_ref, sxn_ref,
                  gu_ref, gv_ref, mg0_ref, mg1_ref, mg2_ref, mg3_ref,
                  scw_ref, lnw_ref, lnb_ref, sgw_ref, sgb_ref, wb_ref, wo_ref, rw_ref, rb_ref,
                  xo_ref, h2_ref, idx_ref, gate_ref, rank_ref, cnt_ref,
                  carry, *, tm, tc, n_ctx, n_lat):
    i = pl.program_id(0)
    r0 = i * tm

    row = lax.broadcasted_iota(I32, (tm, 1), 0)
    g_row = r0 + row
    in_ctx = g_row < tc
    seg_pos = jnp.where(in_ctx, g_row % n_ctx, (g_row - tc) % n_lat)
    seg_len = jnp.where(in_ctx, n_ctx, n_lat)
    cx = sc_ref[...].astype(F32) * sx_ref[...].astype(F32)
    cx_prev_halo = scp_ref[SUBLANES - 1:SUBLANES, :].astype(F32) * sxp_ref[SUBLANES - 1:SUBLANES, :].astype(F32)
    cx_next_halo = scn_ref[0:1, :].astype(F32) * sxn_ref[0:1, :].astype(F32)
    prev = jnp.where(row == 0, cx_prev_halo, pltpu.roll(cx, 1, 0))
    prev = jnp.where(seg_pos == 0, 0.0, prev)
    nxt = jnp.where(row == tm - 1, cx_next_halo, pltpu.roll(cx, tm - 1, 0))
    nxt = jnp.where(seg_pos == seg_len - 1, 0.0, nxt)
    scw = scw_ref[...]
    y1 = sb_ref[...].astype(F32) * (prev * scw[0:1, :] + cx * scw[1:2, :] + nxt * scw[2:3, :])

    inv_sqrt2 = 1.0 / math.sqrt(2.0)

    def gelu(t):
        return 0.5 * t * (1.0 + lax.erf(t * inv_sqrt2))

    u = gelu(gu_ref[...].astype(F32))
    v = gelu(gv_ref[...].astype(F32))
    mu = jnp.mean(v, axis=-1, keepdims=True)
    vc = v - mu
    v = vc * lax.rsqrt(jnp.mean(vc * vc, axis=-1, keepdims=True) + LN_EPS) * lnw_ref[...] + lnb_ref[...]
    vb = v.astype(BF16)
    gw = SG_WIDTH // SG_GROUPS
    chunks = []
    for c in range(tm // SG_CHUNK):
        groups = []
        for g in range(SG_GROUPS):
            groups.append(_dot(sgw_ref[g], vb[c * SG_CHUNK:(c + 1) * SG_CHUNK, g * gw:(g + 1) * gw]))
        chunks.append(jnp.concatenate(groups, axis=1) + sgb_ref[...])
    y2 = u * jnp.concatenate(chunks, axis=0)

    ys = (y0_ref[...], y1.astype(BF16), y2.astype(BF16), y3_ref[...])
    gates = (mg0_ref, mg1_ref, mg2_ref, mg3_ref)
    m = None
    for br in range(N_BRANCH):
        term = jax.nn.sigmoid(gates[br][...].astype(F32)) * _dot(ys[br], wb_ref[br])
        m = term if m is None else m + term
    y = _dot(m.astype(BF16), wo_ref[...])
    x_new = x_ref[...] + mod_ref[2:3, :] * y
    xo_ref[...] = x_new

    ms = jnp.mean(x_new * x_new, axis=-1, keepdims=True)
    h2 = x_new * lax.rsqrt(ms + NORM_EPS) * n2_ref[...] * (1.0 + mod_ref[4:5, :]) + mod_ref[3:4, :]
    half = D_MODEL // 2
    h2_ref[...] = _pack_bf16_pair(h2[:, :half], h2[:, half:])

    h2_hi = h2.astype(BF16)
    h2_lo = (h2 - h2_hi.astype(F32)).astype(BF16)
    prod = _dot(jnp.concatenate([h2_hi, h2_lo], axis=1), rw_ref[...])
    lane = lax.broadcasted_iota(I32, (tm, LANES), 1)
    logits = prod + pltpu.roll(prod, LANES - N_EXPERTS, 1)
    logits = jnp.where(lane < N_EXPERTS, logits, NEG_INF) + rb_ref[...]
    lane_f = lane.astype(F32)
    work = logits
    topv = jnp.full((tm, LANES), NEG_INF, F32)
    topi = jnp.zeros((tm, LANES), I32)
    onehot = jnp.zeros((tm, LANES), F32)
    firsts = []
    for k in range(TOP_K):
        mx = jnp.max(work, axis=-1, keepdims=True)
        first_f = jnp.min(jnp.where(work == mx, lane_f, float(LANES)), axis=-1, keepdims=True)
        first = first_f.astype(I32)
        hit = lane == first
        topv = jnp.where(lane == k, mx, topv)
        topi = jnp.where(lane == k, first, topi)
        onehot = jnp.where(hit, 1.0, onehot)
        work = jnp.where(hit, -jnp.inf, work)
        firsts.append(first)
    e = jnp.where(lane < TOP_K, jnp.exp(topv - jnp.max(topv, axis=-1, keepdims=True)), 0.0)
    gate = e / jnp.sum(e, axis=-1, keepdims=True)

    @pl.when(i == 0)
    def _init():
        carry[...] = jnp.zeros_like(carry)

    tri = (lax.broadcasted_iota(I32, (tm, tm), 0) > lax.broadcasted_iota(I32, (tm, tm), 1)).astype(BF16)
    before = carry[0:1, :] + _dot(tri, onehot.astype(BF16))
    rank = jnp.zeros((tm, LANES), F32)
    for k in range(TOP_K):
        r_k = jnp.sum(jnp.where(lane == firsts[k], before, 0.0), axis=-1, keepdims=True)
        rank = jnp.where(lane == k, r_k, rank)
    new_carry = carry[0:1, :] + jnp.sum(onehot, axis=0, keepdims=True)
    carry[...] = jnp.broadcast_to(new_carry, carry.shape)
    cnt_ref[...] = jnp.broadcast_to(new_carry, cnt_ref.shape).astype(I32)
    idx_ref[...] = topi[:, :SUBLANES]
    gate_ref[...] = gate[:, :SUBLANES]
    rank_ref[...] = rank[:, :SUBLANES].astype(I32)


def _merge(x, mod, norm2, y0, y3, p, w, l, geo):
    t, dm = x.shape
    tm = geo["tm_merge"]
    mod_row = geo["mod_row"]
    kern = functools.partial(_merge_kernel, tm=tm, tc=geo["tc"], n_ctx=geo["ctx"], n_lat=geo["lat"])
    halo = tm // SUBLANES
    n_halo = t // SUBLANES

    def col(c, width=512):
        return pl.BlockSpec((tm, width), lambda i: (i, c // width))

    def prev_halo(c):
        return pl.BlockSpec((SUBLANES, 512), lambda i: (jnp.maximum(i * halo - 1, 0), c // 512))

    def next_halo(c):
        return pl.BlockSpec((SUBLANES, 512), lambda i: (jnp.minimum((i + 1) * halo, n_halo - 1), c // 512))

    def layer(shape):
        nd = len(shape)
        return pl.BlockSpec((None,) + shape, lambda i: (l,) + (0,) * nd)

    in_specs = [
        pl.BlockSpec((tm, dm), lambda i: (i, 0)),
        pl.BlockSpec((None, None, N_MOD, dm), lambda i: (l, mod_row(i * tm), 0, 0)),
        layer((1, dm)),
        pl.BlockSpec((tm, 512), lambda i: (i, 0)),
        pl.BlockSpec((tm, 512), lambda i: (i, 0)),
        col(COL_SB), col(COL_SC), col(COL_SX),
        prev_halo(COL_SC), prev_halo(COL_SX), next_halo(COL_SC), next_halo(COL_SX),
        col(COL_GU), col(COL_GV),
        col(COL_MG, 1024), col(COL_MG + 1024, 1024), col(COL_MG + 2048, 1024), col(COL_MG + 3072, 1024),
        layer((SC_CONV, SC_WIDTH)), layer((1, SG_WIDTH)), layer((1, SG_WIDTH)),
        layer((SG_GROUPS, SG_CHUNK, SG_CHUNK)), layer((SG_CHUNK, SG_WIDTH)),
        layer((N_BRANCH, BRANCH_W, dm)), layer((dm, dm)), layer((2 * dm, LANES)), layer((1, LANES)),
    ]
    out_shape = (
        jax.ShapeDtypeStruct((t, dm), F32),
        jax.ShapeDtypeStruct((t, dm // 2), U32),
        jax.ShapeDtypeStruct((t, SUBLANES), I32),
        jax.ShapeDtypeStruct((t, SUBLANES), F32),
        jax.ShapeDtypeStruct((t, SUBLANES), I32),
        jax.ShapeDtypeStruct((SUBLANES, LANES), I32),
    )
    out_specs = (
        pl.BlockSpec((tm, dm), lambda i: (i, 0)),
        pl.BlockSpec((tm, dm // 2), lambda i: (i, 0)),
        pl.BlockSpec((tm, SUBLANES), lambda i: (i, 0)),
        pl.BlockSpec((tm, SUBLANES), lambda i: (i, 0)),
        pl.BlockSpec((tm, SUBLANES), lambda i: (i, 0)),
        pl.BlockSpec((SUBLANES, LANES), lambda i: (0, 0)),
    )
    return pl.pallas_call(
        kern,
        out_shape=out_shape,
        grid=(t // tm,),
        in_specs=in_specs,
        out_specs=out_specs,
        scratch_shapes=[pltpu.VMEM((SUBLANES, LANES), F32)],
        compiler_params=_cparams(("arbitrary",)),
        name="merge",
    )(x, mod, norm2, y0, y3, p, p, p, p, p, p, p, p, p, p, p, p, p,
      w["sc_conv"], w["sg_ln_w"], w["sg_ln_b"], w["sg_w"], w["sg_b"], w["w_branch"], w["w_out"],
      w["router_w"], w["router_b"])


def _dispatch_kernel(dest_ref, h_ref, xs_in_ref, xs_ref, sem, *, tm):
    del xs_in_ref

    def row_copy(t, d):
        return pltpu.make_async_copy(h_ref.at[pl.ds(t, 1), :], xs_ref.at[pl.ds(d, 1), :], sem)

    def issue(t, carry):
        for k in range(TOP_K):
            row_copy(t, dest_ref[t * TOP_K + k]).start(priority=k % 2)
        return carry

    lax.fori_loop(0, tm, issue, 0)
    for _ in range(TOP_K):
        pltpu.make_async_copy(h_ref, xs_ref.at[pl.ds(0, tm), :], sem).wait()


def _dispatch(h2p, dest_flat, n_slots, geo):
    t, half = h2p.shape
    tm = geo["tm_moe"]
    xs0 = jnp.zeros((n_slots, half), U32)
    return pl.pallas_call(
        functools.partial(_dispatch_kernel, tm=tm),
        out_shape=jax.ShapeDtypeStruct((n_slots, half), U32),
        grid=(t // tm,),
        in_specs=[
            pl.BlockSpec((tm * TOP_K,), lambda i: (i,), memory_space=pltpu.SMEM),
            pl.BlockSpec((tm, half), lambda i: (i, 0)),
            pl.BlockSpec(memory_space=pl.ANY),
        ],
        out_specs=pl.BlockSpec(memory_space=pl.ANY),
        scratch_shapes=[pltpu.SemaphoreType.DMA],
        input_output_aliases={2: 0},
        compiler_params=_cparams(("arbitrary",)),
        name="moe_dispatch",
    )(dest_flat, h2p, xs0)


def _expert_kernel(be_ref, xs_ref, w1_ref, b1_ref, w2_ref, b2_ref, ys_ref):
    del be_ref
    lo, hi = _unpack_bf16_pair(xs_ref[...])
    half = D_MODEL // 2
    h = (_dot(lo.astype(BF16), w1_ref[0:half, :]) + _dot(hi.astype(BF16), w1_ref[half:, :]) + b1_ref[...])
    hg = W1_GROUP // 2
    acts = []
    for g in range(2 * D_EXPERT // W1_GROUP):
        glu = jnp.minimum(h[:, g * W1_GROUP:g * W1_GROUP + hg], SWIGLU_LIMIT)
        lin = jnp.clip(h[:, g * W1_GROUP + hg:(g + 1) * W1_GROUP], -SWIGLU_LIMIT, SWIGLU_LIMIT)
        acts.append((glu * jax.nn.sigmoid(SWIGLU_ALPHA * glu) * (lin + 1.0)).astype(BF16))
    y = _dot(jnp.concatenate(acts, axis=1), w2_ref[...]) + b2_ref[...]
    ys_ref[...] = _pack_bf16_pair(y[:, :half], y[:, half:])


def _w1_regroup_kernel(w_ref, o_ref):
    hg = W1_GROUP // 2
    r = lax.broadcasted_iota(I32, (W1_GROUP, W1_GROUP), 0)
    c = lax.broadcasted_iota(I32, (W1_GROUP, W1_GROUP), 1)
    perm = (r == jnp.where(c < hg, 2 * c, 2 * (c - hg) + 1)).astype(BF16)
    for g in range(2 * D_EXPERT // W1_GROUP):
        sl = slice(g * W1_GROUP, (g + 1) * W1_GROUP)
        o_ref[:, sl] = _dot(w_ref[:, sl].astype(BF16), perm).astype(BF16)


def _w1_regroup(exp_w1):
    depth, ne, dm, dh = exp_w1.shape
    tr = 512
    out = pl.pallas_call(
        _w1_regroup_kernel,
        out_shape=jax.ShapeDtypeStruct((depth * ne, dm, dh), BF16),
        grid=(depth * ne, dm // tr),
        in_specs=[pl.BlockSpec((None, tr, dh), lambda e, i: (e, i, 0))],
        out_specs=pl.BlockSpec((None, tr, dh), lambda e, i: (e, i, 0)),
        compiler_params=_cparams(("arbitrary", "arbitrary")),
        name="w1_regroup",
    )(exp_w1.reshape(depth * ne, dm, dh))
    return out.reshape(depth, ne, dm, dh)


def _experts(xs, block_expert, w1p, b1p, w2, b2, l):
    n_slots, half = xs.shape
    bm = MOE_BM
    grid_spec = pltpu.PrefetchScalarGridSpec(
        num_scalar_prefetch=1,
        grid=(n_slots // bm,),
        in_specs=[
            pl.BlockSpec((bm, half), lambda i, be: (i, 0)),
            pl.BlockSpec((None, None, D_MODEL, 2 * D_EXPERT), lambda i, be: (l, be[i], 0, 0)),
            pl.BlockSpec((None, None, 1, 2 * D_EXPERT), lambda i, be: (l, be[i], 0, 0)),
            pl.BlockSpec((None, None, D_EXPERT, D_MODEL), lambda i, be: (l, be[i], 0, 0)),
            pl.BlockSpec((None, None, 1, D_MODEL), lambda i, be: (l, be[i], 0, 0)),
        ],
        out_specs=pl.BlockSpec((bm, half), lambda i, be: (i, 0)),
    )
    return pl.pallas_call(
        _expert_kernel,
        out_shape=jax.ShapeDtypeStruct((n_slots, half), U32),
        grid_spec=grid_spec,
        compiler_params=_cparams(("arbitrary",)),
        name="moe_experts",
    )(block_expert, xs, w1p, b1p, w2, b2)


def _combine_kernel(dest_ref, gate_ref, x_ref, mod_ref, ys_ref, xo_ref, buf, sem, *, tm):
    def row_copy(t, k, d):
        return pltpu.make_async_copy(ys_ref.at[pl.ds(d, 1), :], buf.at[k, pl.ds(t, 1), :], sem)

    def issue(t, carry):
        for k in range(TOP_K):
            row_copy(t, k, dest_ref[t * TOP_K + k]).start(priority=k % 2)
        return carry

    lax.fori_loop(0, tm, issue, 0)
    for k in range(TOP_K):
        pltpu.make_async_copy(ys_ref.at[pl.ds(0, tm), :], buf.at[k], sem).wait()

    gate = gate_ref[...]
    half = D_MODEL // 2
    f_lo = jnp.zeros((tm, half), F32)
    f_hi = jnp.zeros((tm, half), F32)
    for k in range(TOP_K):
        lo, hi = _unpack_bf16_pair(buf[k])
        g = gate[:, k:k + 1]
        f_lo = f_lo + g * lo
        f_hi = f_hi + g * hi
    g2 = mod_ref[5:6, :]
    xo_ref[:, :half] = x_ref[:, :half] + g2[:, :half] * f_lo
    xo_ref[:, half:] = x_ref[:, half:] + g2[:, half:] * f_hi


def _combine(x, mod, gate, dest_flat, ys, l, geo):
    t, dm = x.shape
    tm = geo["tm_moe"]
    mod_row = geo["mod_row"]
    return pl.pallas_call(
        functools.partial(_combine_kernel, tm=tm),
        out_shape=jax.ShapeDtypeStruct((t, dm), F32),
        grid=(t // tm,),
        in_specs=[
            pl.BlockSpec((tm * TOP_K,), lambda i: (i,), memory_space=pltpu.SMEM),
            pl.BlockSpec((tm, SUBLANES), lambda i: (i, 0)),
            pl.BlockSpec((tm, dm), lambda i: (i, 0)),
            pl.BlockSpec((None, None, N_MOD, dm), lambda i: (l, mod_row(i * tm), 0, 0)),
            pl.BlockSpec(memory_space=pl.ANY),
        ],
        out_specs=pl.BlockSpec((tm, dm), lambda i: (i, 0)),
        scratch_shapes=[pltpu.VMEM((TOP_K, tm, dm // 2), U32), pltpu.SemaphoreType.DMA],
        compiler_params=_cparams(("arbitrary",)),
        name="moe_combine",
    )(dest_flat, gate, x, mod, ys)


def _moe_plan(idx, rank, counts, n_blocks):
    bm = MOE_BM
    padded = (counts + bm - 1) // bm * bm
    pad_end = jnp.cumsum(padded)
    pad_start = pad_end - padded
    expert = idx[:, :TOP_K]
    onehot = expert[:, :, None] == jnp.arange(N_EXPERTS, dtype=I32)[None, None, :]
    dest = rank[:, :TOP_K] + jnp.sum(jnp.where(onehot, pad_start[None, None, :], 0), axis=-1)
    block_start = jnp.arange(n_blocks, dtype=I32) * bm
    block_expert = jnp.minimum(
        jnp.sum((block_start[:, None] >= pad_end[None, :]).astype(I32), axis=-1), N_EXPERTS - 1)
    return dest.reshape(-1).astype(I32), block_expert.astype(I32)


def _rope_tables(n_lat):
    rows = n_lat // GRID_W
    row = jnp.repeat(jnp.arange(rows, dtype=F32), GRID_W)
    colp = jnp.tile(jnp.arange(GRID_W, dtype=F32), rows)
    n_freq = HEAD_DIM // 4
    inv = ROPE_BASE ** (-jnp.arange(n_freq, dtype=F32) / n_freq)
    ang = jnp.concatenate([row[:, None] * inv, colp[:, None] * inv], axis=-1)
    cos, sin = jnp.cos(ang), jnp.sin(ang)
    cr, cc = cos[:, :n_freq], cos[:, n_freq:]
    sr, sc = sin[:, :n_freq], sin[:, n_freq:]
    cos64 = jnp.concatenate([cr, cr, cc, cc], axis=-1)
    sin64 = jnp.concatenate([-sr, sr, -sc, sc], axis=-1)
    return jnp.tile(cos64, (1, 2)), jnp.tile(sin64, (1, 2))


def _prep_weights(w_in, w_branch, w_out, router_w, router_b, sg_b, exp_w1, exp_b1, exp_w2, exp_b2,
                  attn_q_norm, attn_k_norm):
    depth, dm, _ = w_in.shape
    qkvz = 2048
    ab0, sc0, sg0, at0, mg0 = 2048, 2064, 3600, 4624, 5392
    pad = PROJ_COLS - (COL_AB + 16)
    w_p = jnp.concatenate([
        w_in[:, :, 0:qkvz],
        w_in[:, :, sc0:sg0],
        w_in[:, :, sg0:at0],
        w_in[:, :, at0:at0 + 512],
        w_in[:, :, mg0:mg0 + 4096],
        w_in[:, :, at0 + 512:at0 + 768],
        w_in[:, :, ab0:ab0 + 16],
        jnp.zeros((depth, dm, pad), w_in.dtype),
    ], axis=-1).astype(BF16)
    rw_hi = router_w.astype(BF16)
    rw_lo = (router_w - rw_hi.astype(F32)).astype(BF16)
    rw = jnp.concatenate([
        jnp.concatenate([rw_hi, rw_lo, jnp.zeros((depth, dm, LANES - 2 * N_EXPERTS), BF16)], axis=-1),
        jnp.concatenate([rw_hi, jnp.zeros((depth, dm, LANES - N_EXPERTS), BF16)], axis=-1),
    ], axis=1)
    rb = jnp.concatenate([router_b, jnp.full((depth, LANES - N_EXPERTS), NEG_INF, F32)], axis=-1)
    w1p = _w1_regroup(exp_w1)
    ne = exp_b1.shape[1]
    b1p = jnp.swapaxes(exp_b1.reshape(depth, ne, -1, W1_GROUP // 2, 2), -1, -2).reshape(depth, ne, -1)
    return {
        "w_in": w_p,
        "w_branch": w_branch.astype(BF16),
        "w_out": w_out.astype(BF16),
        "router_w": rw,
        "router_b": rb[:, None, :],
        "sg_b": jnp.repeat(jnp.swapaxes(sg_b, 1, 2), SG_WIDTH // SG_GROUPS, axis=2),
        "w1": w1p,
        "b1": b1p[:, :, None, :],
        "w2": exp_w2.astype(BF16),
        "b2": exp_b2[:, :, None, :],
        "q_norm": jnp.tile(attn_q_norm, (1, 2))[:, None, :],
        "k_norm": jnp.tile(attn_k_norm, (1, 2))[:, None, :],
    }


def kernel(x, c, ctx, c_ctx, norm1, norm2, w_mod, b_mod, w_in, gdn_conv, gdn_a_log, gdn_dt_bias, gdn_out_norm,
           sc_conv, sg_ln_w, sg_ln_b, sg_w, sg_b, attn_q_norm, attn_k_norm, attn_sink, w_branch, w_out,
           router_w, router_b, exp_w1, exp_b1, exp_w2, exp_b2):
    b, n_lat, dm = x.shape
    n_ctx = ctx.shape[1]
    depth = w_in.shape[0]
    tc = b * n_ctx
    t = tc + b * n_lat
    assert dm == D_MODEL and tc % n_lat == 0 and n_lat % 256 == 0 and n_ctx % 256 == 0
    tile_cap = math.gcd(tc, n_lat)

    def mod_row(r0):
        return jnp.where(r0 < tc, 0, 1 + (r0 - tc) // n_lat)

    geo = {
        "b": b, "ctx": n_ctx, "lat": n_lat, "tc": tc, "mod_row": mod_row,
        "tm_proj": min(1024, tile_cap), "tm_merge": min(512, tile_cap), "tm_moe": min(512, tile_cap),
    }

    wts = _prep_weights(w_in, w_branch, w_out, router_w, router_b, sg_b, exp_w1, exp_b1, exp_w2, exp_b2,
                        attn_q_norm, attn_k_norm)
    rope_cos, rope_sin = _rope_tables(n_lat)

    mod_rows = -(-(1 + b) // SUBLANES) * SUBLANES
    c_all = jnp.concatenate([c_ctx[None, :], c, jnp.zeros((mod_rows - 1 - b, dm), F32)], axis=0)
    mod = _modulation(c_all, w_mod, b_mod).reshape(depth, mod_rows, N_MOD, dm)

    xs_flat = jnp.concatenate([ctx.reshape(tc, dm), x.reshape(b * n_lat, dm)], axis=0)
    n_blocks = -(-(t * TOP_K) // MOE_BM) + N_EXPERTS
    n_slots = n_blocks * MOE_BM

    for l in range(depth):
        last = l == depth - 1
        p = _in_proj(xs_flat, mod, norm1[:, None, :], wts["w_in"], l, geo)
        g_lat, g_ctx = _gdn(p, gdn_conv, gdn_a_log, gdn_dt_bias, gdn_out_norm[:, None, :], l, geo)
        a_lat = _attention(p, attn_sink, wts["q_norm"], wts["k_norm"], rope_cos, rope_sin, l, geo, False)
        if last:
            a_ctx = jnp.zeros((tc, ATT_HEADS * HEAD_DIM), BF16)
        else:
            a_ctx = _attention(p, attn_sink, wts["q_norm"], wts["k_norm"], rope_cos, rope_sin, l, geo, True)
        y0 = jnp.concatenate([g_ctx, g_lat], axis=0)
        y3 = jnp.concatenate([a_ctx, a_lat], axis=0)
        layer_w = {
            "sc_conv": sc_conv, "sg_ln_w": sg_ln_w[:, None, :], "sg_ln_b": sg_ln_b[:, None, :], "sg_w": sg_w.astype(BF16),
            "sg_b": wts["sg_b"], "w_branch": wts["w_branch"], "w_out": wts["w_out"],
            "router_w": wts["router_w"], "router_b": wts["router_b"],
        }
        x_mid, h2p, idx, gate, rank, counts = _merge(xs_flat, mod, norm2[:, None, :], y0, y3, p, layer_w, l, geo)
        dest, block_expert = _moe_plan(idx, rank, counts[0, :N_EXPERTS], n_blocks)
        xs_sorted = _dispatch(h2p, dest, n_slots, geo)
        ys_sorted = _experts(xs_sorted, block_expert, wts["w1"], wts["b1"], wts["w2"], wts["b2"], l)
        xs_flat = _combine(x_mid, mod, gate, dest, ys_sorted, l, geo)

    return xs_flat[tc:].reshape(b, n_lat, dm)
```

```python
import functools
import math

import jax
import jax.numpy as jnp
from jax import lax
from jax.experimental import pallas as pl
from jax.experimental.pallas import tpu as pltpu

F32 = jnp.float32
BF16 = jnp.bfloat16
I32 = jnp.int32
U32 = jnp.uint32
HIGHEST = lax.Precision.HIGHEST

D_MODEL = 1024
GRID_W = 64
N_MOD = 6
GDN_HEADS = 4
GDN_DK = 128
GDN_DV = 128
GDN_CONV = 5
SC_WIDTH = 512
SC_CONV = 3
SG_CHUNK = 128
SG_GROUPS = 4
SG_WIDTH = 512
ATT_HEADS = 8
ATT_KV_HEADS = 2
HEAD_DIM = 64
WINDOW = 128
ROPE_BASE = 10000.0
N_BRANCH = 4
BRANCH_W = 512
N_EXPERTS = 32
TOP_K = 4
D_EXPERT = 1024
SWIGLU_ALPHA = 1.702
SWIGLU_LIMIT = 7.0
NORM_EPS = 1e-6
LN_EPS = 1e-5
NEG_INF = -1e30

LANES = 128
SUBLANES = 8
VMEM_LIMIT_BYTES = 56 * 2**20

COL_Q = 0
COL_K = 512
COL_V = 1024
COL_Z = 1536
COL_SB = 2048
COL_SC = 2560
COL_SX = 3072
COL_GU = 3584
COL_GV = 4096
COL_AQ = 4608
COL_MG = 5120
COL_AK = 9216
COL_AV = 9344
COL_AB = 9472
PROJ_COLS = 10240
PROJ_TN = 2048
PROJ_CHUNK = 512

GDN_CHUNK = 128
GDN_SOLVE_BLOCK = 64
GDN_PREP_UNROLL = 4
MOE_BM = 512
W1_GROUP = 256


def _cparams(sem):
    return pltpu.CompilerParams(dimension_semantics=sem, vmem_limit_bytes=VMEM_LIMIT_BYTES)


def _silu(x):
    return x * jax.nn.sigmoid(x)


def _dot(a, b, precision=None):
    return jnp.dot(a, b, preferred_element_type=F32, precision=precision)


def _dot_nt(a, b):
    return lax.dot_general(a, b, (((1,), (1,)), ((), ())), preferred_element_type=F32)


def _dot_tn(a, b):
    return lax.dot_general(a, b, (((0,), (0,)), ((), ())), preferred_element_type=F32)


def _aligned(start, multiple):
    return start if isinstance(start, int) else pl.multiple_of(start, multiple)


def _pack_bf16_pair(lo, hi):
    lo_b = pltpu.bitcast(lo.astype(BF16).astype(F32), U32)
    hi_b = pltpu.bitcast(hi.astype(BF16).astype(F32), U32)
    return (hi_b & jnp.uint32(0xFFFF0000)) | (lo_b >> 16)


def _unpack_bf16_pair(u):
    lo = pltpu.bitcast(u << 16, F32)
    hi = pltpu.bitcast(u & jnp.uint32(0xFFFF0000), F32)
    return lo, hi


def _mod_kernel(c_ref, w_ref, b_ref, o_ref):
    s = _silu(c_ref[...])
    o_ref[...] = _dot(s, w_ref[...], precision=HIGHEST) + b_ref[...]


def _modulation(c_all, w_mod, b_mod):
    depth, dm, nm = w_mod.shape
    rows = c_all.shape[0]
    tn = 1536
    return pl.pallas_call(
        _mod_kernel,
        out_shape=jax.ShapeDtypeStruct((depth, rows, nm), F32),
        grid=(depth, nm // tn),
        in_specs=[
            pl.BlockSpec((rows, dm), lambda l, j: (0, 0)),
            pl.BlockSpec((None, dm, tn), lambda l, j: (l, 0, j)),
            pl.BlockSpec((None, 1, tn), lambda l, j: (l, 0, j)),
        ],
        out_specs=pl.BlockSpec((None, rows, tn), lambda l, j: (l, 0, j)),
        compiler_params=_cparams(("arbitrary", "arbitrary")),
        name="modulation",
    )(c_all, w_mod, b_mod.reshape(depth, 1, nm))


def _in_proj_kernel(x_ref, mod_ref, nw_ref, w_ref, o_ref):
    x = x_ref[...]
    ms = jnp.mean(x * x, axis=-1, keepdims=True)
    y = x * lax.rsqrt(ms + NORM_EPS) * nw_ref[...]
    h = (y * (1.0 + mod_ref[1:2, :]) + mod_ref[0:1, :]).astype(BF16)
    for c in range(PROJ_TN // PROJ_CHUNK):
        sl = slice(c * PROJ_CHUNK, (c + 1) * PROJ_CHUNK)
        o_ref[:, sl] = _dot(h, w_ref[:, sl]).astype(BF16)


def _in_proj(x, mod, norm_w, w_p, l, geo):
    t, dm = x.shape
    tm = geo["tm_proj"]
    mod_row = geo["mod_row"]
    return pl.pallas_call(
        _in_proj_kernel,
        out_shape=jax.ShapeDtypeStruct((t, PROJ_COLS), BF16),
        grid=(PROJ_COLS // PROJ_TN, t // tm),
        in_specs=[
            pl.BlockSpec((tm, dm), lambda j, i: (i, 0)),
            pl.BlockSpec((None, None, N_MOD, dm), lambda j, i: (l, mod_row(i * tm), 0, 0)),
            pl.BlockSpec((None, 1, dm), lambda j, i: (l, 0, 0)),
            pl.BlockSpec((None, dm, PROJ_TN), lambda j, i: (l, 0, j)),
        ],
        out_specs=pl.BlockSpec((tm, PROJ_TN), lambda j, i: (i, j)),
        compiler_params=_cparams(("arbitrary", "arbitrary")),
        name="in_proj",
    )(x, mod, norm_w, w_p)


def _softplus(x):
    return jnp.maximum(x, 0.0) + jnp.log1p(jnp.exp(-jnp.abs(x)))


def _gdn_kernel(alog_ref, dtb_ref,
                ql_ref, kl_ref, vl_ref, zl_ref, abl_ref,
                qc_ref, kc_ref, vc_ref, zc_ref, abc_ref,
                cwq_ref, cwk_ref, cwv_ref, onorm_ref,
                yl_ref, yc_ref,
                xpad, qs, ks, vs, gsc, bsc,
                u_sc, w16, qk16, qd16, kdk16, glast, oacc,
                *, n_ctx, n_lat):
    cs = GDN_CHUNK
    hd = GDN_DK
    n_all = n_ctx + n_lat
    n_chunks = n_all // cs
    n_ctx_chunks = n_ctx // cs
    h = pl.program_id(1)
    pad = SUBLANES
    lat_off = n_ctx + 3 * pad

    zeros_pad = jnp.zeros((pad, hd), F32)
    xpad[0:pad, :] = zeros_pad
    xpad[pad + n_ctx:pad + n_ctx + 2 * pad, :] = jnp.zeros((2 * pad, hd), F32)
    xpad[lat_off + n_lat:lat_off + n_lat + pad, :] = zeros_pad

    half = GDN_CONV // 2
    tile = 256

    def conv_into(src_c, src_l, cw_ref, dst, mode):
        xpad[pad:pad + n_ctx, :] = src_c[...].astype(F32)
        xpad[lat_off:lat_off + n_lat, :] = src_l[...].astype(F32)
        w = cw_ref[...]
        for seg_off, dst_off, seg_len in ((pad, 0, n_ctx), (lat_off, n_ctx, n_lat)):
            for r0 in range(0, seg_len, tile):
                acc = jnp.zeros((tile, hd), F32)
                for j in range(GDN_CONV):
                    s = seg_off + r0 + j - half
                    acc = acc + xpad[s:s + tile, :] * w[j:j + 1, :]
                y = _silu(acc)
                if mode != "v":
                    y = y * lax.rsqrt(jnp.sum(y * y, axis=-1, keepdims=True) + NORM_EPS)
                if mode == "q":
                    y = y * (hd ** -0.5)
                dst[dst_off + r0:dst_off + r0 + tile, :] = y

    conv_into(qc_ref, ql_ref, cwq_ref, qs, "q")
    conv_into(kc_ref, kl_ref, cwk_ref, ks, "k")
    conv_into(vc_ref, vl_ref, cwv_ref, vs, "v")

    sel_r = lax.broadcasted_iota(I32, (LANES, LANES), 0)
    pos = lax.broadcasted_iota(I32, (tile, LANES), 0) % cs
    for d in range(2):
        col = d * GDN_HEADS + h
        sel_a = (sel_r == col).astype(BF16)
        sel_b = (sel_r == (2 * GDN_HEADS + col)).astype(BF16)
        neg_a = -jnp.exp(jnp.full((1, LANES), alog_ref[d, h], F32))
        dtb = dtb_ref[d, h]
        for src, r_off, r_len in ((abc_ref, 0, n_ctx), (abl_ref, n_ctx, n_lat)):
            for r0 in range(0, r_len, tile):
                ab = src[r0:r0 + tile, :]
                a_b = _dot(ab, sel_a)
                b_b = _dot(ab, sel_b)
                gc = neg_a * _softplus(a_b + dtb)
                s = 1
                while s < cs:
                    if d == 0:
                        gc = gc + jnp.where(pos >= s, pltpu.roll(gc, s, 0), 0.0)
                    else:
                        gc = gc + jnp.where(pos < cs - s, pltpu.roll(gc, tile - s, 0), 0.0)
                    s *= 2
                gsc[d, r_off + r0:r_off + r0 + tile, :] = gc
                bsc[d, r_off + r0:r_off + r0 + tile, :] = jax.nn.sigmoid(b_b)

    ri = lax.broadcasted_iota(I32, (cs, cs), 0)
    ci = lax.broadcasted_iota(I32, (cs, cs), 1)
    low = (ri >= ci)
    up = (ri <= ci)
    eye = (ri == ci).astype(F32)
    offdiag = (ri != ci).astype(F32)
    assert cs in (GDN_SOLVE_BLOCK, 2 * GDN_SOLVE_BLOCK)
    same_blk = ((ri // GDN_SOLVE_BLOCK) == (ci // GDN_SOLVE_BLOCK)).astype(F32)
    n_levels = int(math.log2(min(cs, GDN_SOLVE_BLOCK))) - 1

    def prep_group(first_chunk, count):
        chunk_ids = [first_chunk + uu for uu in range(count)]
        rows = [pl.ds(_aligned(c * cs, cs), cs) for c in chunk_ids]
        q = [qs[r, :] for r in rows]
        k = [ks[r, :] for r in rows]
        v = [vs[r, :] for r in rows]
        kb = [x.astype(BF16) for x in k]
        kkt = [_dot_nt(x, x) for x in kb]
        qkt = [_dot_nt(a.astype(BF16), b) for a, b in zip(q, kb)]
        probs = [(ui, d) for ui in range(count) for d in range(2)]
        gc_col, beta_b, dmat, t, p, n_off = {}, {}, {}, {}, {}, {}
        for ui, d in probs:
            gc = gsc[d, rows[ui], :]
            bt = bsc[d, rows[ui], :]
            gsq = gc[:, :cs]
            dm = jnp.where(low if d == 0 else up, jnp.exp(gsq - gsq.T), 0.0)
            n = -(kkt[ui] * bt[:, :cs] * dm * offdiag)
            gc_col[ui, d], beta_b[ui, d], dmat[ui, d] = gc, bt, dm
            n_diag = n * same_blk
            t[ui, d] = eye + n_diag
            p[ui, d] = n_diag.astype(BF16)
            n_off[ui, d] = (n - n_diag).astype(BF16)
        p = {key: _dot(x, x).astype(BF16) for key, x in p.items()}
        for lvl in range(n_levels):
            t_next = {key: t[key] + _dot(p[key], t[key].astype(BF16)) for key in probs}
            if lvl + 1 < n_levels:
                p = {key: _dot(x, x).astype(BF16) for key, x in p.items()}
            t = t_next
        tb = {key: x.astype(BF16) for key, x in t.items()}
        if cs > GDN_SOLVE_BLOCK:
            x_off = {key: _dot(tb[key], n_off[key]).astype(BF16) for key in probs}
            t = {key: t[key] + _dot(x_off[key], tb[key]) for key in probs}
            tb = {key: x.astype(BF16) for key, x in t.items()}
        eg = {key: jnp.exp(x) for key, x in gc_col.items()}
        u = {(ui, d): _dot(tb[ui, d], (v[ui] * beta_b[ui, d]).astype(BF16)) for ui, d in probs}
        w = {(ui, d): _dot(tb[ui, d], (k[ui] * beta_b[ui, d] * eg[ui, d]).astype(BF16)) for ui, d in probs}
        for ui, d in probs:
            r = rows[ui]
            last = cs - 1 if d == 0 else 0
            gl = gc_col[ui, d][last:last + 1, :]
            u_sc[d, r, :] = u[ui, d]
            w16[d, r, :] = w[ui, d].astype(BF16)
            qk16[d, r, :] = (qkt[ui] * dmat[ui, d]).astype(BF16)
            qd16[d, r, :] = (q[ui] * eg[ui, d]).astype(BF16)
            kdk16[d, r, :] = (k[ui] * jnp.exp(gl - gc_col[ui, d])).astype(BF16)
            g0 = _aligned(chunk_ids[ui] * SUBLANES, SUBLANES)
            glast[d, pl.ds(g0, SUBLANES), :] = jnp.broadcast_to(jnp.exp(gl), (SUBLANES, LANES))

    def prep(gi, carry):
        prep_group(gi * GDN_PREP_UNROLL, GDN_PREP_UNROLL)
        return carry

    n_groups = n_chunks // GDN_PREP_UNROLL
    lax.fori_loop(0, n_groups, prep, 0)
    if n_chunks % GDN_PREP_UNROLL:
        prep_group(n_groups * GDN_PREP_UNROLL, n_chunks % GDN_PREP_UNROLL)

    oacc[...] = jnp.zeros_like(oacc)

    def step(i, carry):
        c_b = jnp.where(i < n_ctx_chunks, n_ctx_chunks - 1 - i, n_chunks + n_ctx_chunks - 1 - i)
        cid = (i, c_b)
        rows = [pl.ds(pl.multiple_of(c * cs, cs), cs) for c in cid]
        sb = [s.astype(BF16) for s in carry]
        ws = [_dot(w16[d, rows[d], :], sb[d]) for d in range(2)]
        qs_s = [_dot(qd16[d, rows[d], :], sb[d]) for d in range(2)]
        vnb = [(u_sc[d, rows[d], :] - ws[d]).astype(BF16) for d in range(2)]
        o = [qs_s[d] + _dot(qk16[d, rows[d], :], vnb[d]) for d in range(2)]
        kv = [_dot_tn(kdk16[d, rows[d], :], vnb[d]) for d in range(2)]
        s_new = []
        for d in range(2):
            g0 = pl.multiple_of(cid[d] * SUBLANES, SUBLANES)
            s_new.append(carry[d] * glast[d, pl.ds(g0, 1), :] + kv[d])
            oacc[rows[d], :] = oacc[rows[d], :] + o[d]
        return tuple(s_new)

    s0 = jnp.zeros((hd, GDN_DV), F32)
    lax.fori_loop(0, n_chunks, step, (s0, s0))

    onw = onorm_ref[...]

    def out_gate(o, z):
        y = o * lax.rsqrt(jnp.mean(o * o, axis=-1, keepdims=True) + NORM_EPS) * onw
        return (y * _silu(z.astype(F32))).astype(BF16)

    yc_ref[...] = out_gate(oacc[0:n_ctx, :], zc_ref[...])
    for r0 in range(0, n_lat, tile):
        yl_ref[r0:r0 + tile, :] = out_gate(oacc[n_ctx + r0:n_ctx + r0 + tile, :], zl_ref[r0:r0 + tile, :])


def _gdn(p, conv_w, a_log, dt_bias, out_norm, l, geo):
    b, n_ctx, n_lat = geo["b"], geo["ctx"], geo["lat"]
    n_all = n_ctx + n_lat
    lat0 = geo["tc"] // n_lat
    hd = GDN_DK

    def lat_spec(col):
        return pl.BlockSpec((n_lat, hd), lambda bi, h: (lat0 + bi, col // hd + h))

    def ctx_spec(col):
        return pl.BlockSpec((n_ctx, hd), lambda bi, h: (bi, col // hd + h))

    def fixed_lat(col):
        return pl.BlockSpec((n_lat, hd), lambda bi, h: (lat0 + bi, col // hd))

    def fixed_ctx(col):
        return pl.BlockSpec((n_ctx, hd), lambda bi, h: (bi, col // hd))

    def cw_spec(col):
        return pl.BlockSpec((None, GDN_CONV, hd), lambda bi, h: (l, 0, col // hd + h))

    smem = pl.BlockSpec(memory_space=pltpu.SMEM)
    kern = functools.partial(_gdn_kernel, n_ctx=n_ctx, n_lat=n_lat)
    n_chunks = n_all // GDN_CHUNK
    assert n_ctx % GDN_CHUNK == 0 and n_lat % GDN_CHUNK == 0
    scratch = [
        pltpu.VMEM((n_all + 4 * SUBLANES, hd), F32),
        pltpu.VMEM((n_all, hd), F32),
        pltpu.VMEM((n_all, hd), F32),
        pltpu.VMEM((n_all, hd), F32),
        pltpu.VMEM((2, n_all, LANES), F32),
        pltpu.VMEM((2, n_all, LANES), F32),
        pltpu.VMEM((2, n_all, GDN_DV), F32),
        pltpu.VMEM((2, n_all, hd), BF16),
        pltpu.VMEM((2, n_all, GDN_CHUNK), BF16),
        pltpu.VMEM((2, n_all, hd), BF16),
        pltpu.VMEM((2, n_all, hd), BF16),
        pltpu.VMEM((2, n_chunks * SUBLANES, LANES), F32),
        pltpu.VMEM((n_all, GDN_DV), F32),
    ]
    y_lat, y_ctx = pl.pallas_call(
        kern,
        out_shape=(jax.ShapeDtypeStruct((b * n_lat, GDN_HEADS * GDN_DV), BF16),
                   jax.ShapeDtypeStruct((b * n_ctx, GDN_HEADS * GDN_DV), BF16)),
        grid=(b, GDN_HEADS),
        in_specs=[smem, smem,
                  lat_spec(COL_Q), lat_spec(COL_K), lat_spec(COL_V), lat_spec(COL_Z), fixed_lat(COL_AB),
                  ctx_spec(COL_Q), ctx_spec(COL_K), ctx_spec(COL_V), ctx_spec(COL_Z), fixed_ctx(COL_AB),
                  cw_spec(0), cw_spec(GDN_HEADS * hd), cw_spec(2 * GDN_HEADS * hd),
                  pl.BlockSpec((None, 1, GDN_DV), lambda bi, h: (l, 0, 0))],
        out_specs=(pl.BlockSpec((n_lat, GDN_DV), lambda bi, h: (bi, h)),
                   pl.BlockSpec((n_ctx, GDN_DV), lambda bi, h: (bi, h))),
        scratch_shapes=scratch,
        compiler_params=_cparams(("arbitrary", "arbitrary")),
        name="gdn",
    )(a_log[l], dt_bias[l], p, p, p, p, p, p, p, p, p, p, conv_w, conv_w, conv_w, out_norm)
    return y_lat, y_ctx


def _head_rms(x, bd, w):
    ss = _dot(x * x, bd, precision=HIGHEST) * (1.0 / HEAD_DIM)
    return x * lax.rsqrt(ss + NORM_EPS) * w


def _rope(x, cos, sin_signed, lane):
    nf = HEAD_DIM // 4
    partner = jnp.where((lane % (2 * nf)) < nf, pltpu.roll(x, LANES - nf, 1), pltpu.roll(x, nf, 1))
    return x * cos + partner * sin_signed


def _attn_kernel(sink_ref, q_ref, kq_ref, vq_ref, kc_ref, vc_ref,
                 cosq_ref, sinq_ref, cosk_ref, sink_k_ref, qw_ref, kw_ref,
                 o_ref, kp, vp, kcp, vcp, *, n_keys, n_ctx, local):
    n = pl.program_id(1)
    w = WINDOW
    lane = lax.broadcasted_iota(I32, (1, LANES), 1)
    lo_mask = (lane < HEAD_DIM).astype(F32)
    hi_mask = 1.0 - lo_mask
    bd_r = lax.broadcasted_iota(I32, (LANES, LANES), 0) // HEAD_DIM
    bd_c = lax.broadcasted_iota(I32, (LANES, LANES), 1) // HEAD_DIM
    bd = (bd_r == bd_c).astype(F32)
    kw = kw_ref[...]

    def store_variants(dst, row0, x):
        xr = pltpu.roll(x, HEAD_DIM, 1)
        rows = x.shape[0]
        dst[0, row0:row0 + rows, :] = (x * lo_mask).astype(BF16)
        dst[1, row0:row0 + rows, :] = (xr * hi_mask).astype(BF16)
        dst[2, row0:row0 + rows, :] = (xr * lo_mask).astype(BF16)
        dst[3, row0:row0 + rows, :] = (x * hi_mask).astype(BF16)

    @pl.when(n == 0)
    def _prepare_keys():
        tile = 256
        for r0 in range(0, n_ctx, tile):
            rr = min(tile, n_ctx - r0)
            kc = _head_rms(kc_ref[r0:r0 + rr, :].astype(F32), bd, kw)
            store_variants(kcp, r0, kc)
            store_variants(vcp, r0, vc_ref[r0:r0 + rr, :].astype(F32))
        if local:
            zero = jnp.zeros((w, LANES), BF16)
            for t in range(4):
                kp[t, 0:w, :] = zero
                kp[t, w + n_keys:2 * w + n_keys, :] = zero
                vp[t, 0:w, :] = zero
                vp[t, w + n_keys:2 * w + n_keys, :] = zero
            for r0 in range(0, n_keys, tile):
                k = _head_rms(kq_ref[r0:r0 + tile, :].astype(F32), bd, kw)
                k = _rope(k, cosk_ref[r0:r0 + tile, :], sink_k_ref[r0:r0 + tile, :], lane)
                store_variants(kp, w + r0, k)
                store_variants(vp, w + r0, vq_ref[r0:r0 + tile, :].astype(F32))

    qw = qw_ref[...]
    q2 = []
    for g in range(ATT_HEADS // 2):
        qg = _head_rms(q_ref[:, g * LANES:(g + 1) * LANES].astype(F32), bd, qw)
        if local:
            qg = _rope(qg, cosq_ref[...], sinq_ref[...], lane)
        q2.append((qg * (HEAD_DIM ** -0.5)).astype(BF16))

    rows2 = 2 * w
    row_i = lax.broadcasted_iota(I32, (rows2, 1), 0)
    first = row_i < w
    if local:
        n_blk = n_keys // w
        qq = lax.broadcasted_iota(I32, (rows2, 3 * w), 0) % w
        kcol = lax.broadcasted_iota(I32, (rows2, 3 * w), 1)
        kk = kcol % w
        blk = kcol // w
        valid = ((blk == 1)
                 | ((blk == 0) & (kk >= qq) & (n >= 1))
                 | ((blk == 2) & (kk <= qq) & (n < n_blk - 1)))
        k0 = pl.multiple_of(n * w, w)

    for j in range(ATT_KV_HEADS):
        lhs = jnp.concatenate([q2[2 * j], q2[2 * j + 1]], axis=0)
        pair_out = [None, None]
        for t in range(2):
            var = 2 * j + t
            sink = jnp.where(first, sink_ref[4 * j + t], sink_ref[4 * j + 2 + t])
            s_ctx = _dot_nt(lhs, kcp[var])
            m = jnp.maximum(jnp.max(s_ctx, axis=-1, keepdims=True), sink)
            if local:
                s_loc = _dot_nt(lhs, kp[var, pl.ds(k0, 3 * w), :])
                s_loc = jnp.where(valid, s_loc, NEG_INF)
                m = jnp.maximum(m, jnp.max(s_loc, axis=-1, keepdims=True))
            e_ctx = jnp.exp(s_ctx - m)
            den = jnp.sum(e_ctx, axis=-1, keepdims=True) + jnp.exp(sink - m)
            pv = _dot(e_ctx.astype(BF16), vcp[var])
            if local:
                e_loc = jnp.exp(s_loc - m)
                den = den + jnp.sum(e_loc, axis=-1, keepdims=True)
                pv = pv + _dot(e_loc.astype(BF16), vp[var, pl.ds(k0, 3 * w), :])
            o = pv / den
            for half_i in range(2):
                part = o[half_i * w:(half_i + 1) * w, :]
                pair_out[half_i] = part if pair_out[half_i] is None else pair_out[half_i] + part
        for half_i in range(2):
            c0 = (2 * j + half_i) * LANES
            o_ref[:, c0:c0 + LANES] = pair_out[half_i].astype(BF16)


def _attention(p, sink, q_norm_w, k_norm_w, rope_cos, rope_sin, l, geo, ctx_queries):
    b, n_ctx, n_lat = geo["b"], geo["ctx"], geo["lat"]
    w = WINDOW
    lat0 = geo["tc"] // n_lat
    local = not ctx_queries
    n_q = n_ctx if ctx_queries else n_lat
    nqb = n_q // w
    q_row0 = 0 if ctx_queries else geo["tc"] // w
    kern = functools.partial(_attn_kernel, n_keys=n_lat, n_ctx=n_ctx, local=local)
    in_specs = [
        pl.BlockSpec(memory_space=pltpu.SMEM),
        pl.BlockSpec((w, ATT_HEADS * HEAD_DIM), lambda bi, n: (q_row0 + bi * nqb + n, COL_AQ // 512)),
        pl.BlockSpec((n_lat, LANES), lambda bi, n: (lat0 + bi, COL_AK // LANES)),
        pl.BlockSpec((n_lat, LANES), lambda bi, n: (lat0 + bi, COL_AV // LANES)),
        pl.BlockSpec((n_ctx, LANES), lambda bi, n: (bi, COL_AK // LANES)),
        pl.BlockSpec((n_ctx, LANES), lambda bi, n: (bi, COL_AV // LANES)),
        pl.BlockSpec((w, LANES), lambda bi, n: (n if local else 0, 0)),
        pl.BlockSpec((w, LANES), lambda bi, n: (n if local else 0, 0)),
        pl.BlockSpec((n_lat, LANES), lambda bi, n: (0, 0)),
        pl.BlockSpec((n_lat, LANES), lambda bi, n: (0, 0)),
        pl.BlockSpec((None, 1, LANES), lambda bi, n: (l, 0, 0)),
        pl.BlockSpec((None, 1, LANES), lambda bi, n: (l, 0, 0)),
    ]
    scratch = [
        pltpu.VMEM((4, n_lat + 2 * w, LANES), BF16),
        pltpu.VMEM((4, n_lat + 2 * w, LANES), BF16),
        pltpu.VMEM((4, n_ctx, LANES), BF16),
        pltpu.VMEM((4, n_ctx, LANES), BF16),
    ]
    return pl.pallas_call(
        kern,
        out_shape=jax.ShapeDtypeStruct((b * n_q, ATT_HEADS * HEAD_DIM), BF16),
        grid=(b, nqb),
        in_specs=in_specs,
        out_specs=pl.BlockSpec((w, ATT_HEADS * HEAD_DIM), lambda bi, n: (bi * nqb + n, 0)),
        scratch_shapes=scratch,
        compiler_params=_cparams(("arbitrary", "arbitrary")),
        name="attn_ctx" if ctx_queries else "attn_lat",
    )(sink[l], p, p, p, p, p, rope_cos, rope_sin, rope_cos, rope_sin, q_norm_w, k_norm_w)


def _merge_kernel(x_ref, mod_ref, n2_ref, y0_ref, y3_ref,
                  sb_ref, sc_ref, sx_ref, scp_ref, sxp_ref, scn_ref, sxn_ref,
                  gu_ref, gv_ref, mg0_ref, mg1_ref, mg2_ref, mg3_ref,
                  scw_ref, lnw_ref, lnb_ref, sgw_ref, sgb_ref, wb_ref, wo_ref, rw_ref, rb_ref,
                  xo_ref, h2_ref, idx_ref, gate_ref, rank_ref, cnt_ref,
                  carry, *, tm, tc, n_ctx, n_lat):
    i = pl.program_id(0)
    r0 = i * tm

    row = lax.broadcasted_iota(I32, (tm, 1), 0)
    g_row = r0 + row
    in_ctx = g_row < tc
    seg_pos = jnp.where(in_ctx, g_row % n_ctx, (g_row - tc) % n_lat)
    seg_len = jnp.where(in_ctx, n_ctx, n_lat)
    cx = sc_ref[...].astype(F32) * sx_ref[...].astype(F32)
    cx_prev_halo = scp_ref[SUBLANES - 1:SUBLANES, :].astype(F32) * sxp_ref[SUBLANES - 1:SUBLANES, :].astype(F32)
    cx_next_halo = scn_ref[0:1, :].astype(F32) * sxn_ref[0:1, :].astype(F32)
    prev = jnp.where(row == 0, cx_prev_halo, pltpu.roll(cx, 1, 0))
    prev = jnp.where(seg_pos == 0, 0.0, prev)
    nxt = jnp.where(row == tm - 1, cx_next_halo, pltpu.roll(cx, tm - 1, 0))
    nxt = jnp.where(seg_pos == seg_len - 1, 0.0, nxt)
    scw = scw_ref[...]
    y1 = sb_ref[...].astype(F32) * (prev * scw[0:1, :] + cx * scw[1:2, :] + nxt * scw[2:3, :])

    inv_sqrt2 = 1.0 / math.sqrt(2.0)

    def gelu(t):
        return 0.5 * t * (1.0 + lax.erf(t * inv_sqrt2))

    u = gelu(gu_ref[...].astype(F32))
    v = gelu(gv_ref[...].astype(F32))
    mu = jnp.mean(v, axis=-1, keepdims=True)
    vc = v - mu
    v = vc * lax.rsqrt(jnp.mean(vc * vc, axis=-1, keepdims=True) + LN_EPS) * lnw_ref[...] + lnb_ref[...]
    vb = v.astype(BF16)
    gw = SG_WIDTH // SG_GROUPS
    chunks = []
    for c in range(tm // SG_CHUNK):
        groups = []
        for g in range(SG_GROUPS):
            groups.append(_dot(sgw_ref[g], vb[c * SG_CHUNK:(c + 1) * SG_CHUNK, g * gw:(g + 1) * gw]))
        chunks.append(jnp.concatenate(groups, axis=1) + sgb_ref[...])
    y2 = u * jnp.concatenate(chunks, axis=0)

    ys = (y0_ref[...], y1.astype(BF16), y2.astype(BF16), y3_ref[...])
    gates = (mg0_ref, mg1_ref, mg2_ref, mg3_ref)
    m = None
    for br in range(N_BRANCH):
        term = jax.nn.sigmoid(gates[br][...].astype(F32)) * _dot(ys[br], wb_ref[br])
        m = term if m is None else m + term
    y = _dot(m.astype(BF16), wo_ref[...])
    x_new = x_ref[...] + mod_ref[2:3, :] * y
    xo_ref[...] = x_new

    ms = jnp.mean(x_new * x_new, axis=-1, keepdims=True)
    h2 = x_new * lax.rsqrt(ms + NORM_EPS) * n2_ref[...] * (1.0 + mod_ref[4:5, :]) + mod_ref[3:4, :]
    half = D_MODEL // 2
    h2_ref[...] = _pack_bf16_pair(h2[:, :half], h2[:, half:])

    h2_hi = h2.astype(BF16)
    h2_lo = (h2 - h2_hi.astype(F32)).astype(BF16)
    prod = _dot(jnp.concatenate([h2_hi, h2_lo], axis=1), rw_ref[...])
    lane = lax.broadcasted_iota(I32, (tm, LANES), 1)
    logits = prod + pltpu.roll(prod, LANES - N_EXPERTS, 1)
    logits = jnp.where(lane < N_EXPERTS, logits, NEG_INF) + rb_ref[...]
    lane_f = lane.astype(F32)
    work = logits
    topv = jnp.full((tm, LANES), NEG_INF, F32)
    topi = jnp.zeros((tm, LANES), I32)
    onehot = jnp.zeros((tm, LANES), F32)
    firsts = []
    for k in range(TOP_K):
        mx = jnp.max(work, axis=-1, keepdims=True)
        first_f = jnp.min(jnp.where(work == mx, lane_f, float(LANES)), axis=-1, keepdims=True)
        first = first_f.astype(I32)
        hit = lane == first
        topv = jnp.where(lane == k, mx, topv)
        topi = jnp.where(lane == k, first, topi)
        onehot = jnp.where(hit, 1.0, onehot)
        work = jnp.where(hit, -jnp.inf, work)
        firsts.append(first)
    e = jnp.where(lane < TOP_K, jnp.exp(topv - jnp.max(topv, axis=-1, keepdims=True)), 0.0)
    gate = e / jnp.sum(e, axis=-1, keepdims=True)

    @pl.when(i == 0)
    def _init():
        carry[...] = jnp.zeros_like(carry)

    tri = (lax.broadcasted_iota(I32, (tm, tm), 0) > lax.broadcasted_iota(I32, (tm, tm), 1)).astype(BF16)
    before = carry[0:1, :] + _dot(tri, onehot.astype(BF16))
    rank = jnp.zeros((tm, LANES), F32)
    for k in range(TOP_K):
        r_k = jnp.sum(jnp.where(lane == firsts[k], before, 0.0), axis=-1, keepdims=True)
        rank = jnp.where(lane == k, r_k, rank)
    new_carry = carry[0:1, :] + jnp.sum(onehot, axis=0, keepdims=True)
    carry[...] = jnp.broadcast_to(new_carry, carry.shape)
    cnt_ref[...] = jnp.broadcast_to(new_carry, cnt_ref.shape).astype(I32)
    idx_ref[...] = topi[:, :SUBLANES]
    gate_ref[...] = gate[:, :SUBLANES]
    rank_ref[...] = rank[:, :SUBLANES].astype(I32)


def _merge(x, mod, norm2, y0, y3, p, w, l, geo):
    t, dm = x.shape
    tm = geo["tm_merge"]
    mod_row = geo["mod_row"]
    kern = functools.partial(_merge_kernel, tm=tm, tc=geo["tc"], n_ctx=geo["ctx"], n_lat=geo["lat"])
    halo = tm // SUBLANES
    n_halo = t // SUBLANES

    def col(c, width=512):
        return pl.BlockSpec((tm, width), lambda i: (i, c // width))

    def prev_halo(c):
        return pl.BlockSpec((SUBLANES, 512), lambda i: (jnp.maximum(i * halo - 1, 0), c // 512))

    def next_halo(c):
        return pl.BlockSpec((SUBLANES, 512), lambda i: (jnp.minimum((i + 1) * halo, n_halo - 1), c // 512))

    def layer(shape):
        nd = len(shape)
        return pl.BlockSpec((None,) + shape, lambda i: (l,) + (0,) * nd)

    in_specs = [
        pl.BlockSpec((tm, dm), lambda i: (i, 0)),
        pl.BlockSpec((None, None, N_MOD, dm), lambda i: (l, mod_row(i * tm), 0, 0)),
        layer((1, dm)),
        pl.BlockSpec((tm, 512), lambda i: (i, 0)),
        pl.BlockSpec((tm, 512), lambda i: (i, 0)),
        col(COL_SB), col(COL_SC), col(COL_SX),
        prev_halo(COL_SC), prev_halo(COL_SX), next_halo(COL_SC), next_halo(COL_SX),
        col(COL_GU), col(COL_GV),
        col(COL_MG, 1024), col(COL_MG + 1024, 1024), col(COL_MG + 2048, 1024), col(COL_MG + 3072, 1024),
        layer((SC_CONV, SC_WIDTH)), layer((1, SG_WIDTH)), layer((1, SG_WIDTH)),
        layer((SG_GROUPS, SG_CHUNK, SG_CHUNK)), layer((SG_CHUNK, SG_WIDTH)),
        layer((N_BRANCH, BRANCH_W, dm)), layer((dm, dm)), layer((2 * dm, LANES)), layer((1, LANES)),
    ]
    out_shape = (
        jax.ShapeDtypeStruct((t, dm), F32),
        jax.ShapeDtypeStruct((t, dm // 2), U32),
        jax.ShapeDtypeStruct((t, SUBLANES), I32),
        jax.ShapeDtypeStruct((t, SUBLANES), F32),
        jax.ShapeDtypeStruct((t, SUBLANES), I32),
        jax.ShapeDtypeStruct((SUBLANES, LANES), I32),
    )
    out_specs = (
        pl.BlockSpec((tm, dm), lambda i: (i, 0)),
        pl.BlockSpec((tm, dm // 2), lambda i: (i, 0)),
        pl.BlockSpec((tm, SUBLANES), lambda i: (i, 0)),
        pl.BlockSpec((tm, SUBLANES), lambda i: (i, 0)),
        pl.BlockSpec((tm, SUBLANES), lambda i: (i, 0)),
        pl.BlockSpec((SUBLANES, LANES), lambda i: (0, 0)),
    )
    return pl.pallas_call(
        kern,
        out_shape=out_shape,
        grid=(t // tm,),
        in_specs=in_specs,
        out_specs=out_specs,
        scratch_shapes=[pltpu.VMEM((SUBLANES, LANES), F32)],
        compiler_params=_cparams(("arbitrary",)),
        name="merge",
    )(x, mod, norm2, y0, y3, p, p, p, p, p, p, p, p, p, p, p, p, p,
      w["sc_conv"], w["sg_ln_w"], w["sg_ln_b"], w["sg_w"], w["sg_b"], w["w_branch"], w["w_out"],
      w["router_w"], w["router_b"])


def _dispatch_kernel(dest_ref, h_ref, xs_in_ref, xs_ref, sem, *, tm):
    del xs_in_ref

    def row_copy(t, d):
        return pltpu.make_async_copy(h_ref.at[pl.ds(t, 1), :], xs_ref.at[pl.ds(d, 1), :], sem)

    def issue(t, carry):
        for k in range(TOP_K):
            row_copy(t, dest_ref[t * TOP_K + k]).start(priority=k % 2)
        return carry

    lax.fori_loop(0, tm, issue, 0)
    for _ in range(TOP_K):
        pltpu.make_async_copy(h_ref, xs_ref.at[pl.ds(0, tm), :], sem).wait()


def _dispatch(h2p, dest_flat, n_slots, geo):
    t, half = h2p.shape
    tm = geo["tm_moe"]
    xs0 = jnp.zeros((n_slots, half), U32)
    return pl.pallas_call(
        functools.partial(_dispatch_kernel, tm=tm),
        out_shape=jax.ShapeDtypeStruct((n_slots, half), U32),
        grid=(t // tm,),
        in_specs=[
            pl.BlockSpec((tm * TOP_K,), lambda i: (i,), memory_space=pltpu.SMEM),
            pl.BlockSpec((tm, half), lambda i: (i, 0)),
            pl.BlockSpec(memory_space=pl.ANY),
        ],
        out_specs=pl.BlockSpec(memory_space=pl.ANY),
        scratch_shapes=[pltpu.SemaphoreType.DMA],
        input_output_aliases={2: 0},
        compiler_params=_cparams(("arbitrary",)),
        name="moe_dispatch",
    )(dest_flat, h2p, xs0)


def _expert_kernel(be_ref, xs_ref, w1_ref, b1_ref, w2_ref, b2_ref, ys_ref):
    del be_ref
    lo, hi = _unpack_bf16_pair(xs_ref[...])
    half = D_MODEL // 2
    h = (_dot(lo.astype(BF16), w1_ref[0:half, :]) + _dot(hi.astype(BF16), w1_ref[half:, :]) + b1_ref[...])
    hg = W1_GROUP // 2
    acts = []
    for g in range(2 * D_EXPERT // W1_GROUP):
        glu = jnp.minimum(h[:, g * W1_GROUP:g * W1_GROUP + hg], SWIGLU_LIMIT)
        lin = jnp.clip(h[:, g * W1_GROUP + hg:(g + 1) * W1_GROUP], -SWIGLU_LIMIT, SWIGLU_LIMIT)
        acts.append((glu * jax.nn.sigmoid(SWIGLU_ALPHA * glu) * (lin + 1.0)).astype(BF16))
    y = _dot(jnp.concatenate(acts, axis=1), w2_ref[...]) + b2_ref[...]
    ys_ref[...] = _pack_bf16_pair(y[:, :half], y[:, half:])


def _w1_regroup_kernel(w_ref, o_ref):
    hg = W1_GROUP // 2
    r = lax.broadcasted_iota(I32, (W1_GROUP, W1_GROUP), 0)
    c = lax.broadcasted_iota(I32, (W1_GROUP, W1_GROUP), 1)
    perm = (r == jnp.where(c < hg, 2 * c, 2 * (c - hg) + 1)).astype(BF16)
    for g in range(2 * D_EXPERT // W1_GROUP):
        sl = slice(g * W1_GROUP, (g + 1) * W1_GROUP)
        o_ref[:, sl] = _dot(w_ref[:, sl].astype(BF16), perm).astype(BF16)


def _w1_regroup(exp_w1):
    depth, ne, dm, dh = exp_w1.shape
    tr = 512
    out = pl.pallas_call(
        _w1_regroup_kernel,
        out_shape=jax.ShapeDtypeStruct((depth * ne, dm, dh), BF16),
        grid=(depth * ne, dm // tr),
        in_specs=[pl.BlockSpec((None, tr, dh), lambda e, i: (e, i, 0))],
        out_specs=pl.BlockSpec((None, tr, dh), lambda e, i: (e, i, 0)),
        compiler_params=_cparams(("arbitrary", "arbitrary")),
        name="w1_regroup",
    )(exp_w1.reshape(depth * ne, dm, dh))
    return out.reshape(depth, ne, dm, dh)


def _experts(xs, block_expert, w1p, b1p, w2, b2, l):
    n_slots, half = xs.shape
    bm = MOE_BM
    grid_spec = pltpu.PrefetchScalarGridSpec(
        num_scalar_prefetch=1,
        grid=(n_slots // bm,),
        in_specs=[
            pl.BlockSpec((bm, half), lambda i, be: (i, 0)),
            pl.BlockSpec((None, None, D_MODEL, 2 * D_EXPERT), lambda i, be: (l, be[i], 0, 0)),
            pl.BlockSpec((None, None, 1, 2 * D_EXPERT), lambda i, be: (l, be[i], 0, 0)),
            pl.BlockSpec((None, None, D_EXPERT, D_MODEL), lambda i, be: (l, be[i], 0, 0)),
            pl.BlockSpec((None, None, 1, D_MODEL), lambda i, be: (l, be[i], 0, 0)),
        ],
        out_specs=pl.BlockSpec((bm, half), lambda i, be: (i, 0)),
    )
    return pl.pallas_call(
        _expert_kernel,
        out_shape=jax.ShapeDtypeStruct((n_slots, half), U32),
        grid_spec=grid_spec,
        compiler_params=_cparams(("arbitrary",)),
        name="moe_experts",
    )(block_expert, xs, w1p, b1p, w2, b2)


def _combine_kernel(dest_ref, gate_ref, x_ref, mod_ref, ys_ref, xo_ref, buf, sem, *, tm):
    def row_copy(t, k, d):
        return pltpu.make_async_copy(ys_ref.at[pl.ds(d, 1), :], buf.at[k, pl.ds(t, 1), :], sem)

    def issue(t, carry):
        for k in range(TOP_K):
            row_copy(t, k, dest_ref[t * TOP_K + k]).start(priority=k % 2)
        return carry

    lax.fori_loop(0, tm, issue, 0)
    for k in range(TOP_K):
        pltpu.make_async_copy(ys_ref.at[pl.ds(0, tm), :], buf.at[k], sem).wait()

    gate = gate_ref[...]
    half = D_MODEL // 2
    f_lo = jnp.zeros((tm, half), F32)
    f_hi = jnp.zeros((tm, half), F32)
    for k in range(TOP_K):
        lo, hi = _unpack_bf16_pair(buf[k])
        g = gate[:, k:k + 1]
        f_lo = f_lo + g * lo
        f_hi = f_hi + g * hi
    g2 = mod_ref[5:6, :]
    xo_ref[:, :half] = x_ref[:, :half] + g2[:, :half] * f_lo
    xo_ref[:, half:] = x_ref[:, half:] + g2[:, half:] * f_hi


def _combine(x, mod, gate, dest_flat, ys, l, geo):
    t, dm = x.shape
    tm = geo["tm_moe"]
    mod_row = geo["mod_row"]
    return pl.pallas_call(
        functools.partial(_combine_kernel, tm=tm),
        out_shape=jax.ShapeDtypeStruct((t, dm), F32),
        grid=(t // tm,),
        in_specs=[
            pl.BlockSpec((tm * TOP_K,), lambda i: (i,), memory_space=pltpu.SMEM),
            pl.BlockSpec((tm, SUBLANES), lambda i: (i, 0)),
            pl.BlockSpec((tm, dm), lambda i: (i, 0)),
            pl.BlockSpec((None, None, N_MOD, dm), lambda i: (l, mod_row(i * tm), 0, 0)),
            pl.BlockSpec(memory_space=pl.ANY),
        ],
        out_specs=pl.BlockSpec((tm, dm), lambda i: (i, 0)),
        scratch_shapes=[pltpu.VMEM((TOP_K, tm, dm // 2), U32), pltpu.SemaphoreType.DMA],
        compiler_params=_cparams(("arbitrary",)),
        name="moe_combine",
    )(dest_flat, gate, x, mod, ys)


def _moe_plan(idx, rank, counts, n_blocks):
    bm = MOE_BM
    padded = (counts + bm - 1) // bm * bm
    pad_end = jnp.cumsum(padded)
    pad_start = pad_end - padded
    expert = idx[:, :TOP_K]
    onehot = expert[:, :, None] == jnp.arange(N_EXPERTS, dtype=I32)[None, None, :]
    dest = rank[:, :TOP_K] + jnp.sum(jnp.where(onehot, pad_start[None, None, :], 0), axis=-1)
    block_start = jnp.arange(n_blocks, dtype=I32) * bm
    block_expert = jnp.minimum(
        jnp.sum((block_start[:, None] >= pad_end[None, :]).astype(I32), axis=-1), N_EXPERTS - 1)
    return dest.reshape(-1).astype(I32), block_expert.astype(I32)


def _rope_tables(n_lat):
    rows = n_lat // GRID_W
    row = jnp.repeat(jnp.arange(rows, dtype=F32), GRID_W)
    colp = jnp.tile(jnp.arange(GRID_W, dtype=F32), rows)
    n_freq = HEAD_DIM // 4
    inv = ROPE_BASE ** (-jnp.arange(n_freq, dtype=F32) / n_freq)
    ang = jnp.concatenate([row[:, None] * inv, colp[:, None] * inv], axis=-1)
    cos, sin = jnp.cos(ang), jnp.sin(ang)
    cr, cc = cos[:, :n_freq], cos[:, n_freq:]
    sr, sc = sin[:, :n_freq], sin[:, n_freq:]
    cos64 = jnp.concatenate([cr, cr, cc, cc], axis=-1)
    sin64 = jnp.concatenate([-sr, sr, -sc, sc], axis=-1)
    return jnp.tile(cos64, (1, 2)), jnp.tile(sin64, (1, 2))


def _prep_weights(w_in, w_branch, w_out, router_w, router_b, sg_b, exp_w1, exp_b1, exp_w2, exp_b2,
                  attn_q_norm, attn_k_norm):
    depth, dm, _ = w_in.shape
    qkvz = 2048
    ab0, sc0, sg0, at0, mg0 = 2048, 2064, 3600, 4624, 5392
    pad = PROJ_COLS - (COL_AB + 16)
    w_p = jnp.concatenate([
        w_in[:, :, 0:qkvz],
        w_in[:, :, sc0:sg0],
        w_in[:, :, sg0:at0],
        w_in[:, :, at0:at0 + 512],
        w_in[:, :, mg0:mg0 + 4096],
        w_in[:, :, at0 + 512:at0 + 768],
        w_in[:, :, ab0:ab0 + 16],
        jnp.zeros((depth, dm, pad), w_in.dtype),
    ], axis=-1).astype(BF16)
    rw_hi = router_w.astype(BF16)
    rw_lo = (router_w - rw_hi.astype(F32)).astype(BF16)
    rw = jnp.concatenate([
        jnp.concatenate([rw_hi, rw_lo, jnp.zeros((depth, dm, LANES - 2 * N_EXPERTS), BF16)], axis=-1),
        jnp.concatenate([rw_hi, jnp.zeros((depth, dm, LANES - N_EXPERTS), BF16)], axis=-1),
    ], axis=1)
    rb = jnp.concatenate([router_b, jnp.full((depth, LANES - N_EXPERTS), NEG_INF, F32)], axis=-1)
    w1p = _w1_regroup(exp_w1)
    ne = exp_b1.shape[1]
    b1p = jnp.swapaxes(exp_b1.reshape(depth, ne, -1, W1_GROUP // 2, 2), -1, -2).reshape(depth, ne, -1)
    return {
        "w_in": w_p,
        "w_branch": w_branch.astype(BF16),
        "w_out": w_out.astype(BF16),
        "router_w": rw,
        "router_b": rb[:, None, :],
        "sg_b": jnp.repeat(jnp.swapaxes(sg_b, 1, 2), SG_WIDTH // SG_GROUPS, axis=2),
        "w1": w1p,
        "b1": b1p[:, :, None, :],
        "w2": exp_w2.astype(BF16),
        "b2": exp_b2[:, :, None, :],
        "q_norm": jnp.tile(attn_q_norm, (1, 2))[:, None, :],
        "k_norm": jnp.tile(attn_k_norm, (1, 2))[:, None, :],
    }


def kernel(x, c, ctx, c_ctx, norm1, norm2, w_mod, b_mod, w_in, gdn_conv, gdn_a_log, gdn_dt_bias, gdn_out_norm,
           sc_conv, sg_ln_w, sg_ln_b, sg_w, sg_b, attn_q_norm, attn_k_norm, attn_sink, w_branch, w_out,
           router_w, router_b, exp_w1, exp_b1, exp_w2, exp_b2):
    b, n_lat, dm = x.shape
    n_ctx = ctx.shape[1]
    depth = w_in.shape[0]
    tc = b * n_ctx
    t = tc + b * n_lat
    assert dm == D_MODEL and tc % n_lat == 0 and n_lat % 256 == 0 and n_ctx % 256 == 0
    tile_cap = math.gcd(tc, n_lat)

    def mod_row(r0):
        return jnp.where(r0 < tc, 0, 1 + (r0 - tc) // n_lat)

    geo = {
        "b": b, "ctx": n_ctx, "lat": n_lat, "tc": tc, "mod_row": mod_row,
        "tm_proj": min(1024, tile_cap), "tm_merge": min(512, tile_cap), "tm_moe": min(512, tile_cap),
    }

    wts = _prep_weights(w_in, w_branch, w_out, router_w, router_b, sg_b, exp_w1, exp_b1, exp_w2, exp_b2,
                        attn_q_norm, attn_k_norm)
    rope_cos, rope_sin = _rope_tables(n_lat)

    mod_rows = -(-(1 + b) // SUBLANES) * SUBLANES
    c_all = jnp.concatenate([c_ctx[None, :], c, jnp.zeros((mod_rows - 1 - b, dm), F32)], axis=0)
    mod = _modulation(c_all, w_mod, b_mod).reshape(depth, mod_rows, N_MOD, dm)

    xs_flat = jnp.concatenate([ctx.reshape(tc, dm), x.reshape(b * n_lat, dm)], axis=0)
    n_blocks = -(-(t * TOP_K) // MOE_BM) + N_EXPERTS
    n_slots = n_blocks * MOE_BM

    for l in range(depth):
        last = l == depth - 1
        p = _in_proj(xs_flat, mod, norm1[:, None, :], wts["w_in"], l, geo)
        g_lat, g_ctx = _gdn(p, gdn_conv, gdn_a_log, gdn_dt_bias, gdn_out_norm[:, None, :], l, geo)
        a_lat = _attention(p, attn_sink, wts["q_norm"], wts["k_norm"], rope_cos, rope_sin, l, geo, False)
        if last:
            a_ctx = jnp.zeros((tc, ATT_HEADS * HEAD_DIM), BF16)
        else:
            a_ctx = _attention(p, attn_sink, wts["q_norm"], wts["k_norm"], rope_cos, rope_sin, l, geo, True)
        y0 = jnp.concatenate([g_ctx, g_lat], axis=0)
        y3 = jnp.concatenate([a_ctx, a_lat], axis=0)
        layer_w = {
            "sc_conv": sc_conv, "sg_ln_w": sg_ln_w[:, None, :], "sg_ln_b": sg_ln_b[:, None, :], "sg_w": sg_w.astype(BF16),
            "sg_b": wts["sg_b"], "w_branch": wts["w_branch"], "w_out": wts["w_out"],
            "router_w": wts["router_w"], "router_b": wts["router_b"],
        }
        x_mid, h2p, idx, gate, rank, counts = _merge(xs_flat, mod, norm2[:, None, :], y0, y3, p, layer_w, l, geo)
        dest, block_expert = _moe_plan(idx, rank, counts[0, :N_EXPERTS], n_blocks)
        xs_sorted = _dispatch(h2p, dest, n_slots, geo)
        ys_sorted = _experts(xs_sorted, block_expert, wts["w1"], wts["b1"], wts["w2"], wts["b2"], l)
        xs_flat = _combine(x_mid, mod, gate, dest, ys_sorted, l, geo)

    return xs_flat[tc:].reshape(b, n_lat, dm)
```

```python
import functools
import math

import jax
import jax.numpy as jnp
from jax import lax
from jax.experimental import pallas as pl
from jax.experimental.pallas import tpu as pltpu
from jax.experimental.pallas import tpu_sc as plsc

F32 = jnp.float32
BF16 = jnp.bfloat16
I32 = jnp.int32
U32 = jnp.uint32
HIGHEST = lax.Precision.HIGHEST

D_MODEL = 1024
GRID_W = 64
N_MOD = 6
GDN_HEADS = 4
GDN_DK = 128
GDN_DV = 128
GDN_CONV = 5
SC_WIDTH = 512
SC_CONV = 3
SG_CHUNK = 128
SG_GROUPS = 4
SG_WIDTH = 512
ATT_HEADS = 8
ATT_KV_HEADS = 2
HEAD_DIM = 64
WINDOW = 128
ROPE_BASE = 10000.0
N_BRANCH = 4
BRANCH_W = 512
N_EXPERTS = 32
TOP_K = 4
D_EXPERT = 1024
SWIGLU_ALPHA = 1.702
SWIGLU_LIMIT = 7.0
NORM_EPS = 1e-6
LN_EPS = 1e-5
NEG_INF = -1e30

LANES = 128
SUBLANES = 8
VMEM_LIMIT_BYTES = 56 * 2**20

COL_Q = 0
COL_K = 512
COL_V = 1024
COL_Z = 1536
COL_SB = 2048
COL_SC = 2560
COL_SX = 3072
COL_GU = 3584
COL_GV = 4096
COL_AQ = 4608
COL_MG = 5120
COL_AK = 9216
COL_AV = 9344
COL_AB = 9472
PROJ_COLS = 10240
PROJ_TN = 2048
PROJ_CHUNK = 512

GDN_CHUNK = 128
GDN_SOLVE_BLOCK = 64
GDN_PREP_UNROLL = 4
MOE_BM = 512
W1_GROUP = 256
SC_ROW = 256
SC_WINDOW = 128


def _cparams(sem):
    return pltpu.CompilerParams(dimension_semantics=sem, vmem_limit_bytes=VMEM_LIMIT_BYTES)


def _silu(x):
    return x * jax.nn.sigmoid(x)


def _dot(a, b, precision=None):
    return jnp.dot(a, b, preferred_element_type=F32, precision=precision)


def _dot_nt(a, b):
    return lax.dot_general(a, b, (((1,), (1,)), ((), ())), preferred_element_type=F32)


def _dot_tn(a, b):
    return lax.dot_general(a, b, (((0,), (0,)), ((), ())), preferred_element_type=F32)


def _aligned(start, multiple):
    return start if isinstance(start, int) else pl.multiple_of(start, multiple)


def _pack_bf16_pair(lo, hi):
    lo_b = pltpu.bitcast(lo.astype(BF16).astype(F32), U32)
    hi_b = pltpu.bitcast(hi.astype(BF16).astype(F32), U32)
    return (hi_b & jnp.uint32(0xFFFF0000)) | (lo_b >> 16)


def _unpack_bf16_pair(u):
    lo = pltpu.bitcast(u << 16, F32)
    hi = pltpu.bitcast(u & jnp.uint32(0xFFFF0000), F32)
    return lo, hi


def _mod_kernel(c_ref, w_ref, b_ref, o_ref):
    s = _silu(c_ref[...])
    o_ref[...] = _dot(s, w_ref[...], precision=HIGHEST) + b_ref[...]


def _modulation(c_all, w_mod, b_mod):
    depth, dm, nm = w_mod.shape
    rows = c_all.shape[0]
    tn = 1536
    return pl.pallas_call(
        _mod_kernel,
        out_shape=jax.ShapeDtypeStruct((depth, rows, nm), F32),
        grid=(depth, nm // tn),
        in_specs=[
            pl.BlockSpec((rows, dm), lambda l, j: (0, 0)),
            pl.BlockSpec((None, dm, tn), lambda l, j: (l, 0, j)),
            pl.BlockSpec((None, 1, tn), lambda l, j: (l, 0, j)),
        ],
        out_specs=pl.BlockSpec((None, rows, tn), lambda l, j: (l, 0, j)),
        compiler_params=_cparams(("arbitrary", "arbitrary")),
        name="modulation",
    )(c_all, w_mod, b_mod.reshape(depth, 1, nm))


def _in_proj_kernel(x_ref, mod_ref, nw_ref, w_ref, o_ref):
    x = x_ref[...]
    ms = jnp.mean(x * x, axis=-1, keepdims=True)
    y = x * lax.rsqrt(ms + NORM_EPS) * nw_ref[...]
    h = (y * (1.0 + mod_ref[1:2, :]) + mod_ref[0:1, :]).astype(BF16)
    for c in range(PROJ_TN // PROJ_CHUNK):
        sl = slice(c * PROJ_CHUNK, (c + 1) * PROJ_CHUNK)
        o_ref[:, sl] = _dot(h, w_ref[:, sl]).astype(BF16)


def _in_proj(x, mod, norm_w, w_p, l, geo):
    t, dm = x.shape
    tm = geo["tm_proj"]
    mod_row = geo["mod_row"]
    return pl.pallas_call(
        _in_proj_kernel,
        out_shape=jax.ShapeDtypeStruct((t, PROJ_COLS), BF16),
        grid=(PROJ_COLS // PROJ_TN, t // tm),
        in_specs=[
            pl.BlockSpec((tm, dm), lambda j, i: (i, 0)),
            pl.BlockSpec((None, None, N_MOD, dm), lambda j, i: (l, mod_row(i * tm), 0, 0)),
            pl.BlockSpec((None, 1, dm), lambda j, i: (l, 0, 0)),
            pl.BlockSpec((None, dm, PROJ_TN), lambda j, i: (l, 0, j)),
        ],
        out_specs=pl.BlockSpec((tm, PROJ_TN), lambda j, i: (i, j)),
        compiler_params=_cparams(("arbitrary", "arbitrary")),
        name="in_proj",
    )(x, mod, norm_w, w_p)


def _softplus(x):
    return jnp.maximum(x, 0.0) + jnp.log1p(jnp.exp(-jnp.abs(x)))


def _gdn_kernel(alog_ref, dtb_ref,
                ql_ref, kl_ref, vl_ref, zl_ref, abl_ref,
                qc_ref, kc_ref, vc_ref, zc_ref, abc_ref,
                cwq_ref, cwk_ref, cwv_ref, onorm_ref,
                yl_ref, yc_ref,
                xpad, qs, ks, vs, gsc, bsc,
                u_sc, w16, qk16, qd16, kdk16, glast, oacc,
                *, n_ctx, n_lat):
    cs = GDN_CHUNK
    hd = GDN_DK
    n_all = n_ctx + n_lat
    n_chunks = n_all // cs
    n_ctx_chunks = n_ctx // cs
    h = pl.program_id(1)
    pad = SUBLANES
    lat_off = n_ctx + 3 * pad

    zeros_pad = jnp.zeros((pad, hd), F32)
    xpad[0:pad, :] = zeros_pad
    xpad[pad + n_ctx:pad + n_ctx + 2 * pad, :] = jnp.zeros((2 * pad, hd), F32)
    xpad[lat_off + n_lat:lat_off + n_lat + pad, :] = zeros_pad

    half = GDN_CONV // 2
    tile = 256

    def conv_into(src_c, src_l, cw_ref, dst, mode):
        xpad[pad:pad + n_ctx, :] = src_c[...].astype(F32)
        xpad[lat_off:lat_off + n_lat, :] = src_l[...].astype(F32)
        w = cw_ref[...]
        for seg_off, dst_off, seg_len in ((pad, 0, n_ctx), (lat_off, n_ctx, n_lat)):
            for r0 in range(0, seg_len, tile):
                acc = jnp.zeros((tile, hd), F32)
                for j in range(GDN_CONV):
                    s = seg_off + r0 + j - half
                    acc = acc + xpad[s:s + tile, :] * w[j:j + 1, :]
                y = _silu(acc)
                if mode != "v":
                    y = y * lax.rsqrt(jnp.sum(y * y, axis=-1, keepdims=True) + NORM_EPS)
                if mode == "q":
                    y = y * (hd ** -0.5)
                dst[dst_off + r0:dst_off + r0 + tile, :] = y

    conv_into(qc_ref, ql_ref, cwq_ref, qs, "q")
    conv_into(kc_ref, kl_ref, cwk_ref, ks, "k")
    conv_into(vc_ref, vl_ref, cwv_ref, vs, "v")

    sel_r = lax.broadcasted_iota(I32, (LANES, LANES), 0)
    pos = lax.broadcasted_iota(I32, (tile, LANES), 0) % cs
    for d in range(2):
        col = d * GDN_HEADS + h
        sel_a = (sel_r == col).astype(BF16)
        sel_b = (sel_r == (2 * GDN_HEADS + col)).astype(BF16)
        neg_a = -jnp.exp(jnp.full((1, LANES), alog_ref[d, h], F32))
        dtb = dtb_ref[d, h]
        for src, r_off, r_len in ((abc_ref, 0, n_ctx), (abl_ref, n_ctx, n_lat)):
            for r0 in range(0, r_len, tile):
                ab = src[r0:r0 + tile, :]
                a_b = _dot(ab, sel_a)
                b_b = _dot(ab, sel_b)
                gc = neg_a * _softplus(a_b + dtb)
                s = 1
                while s < cs:
                    if d == 0:
                        gc = gc + jnp.where(pos >= s, pltpu.roll(gc, s, 0), 0.0)
                    else:
                        gc = gc + jnp.where(pos < cs - s, pltpu.roll(gc, tile - s, 0), 0.0)
                    s *= 2
                gsc[d, r_off + r0:r_off + r0 + tile, :] = gc
                bsc[d, r_off + r0:r_off + r0 + tile, :] = jax.nn.sigmoid(b_b)

    ri = lax.broadcasted_iota(I32, (cs, cs), 0)
    ci = lax.broadcasted_iota(I32, (cs, cs), 1)
    low = (ri >= ci)
    up = (ri <= ci)
    eye = (ri == ci).astype(F32)
    offdiag = (ri != ci).astype(F32)
    assert cs in (GDN_SOLVE_BLOCK, 2 * GDN_SOLVE_BLOCK)
    same_blk = ((ri // GDN_SOLVE_BLOCK) == (ci // GDN_SOLVE_BLOCK)).astype(F32)
    n_levels = int(math.log2(min(cs, GDN_SOLVE_BLOCK))) - 1

    def prep_group(first_chunk, count):
        chunk_ids = [first_chunk + uu for uu in range(count)]
        rows = [pl.ds(_aligned(c * cs, cs), cs) for c in chunk_ids]
        q = [qs[r, :] for r in rows]
        k = [ks[r, :] for r in rows]
        v = [vs[r, :] for r in rows]
        kb = [x.astype(BF16) for x in k]
        kkt = [_dot_nt(x, x) for x in kb]
        qkt = [_dot_nt(a.astype(BF16), b) for a, b in zip(q, kb)]
        probs = [(ui, d) for ui in range(count) for d in range(2)]
        gc_col, beta_b, dmat, t, p, n_off = {}, {}, {}, {}, {}, {}
        for ui, d in probs:
            gc = gsc[d, rows[ui], :]
            bt = bsc[d, rows[ui], :]
            gsq = gc[:, :cs]
            dm = jnp.where(low if d == 0 else up, jnp.exp(gsq - gsq.T), 0.0)
            n = -(kkt[ui] * bt[:, :cs] * dm * offdiag)
            gc_col[ui, d], beta_b[ui, d], dmat[ui, d] = gc, bt, dm
            n_diag = n * same_blk
            t[ui, d] = eye + n_diag
            p[ui, d] = n_diag.astype(BF16)
            n_off[ui, d] = (n - n_diag).astype(BF16)
        p = {key: _dot(x, x).astype(BF16) for key, x in p.items()}
        for lvl in range(n_levels):
            t_next = {key: t[key] + _dot(p[key], t[key].astype(BF16)) for key in probs}
            if lvl + 1 < n_levels:
                p = {key: _dot(x, x).astype(BF16) for key, x in p.items()}
            t = t_next
        tb = {key: x.astype(BF16) for key, x in t.items()}
        if cs > GDN_SOLVE_BLOCK:
            x_off = {key: _dot(tb[key], n_off[key]).astype(BF16) for key in probs}
            t = {key: t[key] + _dot(x_off[key], tb[key]) for key in probs}
            tb = {key: x.astype(BF16) for key, x in t.items()}
        eg = {key: jnp.exp(x) for key, x in gc_col.items()}
        u = {(ui, d): _dot(tb[ui, d], (v[ui] * beta_b[ui, d]).astype(BF16)) for ui, d in probs}
        w = {(ui, d): _dot(tb[ui, d], (k[ui] * beta_b[ui, d] * eg[ui, d]).astype(BF16)) for ui, d in probs}
        for ui, d in probs:
            r = rows[ui]
            last = cs - 1 if d == 0 else 0
            gl = gc_col[ui, d][last:last + 1, :]
            u_sc[d, r, :] = u[ui, d]
            w16[d, r, :] = w[ui, d].astype(BF16)
            qk16[d, r, :] = (qkt[ui] * dmat[ui, d]).astype(BF16)
            qd16[d, r, :] = (q[ui] * eg[ui, d]).astype(BF16)
            kdk16[d, r, :] = (k[ui] * jnp.exp(gl - gc_col[ui, d])).astype(BF16)
            g0 = _aligned(chunk_ids[ui] * SUBLANES, SUBLANES)
            glast[d, pl.ds(g0, SUBLANES), :] = jnp.broadcast_to(jnp.exp(gl), (SUBLANES, LANES))

    def prep(gi, carry):
        prep_group(gi * GDN_PREP_UNROLL, GDN_PREP_UNROLL)
        return carry

    n_groups = n_chunks // GDN_PREP_UNROLL
    lax.fori_loop(0, n_groups, prep, 0)
    if n_chunks % GDN_PREP_UNROLL:
        prep_group(n_groups * GDN_PREP_UNROLL, n_chunks % GDN_PREP_UNROLL)

    oacc[...] = jnp.zeros_like(oacc)

    def step(i, carry):
        c_b = jnp.where(i < n_ctx_chunks, n_ctx_chunks - 1 - i, n_chunks + n_ctx_chunks - 1 - i)
        cid = (i, c_b)
        rows = [pl.ds(pl.multiple_of(c * cs, cs), cs) for c in cid]
        sb = [s.astype(BF16) for s in carry]
        ws = [_dot(w16[d, rows[d], :], sb[d]) for d in range(2)]
        qs_s = [_dot(qd16[d, rows[d], :], sb[d]) for d in range(2)]
        vnb = [(u_sc[d, rows[d], :] - ws[d]).astype(BF16) for d in range(2)]
        o = [qs_s[d] + _dot(qk16[d, rows[d], :], vnb[d]) for d in range(2)]
        kv = [_dot_tn(kdk16[d, rows[d], :], vnb[d]) for d in range(2)]
        s_new = []
        for d in range(2):
            g0 = pl.multiple_of(cid[d] * SUBLANES, SUBLANES)
            s_new.append(carry[d] * glast[d, pl.ds(g0, 1), :] + kv[d])
            oacc[rows[d], :] = oacc[rows[d], :] + o[d]
        return tuple(s_new)

    s0 = jnp.zeros((hd, GDN_DV), F32)
    lax.fori_loop(0, n_chunks, step, (s0, s0))

    onw = onorm_ref[...]

    def out_gate(o, z):
        y = o * lax.rsqrt(jnp.mean(o * o, axis=-1, keepdims=True) + NORM_EPS) * onw
        return (y * _silu(z.astype(F32))).astype(BF16)

    yc_ref[...] = out_gate(oacc[0:n_ctx, :], zc_ref[...])
    for r0 in range(0, n_lat, tile):
        yl_ref[r0:r0 + tile, :] = out_gate(oacc[n_ctx + r0:n_ctx + r0 + tile, :], zl_ref[r0:r0 + tile, :])


def _gdn(p, conv_w, a_log, dt_bias, out_norm, l, geo):
    b, n_ctx, n_lat = geo["b"], geo["ctx"], geo["lat"]
    n_all = n_ctx + n_lat
    lat0 = geo["tc"] // n_lat
    hd = GDN_DK

    def lat_spec(col):
        return pl.BlockSpec((n_lat, hd), lambda bi, h: (lat0 + bi, col // hd + h))

    def ctx_spec(col):
        return pl.BlockSpec((n_ctx, hd), lambda bi, h: (bi, col // hd + h))

    def fixed_lat(col):
        return pl.BlockSpec((n_lat, hd), lambda bi, h: (lat0 + bi, col // hd))

    def fixed_ctx(col):
        return pl.BlockSpec((n_ctx, hd), lambda bi, h: (bi, col // hd))

    def cw_spec(col):
        return pl.BlockSpec((None, GDN_CONV, hd), lambda bi, h: (l, 0, col // hd + h))

    smem = pl.BlockSpec(memory_space=pltpu.SMEM)
    kern = functools.partial(_gdn_kernel, n_ctx=n_ctx, n_lat=n_lat)
    n_chunks = n_all // GDN_CHUNK
    assert n_ctx % GDN_CHUNK == 0 and n_lat % GDN_CHUNK == 0
    scratch = [
        pltpu.VMEM((n_all + 4 * SUBLANES, hd), F32),
        pltpu.VMEM((n_all, hd), F32),
        pltpu.VMEM((n_all, hd), F32),
        pltpu.VMEM((n_all, hd), F32),
        pltpu.VMEM((2, n_all, LANES), F32),
        pltpu.VMEM((2, n_all, LANES), F32),
        pltpu.VMEM((2, n_all, GDN_DV), F32),
        pltpu.VMEM((2, n_all, hd), BF16),
        pltpu.VMEM((2, n_all, GDN_CHUNK), BF16),
        pltpu.VMEM((2, n_all, hd), BF16),
        pltpu.VMEM((2, n_all, hd), BF16),
        pltpu.VMEM((2, n_chunks * SUBLANES, LANES), F32),
        pltpu.VMEM((n_all, GDN_DV), F32),
    ]
    y_lat, y_ctx = pl.pallas_call(
        kern,
        out_shape=(jax.ShapeDtypeStruct((b * n_lat, GDN_HEADS * GDN_DV), BF16),
                   jax.ShapeDtypeStruct((b * n_ctx, GDN_HEADS * GDN_DV), BF16)),
        grid=(b, GDN_HEADS),
        in_specs=[smem, smem,
                  lat_spec(COL_Q), lat_spec(COL_K), lat_spec(COL_V), lat_spec(COL_Z), fixed_lat(COL_AB),
                  ctx_spec(COL_Q), ctx_spec(COL_K), ctx_spec(COL_V), ctx_spec(COL_Z), fixed_ctx(COL_AB),
                  cw_spec(0), cw_spec(GDN_HEADS * hd), cw_spec(2 * GDN_HEADS * hd),
                  pl.BlockSpec((None, 1, GDN_DV), lambda bi, h: (l, 0, 0))],
        out_specs=(pl.BlockSpec((n_lat, GDN_DV), lambda bi, h: (bi, h)),
                   pl.BlockSpec((n_ctx, GDN_DV), lambda bi, h: (bi, h))),
        scratch_shapes=scratch,
        compiler_params=_cparams(("arbitrary", "arbitrary")),
        name="gdn",
    )(a_log[l], dt_bias[l], p, p, p, p, p, p, p, p, p, p, conv_w, conv_w, conv_w, out_norm)
    return y_lat, y_ctx


def _head_rms(x, bd, w):
    ss = _dot(x * x, bd, precision=HIGHEST) * (1.0 / HEAD_DIM)
    return x * lax.rsqrt(ss + NORM_EPS) * w


def _rope(x, cos, sin_signed, lane):
    nf = HEAD_DIM // 4
    partner = jnp.where((lane % (2 * nf)) < nf, pltpu.roll(x, LANES - nf, 1), pltpu.roll(x, nf, 1))
    return x * cos + partner * sin_signed


def _attn_kernel(sink_ref, q_ref, kq_ref, vq_ref, kc_ref, vc_ref,
                 cosq_ref, sinq_ref, cosk_ref, sink_k_ref, qw_ref, kw_ref,
                 o_ref, kp, vp, kcp, vcp, *, n_keys, n_ctx, local):
    n = pl.program_id(1)
    w = WINDOW
    lane = lax.broadcasted_iota(I32, (1, LANES), 1)
    lo_mask = (lane < HEAD_DIM).astype(F32)
    hi_mask = 1.0 - lo_mask
    bd_r = lax.broadcasted_iota(I32, (LANES, LANES), 0) // HEAD_DIM
    bd_c = lax.broadcasted_iota(I32, (LANES, LANES), 1) // HEAD_DIM
    bd = (bd_r == bd_c).astype(F32)
    kw = kw_ref[...]

    def store_variants(dst, row0, x):
        xr = pltpu.roll(x, HEAD_DIM, 1)
        rows = x.shape[0]
        dst[0, row0:row0 + rows, :] = (x * lo_mask).astype(BF16)
        dst[1, row0:row0 + rows, :] = (xr * hi_mask).astype(BF16)
        dst[2, row0:row0 + rows, :] = (xr * lo_mask).astype(BF16)
        dst[3, row0:row0 + rows, :] = (x * hi_mask).astype(BF16)

    @pl.when(n == 0)
    def _prepare_keys():
        tile = 256
        for r0 in range(0, n_ctx, tile):
            rr = min(tile, n_ctx - r0)
            kc = _head_rms(kc_ref[r0:r0 + rr, :].astype(F32), bd, kw)
            store_variants(kcp, r0, kc)
            store_variants(vcp, r0, vc_ref[r0:r0 + rr, :].astype(F32))
        if local:
            zero = jnp.zeros((w, LANES), BF16)
            for t in range(4):
                kp[t, 0:w, :] = zero
                kp[t, w + n_keys:2 * w + n_keys, :] = zero
                vp[t, 0:w, :] = zero
                vp[t, w + n_keys:2 * w + n_keys, :] = zero
            for r0 in range(0, n_keys, tile):
                k = _head_rms(kq_ref[r0:r0 + tile, :].astype(F32), bd, kw)
                k = _rope(k, cosk_ref[r0:r0 + tile, :], sink_k_ref[r0:r0 + tile, :], lane)
                store_variants(kp, w + r0, k)
                store_variants(vp, w + r0, vq_ref[r0:r0 + tile, :].astype(F32))

    qw = qw_ref[...]
    q2 = []
    for g in range(ATT_HEADS // 2):
        qg = _head_rms(q_ref[:, g * LANES:(g + 1) * LANES].astype(F32), bd, qw)
        if local:
            qg = _rope(qg, cosq_ref[...], sinq_ref[...], lane)
        q2.append((qg * (HEAD_DIM ** -0.5)).astype(BF16))

    rows2 = 2 * w
    row_i = lax.broadcasted_iota(I32, (rows2, 1), 0)
    first = row_i < w
    if local:
        n_blk = n_keys // w
        qq = lax.broadcasted_iota(I32, (rows2, 3 * w), 0) % w
        kcol = lax.broadcasted_iota(I32, (rows2, 3 * w), 1)
        kk = kcol % w
        blk = kcol // w
        valid = ((blk == 1)
                 | ((blk == 0) & (kk >= qq) & (n >= 1))
                 | ((blk == 2) & (kk <= qq) & (n < n_blk - 1)))
        k0 = pl.multiple_of(n * w, w)

    for j in range(ATT_KV_HEADS):
        lhs = jnp.concatenate([q2[2 * j], q2[2 * j + 1]], axis=0)
        pair_out = [None, None]
        for t in range(2):
            var = 2 * j + t
            sink = jnp.where(first, sink_ref[4 * j + t], sink_ref[4 * j + 2 + t])
            s_ctx = _dot_nt(lhs, kcp[var])
            m = jnp.maximum(jnp.max(s_ctx, axis=-1, keepdims=True), sink)
            if local:
                s_loc = _dot_nt(lhs, kp[var, pl.ds(k0, 3 * w), :])
                s_loc = jnp.where(valid, s_loc, NEG_INF)
                m = jnp.maximum(m, jnp.max(s_loc, axis=-1, keepdims=True))
            e_ctx = jnp.exp(s_ctx - m)
            den = jnp.sum(e_ctx, axis=-1, keepdims=True) + jnp.exp(sink - m)
            pv = _dot(e_ctx.astype(BF16), vcp[var])
            if local:
                e_loc = jnp.exp(s_loc - m)
                den = den + jnp.sum(e_loc, axis=-1, keepdims=True)
                pv = pv + _dot(e_loc.astype(BF16), vp[var, pl.ds(k0, 3 * w), :])
            o = pv / den
            for half_i in range(2):
                part = o[half_i * w:(half_i + 1) * w, :]
                pair_out[half_i] = part if pair_out[half_i] is None else pair_out[half_i] + part
        for half_i in range(2):
            c0 = (2 * j + half_i) * LANES
            o_ref[:, c0:c0 + LANES] = pair_out[half_i].astype(BF16)


def _attention(p, sink, q_norm_w, k_norm_w, rope_cos, rope_sin, l, geo, ctx_queries):
    b, n_ctx, n_lat = geo["b"], geo["ctx"], geo["lat"]
    w = WINDOW
    lat0 = geo["tc"] // n_lat
    local = not ctx_queries
    n_q = n_ctx if ctx_queries else n_lat
    nqb = n_q // w
    q_row0 = 0 if ctx_queries else geo["tc"] // w
    kern = functools.partial(_attn_kernel, n_keys=n_lat, n_ctx=n_ctx, local=local)
    in_specs = [
        pl.BlockSpec(memory_space=pltpu.SMEM),
        pl.BlockSpec((w, ATT_HEADS * HEAD_DIM), lambda bi, n: (q_row0 + bi * nqb + n, COL_AQ // 512)),
        pl.BlockSpec((n_lat, LANES), lambda bi, n: (lat0 + bi, COL_AK // LANES)),
        pl.BlockSpec((n_lat, LANES), lambda bi, n: (lat0 + bi, COL_AV // LANES)),
        pl.BlockSpec((n_ctx, LANES), lambda bi, n: (bi, COL_AK // LANES)),
        pl.BlockSpec((n_ctx, LANES), lambda bi, n: (bi, COL_AV // LANES)),
        pl.BlockSpec((w, LANES), lambda bi, n: (n if local else 0, 0)),
        pl.BlockSpec((w, LANES), lambda bi, n: (n if local else 0, 0)),
        pl.BlockSpec((n_lat, LANES), lambda bi, n: (0, 0)),
        pl.BlockSpec((n_lat, LANES), lambda bi, n: (0, 0)),
        pl.BlockSpec((None, 1, LANES), lambda bi, n: (l, 0, 0)),
        pl.BlockSpec((None, 1, LANES), lambda bi, n: (l, 0, 0)),
    ]
    scratch = [
        pltpu.VMEM((4, n_lat + 2 * w, LANES), BF16),
        pltpu.VMEM((4, n_lat + 2 * w, LANES), BF16),
        pltpu.VMEM((4, n_ctx, LANES), BF16),
        pltpu.VMEM((4, n_ctx, LANES), BF16),
    ]
    return pl.pallas_call(
        kern,
        out_shape=jax.ShapeDtypeStruct((b * n_q, ATT_HEADS * HEAD_DIM), BF16),
        grid=(b, nqb),
        in_specs=in_specs,
        out_specs=pl.BlockSpec((w, ATT_HEADS * HEAD_DIM), lambda bi, n: (bi * nqb + n, 0)),
        scratch_shapes=scratch,
        compiler_params=_cparams(("arbitrary", "arbitrary")),
        name="attn_ctx" if ctx_queries else "attn_lat",
    )(sink[l], p, p, p, p, p, rope_cos, rope_sin, rope_cos, rope_sin, q_norm_w, k_norm_w)


def _merge_kernel(x_ref, mod_ref, n2_ref, y0_ref, y3_ref,
                  sb_ref, sc_ref, sx_ref, scp_ref, sxp_ref, scn_ref, sxn_ref,
                  gu_ref, gv_ref, mg0_ref, mg1_ref, mg2_ref, mg3_ref,
                  scw_ref, lnw_ref, lnb_ref, sgw_ref, sgb_ref, wb_ref, wo_ref, rw_ref, rb_ref,
                  xo_ref, h2_ref, idx_ref, gate_ref, rank_ref, cnt_ref,
                  carry, *, tm, tc, n_ctx, n_lat):
    i = pl.program_id(0)
    r0 = i * tm

    row = lax.broadcasted_iota(I32, (tm, 1), 0)
    g_row = r0 + row
    in_ctx = g_row < tc
    seg_pos = jnp.where(in_ctx, g_row % n_ctx, (g_row - tc) % n_lat)
    seg_len = jnp.where(in_ctx, n_ctx, n_lat)
    cx = sc_ref[...].astype(F32) * sx_ref[...].astype(F32)
    cx_prev_halo = scp_ref[SUBLANES - 1:SUBLANES, :].astype(F32) * sxp_ref[SUBLANES - 1:SUBLANES, :].astype(F32)
    cx_next_halo = scn_ref[0:1, :].astype(F32) * sxn_ref[0:1, :].astype(F32)
    prev = jnp.where(row == 0, cx_prev_halo, pltpu.roll(cx, 1, 0))
    prev = jnp.where(seg_pos == 0, 0.0, prev)
    nxt = jnp.where(row == tm - 1, cx_next_halo, pltpu.roll(cx, tm - 1, 0))
    nxt = jnp.where(seg_pos == seg_len - 1, 0.0, nxt)
    scw = scw_ref[...]
    y1 = sb_ref[...].astype(F32) * (prev * scw[0:1, :] + cx * scw[1:2, :] + nxt * scw[2:3, :])

    inv_sqrt2 = 1.0 / math.sqrt(2.0)

    def gelu(t):
        return 0.5 * t * (1.0 + lax.erf(t * inv_sqrt2))

    u = gelu(gu_ref[...].astype(F32))
    v = gelu(gv_ref[...].astype(F32))
    mu = jnp.mean(v, axis=-1, keepdims=True)
    vc = v - mu
    v = vc * lax.rsqrt(jnp.mean(vc * vc, axis=-1, keepdims=True) + LN_EPS) * lnw_ref[...] + lnb_ref[...]
    vb = v.astype(BF16)
    gw = SG_WIDTH // SG_GROUPS
    chunks = []
    for c in range(tm // SG_CHUNK):
        groups = []
        for g in range(SG_GROUPS):
            groups.append(_dot(sgw_ref[g], vb[c * SG_CHUNK:(c + 1) * SG_CHUNK, g * gw:(g + 1) * gw]))
        chunks.append(jnp.concatenate(groups, axis=1) + sgb_ref[...])
    y2 = u * jnp.concatenate(chunks, axis=0)

    ys = (y0_ref[...], y1.astype(BF16), y2.astype(BF16), y3_ref[...])
    gates = (mg0_ref, mg1_ref, mg2_ref, mg3_ref)
    m = None
    for br in range(N_BRANCH):
        term = jax.nn.sigmoid(gates[br][...].astype(F32)) * _dot(ys[br], wb_ref[br])
        m = term if m is None else m + term
    y = _dot(m.astype(BF16), wo_ref[...])
    x_new = x_ref[...] + mod_ref[2:3, :] * y
    xo_ref[...] = x_new

    ms = jnp.mean(x_new * x_new, axis=-1, keepdims=True)
    h2 = x_new * lax.rsqrt(ms + NORM_EPS) * n2_ref[...] * (1.0 + mod_ref[4:5, :]) + mod_ref[3:4, :]
    half = D_MODEL // 2
    h2_ref[...] = _pack_bf16_pair(h2[:, :half], h2[:, half:])

    h2_hi = h2.astype(BF16)
    h2_lo = (h2 - h2_hi.astype(F32)).astype(BF16)
    prod = _dot(jnp.concatenate([h2_hi, h2_lo], axis=1), rw_ref[...])
    lane = lax.broadcasted_iota(I32, (tm, LANES), 1)
    logits = prod + pltpu.roll(prod, LANES - N_EXPERTS, 1)
    logits = jnp.where(lane < N_EXPERTS, logits, NEG_INF) + rb_ref[...]
    lane_f = lane.astype(F32)
    work = logits
    topv = jnp.full((tm, LANES), NEG_INF, F32)
    topi = jnp.zeros((tm, LANES), I32)
    onehot = jnp.zeros((tm, LANES), F32)
    firsts = []
    for k in range(TOP_K):
        mx = jnp.max(work, axis=-1, keepdims=True)
        first_f = jnp.min(jnp.where(work == mx, lane_f, float(LANES)), axis=-1, keepdims=True)
        first = first_f.astype(I32)
        hit = lane == first
        topv = jnp.where(lane == k, mx, topv)
        topi = jnp.where(lane == k, first, topi)
        onehot = jnp.where(hit, 1.0, onehot)
        work = jnp.where(hit, -jnp.inf, work)
        firsts.append(first)
    e = jnp.where(lane < TOP_K, jnp.exp(topv - jnp.max(topv, axis=-1, keepdims=True)), 0.0)
    gate = e / jnp.sum(e, axis=-1, keepdims=True)

    @pl.when(i == 0)
    def _init():
        carry[...] = jnp.zeros_like(carry)

    tri = (lax.broadcasted_iota(I32, (tm, tm), 0) > lax.broadcasted_iota(I32, (tm, tm), 1)).astype(BF16)
    before = carry[0:1, :] + _dot(tri, onehot.astype(BF16))
    rank = jnp.zeros((tm, LANES), F32)
    for k in range(TOP_K):
        r_k = jnp.sum(jnp.where(lane == firsts[k], before, 0.0), axis=-1, keepdims=True)
        rank = jnp.where(lane == k, r_k, rank)
    new_carry = carry[0:1, :] + jnp.sum(onehot, axis=0, keepdims=True)
    carry[...] = jnp.broadcast_to(new_carry, carry.shape)
    cnt_ref[...] = jnp.broadcast_to(new_carry, cnt_ref.shape).astype(I32)
    idx_ref[...] = topi[:, :SUBLANES]
    gate_ref[...] = gate[:, :SUBLANES]
    rank_ref[...] = rank[:, :SUBLANES].astype(I32)


def _merge(x, mod, norm2, y0, y3, p, w, l, geo):
    t, dm = x.shape
    tm = geo["tm_merge"]
    mod_row = geo["mod_row"]
    kern = functools.partial(_merge_kernel, tm=tm, tc=geo["tc"], n_ctx=geo["ctx"], n_lat=geo["lat"])
    halo = tm // SUBLANES
    n_halo = t // SUBLANES

    def col(c, width=512):
        return pl.BlockSpec((tm, width), lambda i: (i, c // width))

    def prev_halo(c):
        return pl.BlockSpec((SUBLANES, 512), lambda i: (jnp.maximum(i * halo - 1, 0), c // 512))

    def next_halo(c):
        return pl.BlockSpec((SUBLANES, 512), lambda i: (jnp.minimum((i + 1) * halo, n_halo - 1), c // 512))

    def layer(shape):
        nd = len(shape)
        return pl.BlockSpec((None,) + shape, lambda i: (l,) + (0,) * nd)

    in_specs = [
        pl.BlockSpec((tm, dm), lambda i: (i, 0)),
        pl.BlockSpec((None, None, N_MOD, dm), lambda i: (l, mod_row(i * tm), 0, 0)),
        layer((1, dm)),
        pl.BlockSpec((tm, 512), lambda i: (i, 0)),
        pl.BlockSpec((tm, 512), lambda i: (i, 0)),
        col(COL_SB), col(COL_SC), col(COL_SX),
        prev_halo(COL_SC), prev_halo(COL_SX), next_halo(COL_SC), next_halo(COL_SX),
        col(COL_GU), col(COL_GV),
        col(COL_MG, 1024), col(COL_MG + 1024, 1024), col(COL_MG + 2048, 1024), col(COL_MG + 3072, 1024),
        layer((SC_CONV, SC_WIDTH)), layer((1, SG_WIDTH)), layer((1, SG_WIDTH)),
        layer((SG_GROUPS, SG_CHUNK, SG_CHUNK)), layer((SG_CHUNK, SG_WIDTH)),
        layer((N_BRANCH, BRANCH_W, dm)), layer((dm, dm)), layer((2 * dm, LANES)), layer((1, LANES)),
    ]
    out_shape = (
        jax.ShapeDtypeStruct((t, dm), F32),
        jax.ShapeDtypeStruct((t, dm // 2), U32),
        jax.ShapeDtypeStruct((t, SUBLANES), I32),
        jax.ShapeDtypeStruct((t, SUBLANES), F32),
        jax.ShapeDtypeStruct((t, SUBLANES), I32),
        jax.ShapeDtypeStruct((SUBLANES, LANES), I32),
    )
    out_specs = (
        pl.BlockSpec((tm, dm), lambda i: (i, 0)),
        pl.BlockSpec((tm, dm // 2), lambda i: (i, 0)),
        pl.BlockSpec((tm, SUBLANES), lambda i: (i, 0)),
        pl.BlockSpec((tm, SUBLANES), lambda i: (i, 0)),
        pl.BlockSpec((tm, SUBLANES), lambda i: (i, 0)),
        pl.BlockSpec((SUBLANES, LANES), lambda i: (0, 0)),
    )
    return pl.pallas_call(
        kern,
        out_shape=out_shape,
        grid=(t // tm,),
        in_specs=in_specs,
        out_specs=out_specs,
        scratch_shapes=[pltpu.VMEM((SUBLANES, LANES), F32)],
        compiler_params=_cparams(("arbitrary",)),
        name="merge",
    )(x, mod, norm2, y0, y3, p, p, p, p, p, p, p, p, p, p, p, p, p,
      w["sc_conv"], w["sg_ln_w"], w["sg_ln_b"], w["sg_w"], w["sg_b"], w["w_branch"], w["w_out"],
      w["router_w"], w["router_b"])


def _sc_mesh():
    return plsc.VectorSubcoreMesh(core_axis_name="core", subcore_axis_name="subcore")


def _sc_scatter_rows(src, idx, n_out):
    n, width = src.shape

    @pl.kernel(out_type=jax.ShapeDtypeStruct((n_out, width), U32), mesh=_sc_mesh(), scratch_types=[])
    def scatter(x_hbm, i_hbm, o_hbm):
        def body(x_vmem, *i_vmem):
            for iv in i_vmem:
                pltpu.sync_copy(x_vmem, o_hbm.at[iv.at[0]])

        pltpu.emit_pipeline(
            body,
            grid=(n // SC_WINDOW,),
            in_specs=[pl.BlockSpec((SC_WINDOW, width), lambda i: (i, 0))]
            + [pl.BlockSpec((1, SC_WINDOW), functools.partial(lambda i, k: (k, i), k=k)) for k in range(TOP_K)],
            out_specs=[],
            core_axis_name=("core", "subcore"),
            dimension_semantics=(pltpu.PARALLEL,),
        )(x_hbm, *([i_hbm] * TOP_K))

    return scatter(src, idx)


def _sc_gather_rows(src, idx):
    n = idx.shape[0]
    width = src.shape[1]

    @pl.kernel(out_type=jax.ShapeDtypeStruct((n, width), U32), mesh=_sc_mesh(), scratch_types=[])
    def gather(x_hbm, i_hbm, o_hbm):
        def body(i_vmem, o_vmem):
            pltpu.sync_copy(x_hbm.at[i_vmem.at[0]], o_vmem)

        pltpu.emit_pipeline(
            body,
            grid=(n // SC_WINDOW,),
            in_specs=[pl.BlockSpec((1, SC_WINDOW), lambda i: (0, i))],
            out_specs=[pl.BlockSpec((SC_WINDOW, width), lambda i: (i, 0))],
            core_axis_name=("core", "subcore"),
            dimension_semantics=(pltpu.PARALLEL,),
        )(i_hbm, o_hbm)

    return gather(src, idx.reshape(1, n))


def _expert_kernel(be_ref, bv_ref, xs_ref, w1_ref, b1_ref, w2_ref, b2_ref, ys_ref):
    del be_ref
    valid = lax.broadcasted_iota(I32, (MOE_BM, 1), 0) < bv_ref[pl.program_id(0)]
    lo, hi = _unpack_bf16_pair(jnp.where(valid, xs_ref[...], jnp.uint32(0)))
    half = D_MODEL // 2
    h = (_dot(lo.astype(BF16), w1_ref[0:half, :]) + _dot(hi.astype(BF16), w1_ref[half:, :]) + b1_ref[...])
    hg = W1_GROUP // 2
    acts = []
    for g in range(2 * D_EXPERT // W1_GROUP):
        glu = jnp.minimum(h[:, g * W1_GROUP:g * W1_GROUP + hg], SWIGLU_LIMIT)
        lin = jnp.clip(h[:, g * W1_GROUP + hg:(g + 1) * W1_GROUP], -SWIGLU_LIMIT, SWIGLU_LIMIT)
        acts.append((glu * jax.nn.sigmoid(SWIGLU_ALPHA * glu) * (lin + 1.0)).astype(BF16))
    y = _dot(jnp.concatenate(acts, axis=1), w2_ref[...]) + b2_ref[...]
    ys_ref[...] = _pack_bf16_pair(y[:, :half], y[:, half:])


def _w1_regroup_kernel(w_ref, o_ref):
    hg = W1_GROUP // 2
    r = lax.broadcasted_iota(I32, (W1_GROUP, W1_GROUP), 0)
    c = lax.broadcasted_iota(I32, (W1_GROUP, W1_GROUP), 1)
    perm = (r == jnp.where(c < hg, 2 * c, 2 * (c - hg) + 1)).astype(BF16)
    for g in range(2 * D_EXPERT // W1_GROUP):
        sl = slice(g * W1_GROUP, (g + 1) * W1_GROUP)
        o_ref[:, sl] = _dot(w_ref[:, sl].astype(BF16), perm).astype(BF16)


def _w1_regroup(exp_w1):
    depth, ne, dm, dh = exp_w1.shape
    tr = 512
    out = pl.pallas_call(
        _w1_regroup_kernel,
        out_shape=jax.ShapeDtypeStruct((depth * ne, dm, dh), BF16),
        grid=(depth * ne, dm // tr),
        in_specs=[pl.BlockSpec((None, tr, dh), lambda e, i: (e, i, 0))],
        out_specs=pl.BlockSpec((None, tr, dh), lambda e, i: (e, i, 0)),
        compiler_params=_cparams(("arbitrary", "arbitrary")),
        name="w1_regroup",
    )(exp_w1.reshape(depth * ne, dm, dh))
    return out.reshape(depth, ne, dm, dh)


def _experts(xs, block_expert, block_valid, w1p, b1p, w2, b2, l):
    n_slots, half = xs.shape
    bm = MOE_BM
    grid_spec = pltpu.PrefetchScalarGridSpec(
        num_scalar_prefetch=2,
        grid=(n_slots // bm,),
        in_specs=[
            pl.BlockSpec((bm, half), lambda i, be, bv: (i, 0)),
            pl.BlockSpec((None, None, D_MODEL, 2 * D_EXPERT), lambda i, be, bv: (l, be[i], 0, 0)),
            pl.BlockSpec((None, None, 1, 2 * D_EXPERT), lambda i, be, bv: (l, be[i], 0, 0)),
            pl.BlockSpec((None, None, D_EXPERT, D_MODEL), lambda i, be, bv: (l, be[i], 0, 0)),
            pl.BlockSpec((None, None, 1, D_MODEL), lambda i, be, bv: (l, be[i], 0, 0)),
        ],
        out_specs=pl.BlockSpec((bm, half), lambda i, be, bv: (i, 0)),
    )
    return pl.pallas_call(
        _expert_kernel,
        out_shape=jax.ShapeDtypeStruct((n_slots, half), U32),
        grid_spec=grid_spec,
        compiler_params=_cparams(("arbitrary",)),
        name="moe_experts",
    )(block_expert, block_valid, xs, w1p, b1p, w2, b2)


def _combine_kernel(gate_ref, x_ref, mod_ref, yg_ref, xo_ref, *, tm):
    gate = gate_ref[...]
    half = D_MODEL // 2
    f_lo = jnp.zeros((tm, half), F32)
    f_hi = jnp.zeros((tm, half), F32)
    for k in range(TOP_K):
        lo, hi = _unpack_bf16_pair(yg_ref[:, k * half:(k + 1) * half])
        g = gate[:, k:k + 1]
        f_lo = f_lo + g * lo
        f_hi = f_hi + g * hi
    g2 = mod_ref[5:6, :]
    xo_ref[:, :half] = x_ref[:, :half] + g2[:, :half] * f_lo
    xo_ref[:, half:] = x_ref[:, half:] + g2[:, half:] * f_hi


def _combine(x, mod, gate, yg, l, geo):
    t, dm = x.shape
    tm = geo["tm_moe"]
    mod_row = geo["mod_row"]
    return pl.pallas_call(
        functools.partial(_combine_kernel, tm=tm),
        out_shape=jax.ShapeDtypeStruct((t, dm), F32),
        grid=(t // tm,),
        in_specs=[
            pl.BlockSpec((tm, SUBLANES), lambda i: (i, 0)),
            pl.BlockSpec((tm, dm), lambda i: (i, 0)),
            pl.BlockSpec((None, None, N_MOD, dm), lambda i: (l, mod_row(i * tm), 0, 0)),
            pl.BlockSpec((tm, TOP_K * dm // 2), lambda i: (i, 0)),
        ],
        out_specs=pl.BlockSpec((tm, dm), lambda i: (i, 0)),
        compiler_params=_cparams(("arbitrary",)),
        name="moe_combine",
    )(gate, x, mod, yg)


def _moe_plan(idx, rank, counts, n_blocks):
    bm = MOE_BM
    padded = (counts + bm - 1) // bm * bm
    pad_end = jnp.cumsum(padded)
    pad_start = pad_end - padded
    expert = idx[:, :TOP_K]
    onehot = expert[:, :, None] == jnp.arange(N_EXPERTS, dtype=I32)[None, None, :]
    dest = rank[:, :TOP_K] + jnp.sum(jnp.where(onehot, pad_start[None, None, :], 0), axis=-1)
    block_start = jnp.arange(n_blocks, dtype=I32) * bm
    block_expert = jnp.minimum(
        jnp.sum((block_start[:, None] >= pad_end[None, :]).astype(I32), axis=-1), N_EXPERTS - 1)
    block_valid = jnp.clip(counts[block_expert] - (block_start - pad_start[block_expert]), 0, bm)
    per = (D_MODEL // 2) // SC_ROW
    sub = jnp.arange(per, dtype=I32)
    dest_rows = (dest.astype(I32) * per)[:, :, None] + sub[None, None, :]
    scatter_idx = jnp.transpose(dest_rows, (1, 0, 2)).reshape(TOP_K, -1)
    gather_idx = dest_rows.reshape(-1)
    return scatter_idx, gather_idx, block_expert.astype(I32), block_valid.astype(I32)


def _rope_tables(n_lat):
    rows = n_lat // GRID_W
    row = jnp.repeat(jnp.arange(rows, dtype=F32), GRID_W)
    colp = jnp.tile(jnp.arange(GRID_W, dtype=F32), rows)
    n_freq = HEAD_DIM // 4
    inv = ROPE_BASE ** (-jnp.arange(n_freq, dtype=F32) / n_freq)
    ang = jnp.concatenate([row[:, None] * inv, colp[:, None] * inv], axis=-1)
    cos, sin = jnp.cos(ang), jnp.sin(ang)
    cr, cc = cos[:, :n_freq], cos[:, n_freq:]
    sr, sc = sin[:, :n_freq], sin[:, n_freq:]
    cos64 = jnp.concatenate([cr, cr, cc, cc], axis=-1)
    sin64 = jnp.concatenate([-sr, sr, -sc, sc], axis=-1)
    return jnp.tile(cos64, (1, 2)), jnp.tile(sin64, (1, 2))


def _prep_weights(w_in, w_branch, w_out, router_w, router_b, sg_b, exp_w1, exp_b1, exp_w2, exp_b2,
                  attn_q_norm, attn_k_norm):
    depth, dm, _ = w_in.shape
    qkvz = 2048
    ab0, sc0, sg0, at0, mg0 = 2048, 2064, 3600, 4624, 5392
    pad = PROJ_COLS - (COL_AB + 16)
    w_p = jnp.concatenate([
        w_in[:, :, 0:qkvz],
        w_in[:, :, sc0:sg0],
        w_in[:, :, sg0:at0],
        w_in[:, :, at0:at0 + 512],
        w_in[:, :, mg0:mg0 + 4096],
        w_in[:, :, at0 + 512:at0 + 768],
        w_in[:, :, ab0:ab0 + 16],
        jnp.zeros((depth, dm, pad), w_in.dtype),
    ], axis=-1).astype(BF16)
    rw_hi = router_w.astype(BF16)
    rw_lo = (router_w - rw_hi.astype(F32)).astype(BF16)
    rw = jnp.concatenate([
        jnp.concatenate([rw_hi, rw_lo, jnp.zeros((depth, dm, LANES - 2 * N_EXPERTS), BF16)], axis=-1),
        jnp.concatenate([rw_hi, jnp.zeros((depth, dm, LANES - N_EXPERTS), BF16)], axis=-1),
    ], axis=1)
    rb = jnp.concatenate([router_b, jnp.full((depth, LANES - N_EXPERTS), NEG_INF, F32)], axis=-1)
    w1p = _w1_regroup(exp_w1)
    ne = exp_b1.shape[1]
    b1p = jnp.swapaxes(exp_b1.reshape(depth, ne, -1, W1_GROUP // 2, 2), -1, -2).reshape(depth, ne, -1)
    return {
        "w_in": w_p,
        "w_branch": w_branch.astype(BF16),
        "w_out": w_out.astype(BF16),
        "router_w": rw,
        "router_b": rb[:, None, :],
        "sg_b": jnp.repeat(jnp.swapaxes(sg_b, 1, 2), SG_WIDTH // SG_GROUPS, axis=2),
        "w1": w1p,
        "b1": b1p[:, :, None, :],
        "w2": exp_w2.astype(BF16),
        "b2": exp_b2[:, :, None, :],
        "q_norm": jnp.tile(attn_q_norm, (1, 2))[:, None, :],
        "k_norm": jnp.tile(attn_k_norm, (1, 2))[:, None, :],
    }


def kernel(x, c, ctx, c_ctx, norm1, norm2, w_mod, b_mod, w_in, gdn_conv, gdn_a_log, gdn_dt_bias, gdn_out_norm,
           sc_conv, sg_ln_w, sg_ln_b, sg_w, sg_b, attn_q_norm, attn_k_norm, attn_sink, w_branch, w_out,
           router_w, router_b, exp_w1, exp_b1, exp_w2, exp_b2):
    b, n_lat, dm = x.shape
    n_ctx = ctx.shape[1]
    depth = w_in.shape[0]
    tc = b * n_ctx
    t = tc + b * n_lat
    assert dm == D_MODEL and tc % n_lat == 0 and n_lat % 256 == 0 and n_ctx % 256 == 0
    tile_cap = math.gcd(tc, n_lat)

    def mod_row(r0):
        return jnp.where(r0 < tc, 0, 1 + (r0 - tc) // n_lat)

    geo = {
        "b": b, "ctx": n_ctx, "lat": n_lat, "tc": tc, "mod_row": mod_row,
        "tm_proj": min(1024, tile_cap), "tm_merge": min(512, tile_cap), "tm_moe": min(512, tile_cap),
    }

    wts = _prep_weights(w_in, w_branch, w_out, router_w, router_b, sg_b, exp_w1, exp_b1, exp_w2, exp_b2,
                        attn_q_norm, attn_k_norm)
    rope_cos, rope_sin = _rope_tables(n_lat)

    mod_rows = -(-(1 + b) // SUBLANES) * SUBLANES
    c_all = jnp.concatenate([c_ctx[None, :], c, jnp.zeros((mod_rows - 1 - b, dm), F32)], axis=0)
    mod = _modulation(c_all, w_mod, b_mod).reshape(depth, mod_rows, N_MOD, dm)

    xs_flat = jnp.concatenate([ctx.reshape(tc, dm), x.reshape(b * n_lat, dm)], axis=0)
    n_blocks = -(-(t * TOP_K) // MOE_BM) + N_EXPERTS
    n_slots = n_blocks * MOE_BM

    for l in range(depth):
        last = l == depth - 1
        p = _in_proj(xs_flat, mod, norm1[:, None, :], wts["w_in"], l, geo)
        g_lat, g_ctx = _gdn(p, gdn_conv, gdn_a_log, gdn_dt_bias, gdn_out_norm[:, None, :], l, geo)
        a_lat = _attention(p, attn_sink, wts["q_norm"], wts["k_norm"], rope_cos, rope_sin, l, geo, False)
        if last:
            a_ctx = jnp.zeros((tc, ATT_HEADS * HEAD_DIM), BF16)
        else:
            a_ctx = _attention(p, attn_sink, wts["q_norm"], wts["k_norm"], rope_cos, rope_sin, l, geo, True)
        y0 = jnp.concatenate([g_ctx, g_lat], axis=0)
        y3 = jnp.concatenate([a_ctx, a_lat], axis=0)
        layer_w = {
            "sc_conv": sc_conv, "sg_ln_w": sg_ln_w[:, None, :], "sg_ln_b": sg_ln_b[:, None, :], "sg_w": sg_w.astype(BF16),
            "sg_b": wts["sg_b"], "w_branch": wts["w_branch"], "w_out": wts["w_out"],
            "router_w": wts["router_w"], "router_b": wts["router_b"],
        }
        x_mid, h2p, idx, gate, rank, counts = _merge(xs_flat, mod, norm2[:, None, :], y0, y3, p, layer_w, l, geo)
        scatter_idx, gather_idx, block_expert, block_valid = _moe_plan(idx, rank, counts[0, :N_EXPERTS], n_blocks)
        per = (dm // 2) // SC_ROW
        xs_sorted = _sc_scatter_rows(h2p.reshape(t * per, SC_ROW), scatter_idx, n_slots * per)
        ys_sorted = _experts(xs_sorted.reshape(n_slots, dm // 2), block_expert, block_valid,
                             wts["w1"], wts["b1"], wts["w2"], wts["b2"], l)
        yg = _sc_gather_rows(ys_sorted.reshape(n_slots * per, SC_ROW), gather_idx)
        xs_flat = _combine(x_mid, mod, gate, yg.reshape(t, TOP_K * dm // 2), l, geo)

    return xs_flat[tc:].reshape(b, n_lat, dm)
```

```python
import functools
import math

import jax
import jax.numpy as jnp
from jax import lax
from jax.experimental import pallas as pl
from jax.experimental.pallas import tpu as pltpu
from jax.experimental.pallas import tpu_sc as plsc

F32 = jnp.float32
BF16 = jnp.bfloat16
I32 = jnp.int32
U32 = jnp.uint32
HIGHEST = lax.Precision.HIGHEST

D_MODEL = 1024
GRID_W = 64
N_MOD = 6
GDN_HEADS = 4
GDN_DK = 128
GDN_DV = 128
GDN_CONV = 5
SC_WIDTH = 512
SC_CONV = 3
SG_CHUNK = 128
SG_GROUPS = 4
SG_WIDTH = 512
ATT_HEADS = 8
ATT_KV_HEADS = 2
HEAD_DIM = 64
WINDOW = 128
ROPE_BASE = 10000.0
N_BRANCH = 4
BRANCH_W = 512
N_EXPERTS = 32
TOP_K = 4
D_EXPERT = 1024
SWIGLU_ALPHA = 1.702
SWIGLU_LIMIT = 7.0
NORM_EPS = 1e-6
LN_EPS = 1e-5
NEG_INF = -1e30

LANES = 128
SUBLANES = 8
VMEM_LIMIT_BYTES = 56 * 2**20

COL_Q = 0
COL_K = 512
COL_V = 1024
COL_Z = 1536
COL_SB = 2048
COL_SC = 2560
COL_SX = 3072
COL_GU = 3584
COL_GV = 4096
COL_AQ = 4608
COL_MG = 5120
COL_AK = 9216
COL_AV = 9344
COL_AB = 9472
PROJ_COLS = 10240
PROJ_TN = 2048
PROJ_CHUNK = 512

GDN_CHUNK = 128
GDN_SOLVE_BLOCK = 64
GDN_PREP_UNROLL = 4
MOE_BM = 512
W1_GROUP = 256
SC_ROW = 256
SC_WINDOW = 128
SC_PLANES = (D_MODEL // 2) // SC_ROW


def _cparams(sem):
    return pltpu.CompilerParams(dimension_semantics=sem, vmem_limit_bytes=VMEM_LIMIT_BYTES)


def _silu(x):
    return x * jax.nn.sigmoid(x)


def _dot(a, b, precision=None):
    return jnp.dot(a, b, preferred_element_type=F32, precision=precision)


def _dot_nt(a, b):
    return lax.dot_general(a, b, (((1,), (1,)), ((), ())), preferred_element_type=F32)


def _dot_tn(a, b):
    return lax.dot_general(a, b, (((0,), (0,)), ((), ())), preferred_element_type=F32)


def _aligned(start, multiple):
    return start if isinstance(start, int) else pl.multiple_of(start, multiple)


def _pack_bf16_pair(lo, hi):
    lo_b = pltpu.bitcast(lo.astype(BF16).astype(F32), U32)
    hi_b = pltpu.bitcast(hi.astype(BF16).astype(F32), U32)
    return (hi_b & jnp.uint32(0xFFFF0000)) | (lo_b >> 16)


def _unpack_bf16_pair(u):
    lo = pltpu.bitcast(u << 16, F32)
    hi = pltpu.bitcast(u & jnp.uint32(0xFFFF0000), F32)
    return lo, hi


def _store_planes(ref, packed):
    for p in range(SC_PLANES):
        ref[p] = packed[:, p * SC_ROW:(p + 1) * SC_ROW]


def _load_planes(ref, first=0):
    return jnp.concatenate([ref[first + p] for p in range(SC_PLANES)], axis=1)


def _mod_kernel(c_ref, w_ref, b_ref, o_ref):
    s = _silu(c_ref[...])
    o_ref[...] = _dot(s, w_ref[...], precision=HIGHEST) + b_ref[...]


def _modulation(c_all, w_mod, b_mod):
    depth, dm, nm = w_mod.shape
    rows = c_all.shape[0]
    tn = 1536
    return pl.pallas_call(
        _mod_kernel,
        out_shape=jax.ShapeDtypeStruct((depth, rows, nm), F32),
        grid=(depth, nm // tn),
        in_specs=[
            pl.BlockSpec((rows, dm), lambda l, j: (0, 0)),
            pl.BlockSpec((None, dm, tn), lambda l, j: (l, 0, j)),
            pl.BlockSpec((None, 1, tn), lambda l, j: (l, 0, j)),
        ],
        out_specs=pl.BlockSpec((None, rows, tn), lambda l, j: (l, 0, j)),
        compiler_params=_cparams(("arbitrary", "arbitrary")),
        name="modulation",
    )(c_all, w_mod, b_mod.reshape(depth, 1, nm))


def _in_proj_kernel(x_ref, mod_ref, nw_ref, w_ref, o_ref):
    x = x_ref[...]
    ms = jnp.mean(x * x, axis=-1, keepdims=True)
    y = x * lax.rsqrt(ms + NORM_EPS) * nw_ref[...]
    h = (y * (1.0 + mod_ref[1:2, :]) + mod_ref[0:1, :]).astype(BF16)
    for c in range(PROJ_TN // PROJ_CHUNK):
        sl = slice(c * PROJ_CHUNK, (c + 1) * PROJ_CHUNK)
        o_ref[:, sl] = _dot(h, w_ref[:, sl]).astype(BF16)


def _in_proj(x, mod, norm_w, w_p, l, geo):
    t, dm = x.shape
    tm = geo["tm_proj"]
    mod_row = geo["mod_row"]
    return pl.pallas_call(
        _in_proj_kernel,
        out_shape=jax.ShapeDtypeStruct((t, PROJ_COLS), BF16),
        grid=(PROJ_COLS // PROJ_TN, t // tm),
        in_specs=[
            pl.BlockSpec((tm, dm), lambda j, i: (i, 0)),
            pl.BlockSpec((None, None, N_MOD, dm), lambda j, i: (l, mod_row(i * tm), 0, 0)),
            pl.BlockSpec((None, 1, dm), lambda j, i: (l, 0, 0)),
            pl.BlockSpec((None, dm, PROJ_TN), lambda j, i: (l, 0, j)),
        ],
        out_specs=pl.BlockSpec((tm, PROJ_TN), lambda j, i: (i, j)),
        compiler_params=_cparams(("arbitrary", "arbitrary")),
        name="in_proj",
    )(x, mod, norm_w, w_p)


def _softplus(x):
    return jnp.maximum(x, 0.0) + jnp.log1p(jnp.exp(-jnp.abs(x)))


def _gdn_kernel(alog_ref, dtb_ref,
                ql_ref, kl_ref, vl_ref, zl_ref, abl_ref,
                qc_ref, kc_ref, vc_ref, zc_ref, abc_ref,
                cwq_ref, cwk_ref, cwv_ref, onorm_ref,
                yl_ref, yc_ref,
                xpad, qs, ks, vs, gsc, bsc,
                u_sc, w16, qk16, qd16, kdk16, glast, oacc,
                *, n_ctx, n_lat):
    cs = GDN_CHUNK
    hd = GDN_DK
    n_all = n_ctx + n_lat
    n_chunks = n_all // cs
    n_ctx_chunks = n_ctx // cs
    h = pl.program_id(1)
    pad = SUBLANES
    lat_off = n_ctx + 3 * pad

    zeros_pad = jnp.zeros((pad, hd), F32)
    xpad[0:pad, :] = zeros_pad
    xpad[pad + n_ctx:pad + n_ctx + 2 * pad, :] = jnp.zeros((2 * pad, hd), F32)
    xpad[lat_off + n_lat:lat_off + n_lat + pad, :] = zeros_pad

    half = GDN_CONV // 2
    tile = 256

    def conv_into(src_c, src_l, cw_ref, dst, mode):
        xpad[pad:pad + n_ctx, :] = src_c[...].astype(F32)
        xpad[lat_off:lat_off + n_lat, :] = src_l[...].astype(F32)
        w = cw_ref[...]
        for seg_off, dst_off, seg_len in ((pad, 0, n_ctx), (lat_off, n_ctx, n_lat)):
            for r0 in range(0, seg_len, tile):
                acc = jnp.zeros((tile, hd), F32)
                for j in range(GDN_CONV):
                    s = seg_off + r0 + j - half
                    acc = acc + xpad[s:s + tile, :] * w[j:j + 1, :]
                y = _silu(acc)
                if mode != "v":
                    y = y * lax.rsqrt(jnp.sum(y * y, axis=-1, keepdims=True) + NORM_EPS)
                if mode == "q":
                    y = y * (hd ** -0.5)
                dst[dst_off + r0:dst_off + r0 + tile, :] = y

    conv_into(qc_ref, ql_ref, cwq_ref, qs, "q")
    conv_into(kc_ref, kl_ref, cwk_ref, ks, "k")
    conv_into(vc_ref, vl_ref, cwv_ref, vs, "v")

    sel_r = lax.broadcasted_iota(I32, (LANES, LANES), 0)
    pos = lax.broadcasted_iota(I32, (tile, LANES), 0) % cs
    for d in range(2):
        col = d * GDN_HEADS + h
        sel_a = (sel_r == col).astype(BF16)
        sel_b = (sel_r == (2 * GDN_HEADS + col)).astype(BF16)
        neg_a = -jnp.exp(jnp.full((1, LANES), alog_ref[d, h], F32))
        dtb = dtb_ref[d, h]
        for src, r_off, r_len in ((abc_ref, 0, n_ctx), (abl_ref, n_ctx, n_lat)):
            for r0 in range(0, r_len, tile):
                ab = src[r0:r0 + tile, :]
                a_b = _dot(ab, sel_a)
                b_b = _dot(ab, sel_b)
                gc = neg_a * _softplus(a_b + dtb)
                s = 1
                while s < cs:
                    if d == 0:
                        gc = gc + jnp.where(pos >= s, pltpu.roll(gc, s, 0), 0.0)
                    else:
                        gc = gc + jnp.where(pos < cs - s, pltpu.roll(gc, tile - s, 0), 0.0)
                    s *= 2
                gsc[d, r_off + r0:r_off + r0 + tile, :] = gc
                bsc[d, r_off + r0:r_off + r0 + tile, :] = jax.nn.sigmoid(b_b)

    ri = lax.broadcasted_iota(I32, (cs, cs), 0)
    ci = lax.broadcasted_iota(I32, (cs, cs), 1)
    low = (ri >= ci)
    up = (ri <= ci)
    eye = (ri == ci).astype(F32)
    offdiag = (ri != ci).astype(F32)
    assert cs in (GDN_SOLVE_BLOCK, 2 * GDN_SOLVE_BLOCK)
    same_blk = ((ri // GDN_SOLVE_BLOCK) == (ci // GDN_SOLVE_BLOCK)).astype(F32)
    n_levels = int(math.log2(min(cs, GDN_SOLVE_BLOCK))) - 1

    def prep_group(first_chunk, count):
        chunk_ids = [first_chunk + uu for uu in range(count)]
        rows = [pl.ds(_aligned(c * cs, cs), cs) for c in chunk_ids]
        q = [qs[r, :] for r in rows]
        k = [ks[r, :] for r in rows]
        v = [vs[r, :] for r in rows]
        kb = [x.astype(BF16) for x in k]
        kkt = [_dot_nt(x, x) for x in kb]
        qkt = [_dot_nt(a.astype(BF16), b) for a, b in zip(q, kb)]
        probs = [(ui, d) for ui in range(count) for d in range(2)]
        gc_col, beta_b, dmat, t, p, n_off = {}, {}, {}, {}, {}, {}
        for ui, d in probs:
            gc = gsc[d, rows[ui], :]
            bt = bsc[d, rows[ui], :]
            gsq = gc[:, :cs]
            dm = jnp.where(low if d == 0 else up, jnp.exp(gsq - gsq.T), 0.0)
            n = -(kkt[ui] * bt[:, :cs] * dm * offdiag)
            gc_col[ui, d], beta_b[ui, d], dmat[ui, d] = gc, bt, dm
            n_diag = n * same_blk
            t[ui, d] = eye + n_diag
            p[ui, d] = n_diag.astype(BF16)
            n_off[ui, d] = (n - n_diag).astype(BF16)
        p = {key: _dot(x, x).astype(BF16) for key, x in p.items()}
        for lvl in range(n_levels):
            t_next = {key: t[key] + _dot(p[key], t[key].astype(BF16)) for key in probs}
            if lvl + 1 < n_levels:
                p = {key: _dot(x, x).astype(BF16) for key, x in p.items()}
            t = t_next
        tb = {key: x.astype(BF16) for key, x in t.items()}
        if cs > GDN_SOLVE_BLOCK:
            x_off = {key: _dot(tb[key], n_off[key]).astype(BF16) for key in probs}
            t = {key: t[key] + _dot(x_off[key], tb[key]) for key in probs}
            tb = {key: x.astype(BF16) for key, x in t.items()}
        eg = {key: jnp.exp(x) for key, x in gc_col.items()}
        u = {(ui, d): _dot(tb[ui, d], (v[ui] * beta_b[ui, d]).astype(BF16)) for ui, d in probs}
        w = {(ui, d): _dot(tb[ui, d], (k[ui] * beta_b[ui, d] * eg[ui, d]).astype(BF16)) for ui, d in probs}
        for ui, d in probs:
            r = rows[ui]
            last = cs - 1 if d == 0 else 0
            gl = gc_col[ui, d][last:last + 1, :]
            u_sc[d, r, :] = u[ui, d]
            w16[d, r, :] = w[ui, d].astype(BF16)
            qk16[d, r, :] = (qkt[ui] * dmat[ui, d]).astype(BF16)
            qd16[d, r, :] = (q[ui] * eg[ui, d]).astype(BF16)
            kdk16[d, r, :] = (k[ui] * jnp.exp(gl - gc_col[ui, d])).astype(BF16)
            g0 = _aligned(chunk_ids[ui] * SUBLANES, SUBLANES)
            glast[d, pl.ds(g0, SUBLANES), :] = jnp.broadcast_to(jnp.exp(gl), (SUBLANES, LANES))

    def prep(gi, carry):
        prep_group(gi * GDN_PREP_UNROLL, GDN_PREP_UNROLL)
        return carry

    n_groups = n_chunks // GDN_PREP_UNROLL
    lax.fori_loop(0, n_groups, prep, 0)
    if n_chunks % GDN_PREP_UNROLL:
        prep_group(n_groups * GDN_PREP_UNROLL, n_chunks % GDN_PREP_UNROLL)

    oacc[...] = jnp.zeros_like(oacc)

    def step(i, carry):
        c_b = jnp.where(i < n_ctx_chunks, n_ctx_chunks - 1 - i, n_chunks + n_ctx_chunks - 1 - i)
        cid = (i, c_b)
        rows = [pl.ds(pl.multiple_of(c * cs, cs), cs) for c in cid]
        sb = [s.astype(BF16) for s in carry]
        ws = [_dot(w16[d, rows[d], :], sb[d]) for d in range(2)]
        qs_s = [_dot(qd16[d, rows[d], :], sb[d]) for d in range(2)]
        vnb = [(u_sc[d, rows[d], :] - ws[d]).astype(BF16) for d in range(2)]
        o = [qs_s[d] + _dot(qk16[d, rows[d], :], vnb[d]) for d in range(2)]
        kv = [_dot_tn(kdk16[d, rows[d], :], vnb[d]) for d in range(2)]
        s_new = []
        for d in range(2):
            g0 = pl.multiple_of(cid[d] * SUBLANES, SUBLANES)
            s_new.append(carry[d] * glast[d, pl.ds(g0, 1), :] + kv[d])
            oacc[rows[d], :] = oacc[rows[d], :] + o[d]
        return tuple(s_new)

    s0 = jnp.zeros((hd, GDN_DV), F32)
    lax.fori_loop(0, n_chunks, step, (s0, s0))

    onw = onorm_ref[...]

    def out_gate(o, z):
        y = o * lax.rsqrt(jnp.mean(o * o, axis=-1, keepdims=True) + NORM_EPS) * onw
        return (y * _silu(z.astype(F32))).astype(BF16)

    yc_ref[...] = out_gate(oacc[0:n_ctx, :], zc_ref[...])
    for r0 in range(0, n_lat, tile):
        yl_ref[r0:r0 + tile, :] = out_gate(oacc[n_ctx + r0:n_ctx + r0 + tile, :], zl_ref[r0:r0 + tile, :])


def _gdn(p, conv_w, a_log, dt_bias, out_norm, l, geo):
    b, n_ctx, n_lat = geo["b"], geo["ctx"], geo["lat"]
    n_all = n_ctx + n_lat
    lat0 = geo["tc"] // n_lat
    hd = GDN_DK

    def lat_spec(col):
        return pl.BlockSpec((n_lat, hd), lambda bi, h: (lat0 + bi, col // hd + h))

    def ctx_spec(col):
        return pl.BlockSpec((n_ctx, hd), lambda bi, h: (bi, col // hd + h))

    def fixed_lat(col):
        return pl.BlockSpec((n_lat, hd), lambda bi, h: (lat0 + bi, col // hd))

    def fixed_ctx(col):
        return pl.BlockSpec((n_ctx, hd), lambda bi, h: (bi, col // hd))

    def cw_spec(col):
        return pl.BlockSpec((None, GDN_CONV, hd), lambda bi, h: (l, 0, col // hd + h))

    smem = pl.BlockSpec(memory_space=pltpu.SMEM)
    kern = functools.partial(_gdn_kernel, n_ctx=n_ctx, n_lat=n_lat)
    n_chunks = n_all // GDN_CHUNK
    assert n_ctx % GDN_CHUNK == 0 and n_lat % GDN_CHUNK == 0
    scratch = [
        pltpu.VMEM((n_all + 4 * SUBLANES, hd), F32),
        pltpu.VMEM((n_all, hd), F32),
        pltpu.VMEM((n_all, hd), F32),
        pltpu.VMEM((n_all, hd), F32),
        pltpu.VMEM((2, n_all, LANES), F32),
        pltpu.VMEM((2, n_all, LANES), F32),
        pltpu.VMEM((2, n_all, GDN_DV), F32),
        pltpu.VMEM((2, n_all, hd), BF16),
        pltpu.VMEM((2, n_all, GDN_CHUNK), BF16),
        pltpu.VMEM((2, n_all, hd), BF16),
        pltpu.VMEM((2, n_all, hd), BF16),
        pltpu.VMEM((2, n_chunks * SUBLANES, LANES), F32),
        pltpu.VMEM((n_all, GDN_DV), F32),
    ]
    y_lat, y_ctx = pl.pallas_call(
        kern,
        out_shape=(jax.ShapeDtypeStruct((b * n_lat, GDN_HEADS * GDN_DV), BF16),
                   jax.ShapeDtypeStruct((b * n_ctx, GDN_HEADS * GDN_DV), BF16)),
        grid=(b, GDN_HEADS),
        in_specs=[smem, smem,
                  lat_spec(COL_Q), lat_spec(COL_K), lat_spec(COL_V), lat_spec(COL_Z), fixed_lat(COL_AB),
                  ctx_spec(COL_Q), ctx_spec(COL_K), ctx_spec(COL_V), ctx_spec(COL_Z), fixed_ctx(COL_AB),
                  cw_spec(0), cw_spec(GDN_HEADS * hd), cw_spec(2 * GDN_HEADS * hd),
                  pl.BlockSpec((None, 1, GDN_DV), lambda bi, h: (l, 0, 0))],
        out_specs=(pl.BlockSpec((n_lat, GDN_DV), lambda bi, h: (bi, h)),
                   pl.BlockSpec((n_ctx, GDN_DV), lambda bi, h: (bi, h))),
        scratch_shapes=scratch,
        compiler_params=_cparams(("arbitrary", "arbitrary")),
        name="gdn",
    )(a_log[l], dt_bias[l], p, p, p, p, p, p, p, p, p, p, conv_w, conv_w, conv_w, out_norm)
    return y_lat, y_ctx


def _head_rms(x, bd, w):
    ss = _dot(x * x, bd, precision=HIGHEST) * (1.0 / HEAD_DIM)
    return x * lax.rsqrt(ss + NORM_EPS) * w


def _rope(x, cos, sin_signed, lane):
    nf = HEAD_DIM // 4
    partner = jnp.where((lane % (2 * nf)) < nf, pltpu.roll(x, LANES - nf, 1), pltpu.roll(x, nf, 1))
    return x * cos + partner * sin_signed


def _attn_kernel(sink_ref, q_ref, kq_ref, vq_ref, kc_ref, vc_ref,
                 cosq_ref, sinq_ref, cosk_ref, sink_k_ref, qw_ref, kw_ref,
                 o_ref, kp, vp, kcp, vcp, *, n_keys, n_ctx, local):
    n = pl.program_id(1)
    w = WINDOW
    lane = lax.broadcasted_iota(I32, (1, LANES), 1)
    lo_mask = (lane < HEAD_DIM).astype(F32)
    hi_mask = 1.0 - lo_mask
    bd_r = lax.broadcasted_iota(I32, (LANES, LANES), 0) // HEAD_DIM
    bd_c = lax.broadcasted_iota(I32, (LANES, LANES), 1) // HEAD_DIM
    bd = (bd_r == bd_c).astype(F32)
    kw = kw_ref[...]

    def store_variants(dst, row0, x):
        xr = pltpu.roll(x, HEAD_DIM, 1)
        rows = x.shape[0]
        dst[0, row0:row0 + rows, :] = (x * lo_mask).astype(BF16)
        dst[1, row0:row0 + rows, :] = (xr * hi_mask).astype(BF16)
        dst[2, row0:row0 + rows, :] = (xr * lo_mask).astype(BF16)
        dst[3, row0:row0 + rows, :] = (x * hi_mask).astype(BF16)

    @pl.when(n == 0)
    def _prepare_keys():
        tile = 256
        for r0 in range(0, n_ctx, tile):
            rr = min(tile, n_ctx - r0)
            kc = _head_rms(kc_ref[r0:r0 + rr, :].astype(F32), bd, kw)
            store_variants(kcp, r0, kc)
            store_variants(vcp, r0, vc_ref[r0:r0 + rr, :].astype(F32))
        if local:
            zero = jnp.zeros((w, LANES), BF16)
            for t in range(4):
                kp[t, 0:w, :] = zero
                kp[t, w + n_keys:2 * w + n_keys, :] = zero
                vp[t, 0:w, :] = zero
                vp[t, w + n_keys:2 * w + n_keys, :] = zero
            for r0 in range(0, n_keys, tile):
                k = _head_rms(kq_ref[r0:r0 + tile, :].astype(F32), bd, kw)
                k = _rope(k, cosk_ref[r0:r0 + tile, :], sink_k_ref[r0:r0 + tile, :], lane)
                store_variants(kp, w + r0, k)
                store_variants(vp, w + r0, vq_ref[r0:r0 + tile, :].astype(F32))

    qw = qw_ref[...]
    q2 = []
    for g in range(ATT_HEADS // 2):
        qg = _head_rms(q_ref[:, g * LANES:(g + 1) * LANES].astype(F32), bd, qw)
        if local:
            qg = _rope(qg, cosq_ref[...], sinq_ref[...], lane)
        q2.append((qg * (HEAD_DIM ** -0.5)).astype(BF16))

    rows2 = 2 * w
    row_i = lax.broadcasted_iota(I32, (rows2, 1), 0)
    first = row_i < w
    if local:
        n_blk = n_keys // w
        qq = lax.broadcasted_iota(I32, (rows2, 3 * w), 0) % w
        kcol = lax.broadcasted_iota(I32, (rows2, 3 * w), 1)
        kk = kcol % w
        blk = kcol // w
        valid = ((blk == 1)
                 | ((blk == 0) & (kk >= qq) & (n >= 1))
                 | ((blk == 2) & (kk <= qq) & (n < n_blk - 1)))
        k0 = pl.multiple_of(n * w, w)

    for j in range(ATT_KV_HEADS):
        lhs = jnp.concatenate([q2[2 * j], q2[2 * j + 1]], axis=0)
        pair_out = [None, None]
        for t in range(2):
            var = 2 * j + t
            sink = jnp.where(first, sink_ref[4 * j + t], sink_ref[4 * j + 2 + t])
            s_ctx = _dot_nt(lhs, kcp[var])
            m = jnp.maximum(jnp.max(s_ctx, axis=-1, keepdims=True), sink)
            if local:
                s_loc = _dot_nt(lhs, kp[var, pl.ds(k0, 3 * w), :])
                s_loc = jnp.where(valid, s_loc, NEG_INF)
                m = jnp.maximum(m, jnp.max(s_loc, axis=-1, keepdims=True))
            e_ctx = jnp.exp(s_ctx - m)
            den = jnp.sum(e_ctx, axis=-1, keepdims=True) + jnp.exp(sink - m)
            pv = _dot(e_ctx.astype(BF16), vcp[var])
            if local:
                e_loc = jnp.exp(s_loc - m)
                den = den + jnp.sum(e_loc, axis=-1, keepdims=True)
                pv = pv + _dot(e_loc.astype(BF16), vp[var, pl.ds(k0, 3 * w), :])
            o = pv / den
            for half_i in range(2):
                part = o[half_i * w:(half_i + 1) * w, :]
                pair_out[half_i] = part if pair_out[half_i] is None else pair_out[half_i] + part
        for half_i in range(2):
            c0 = (2 * j + half_i) * LANES
            o_ref[:, c0:c0 + LANES] = pair_out[half_i].astype(BF16)


def _attention(p, sink, q_norm_w, k_norm_w, rope_cos, rope_sin, l, geo, ctx_queries):
    b, n_ctx, n_lat = geo["b"], geo["ctx"], geo["lat"]
    w = WINDOW
    lat0 = geo["tc"] // n_lat
    local = not ctx_queries
    n_q = n_ctx if ctx_queries else n_lat
    nqb = n_q // w
    q_row0 = 0 if ctx_queries else geo["tc"] // w
    kern = functools.partial(_attn_kernel, n_keys=n_lat, n_ctx=n_ctx, local=local)
    in_specs = [
        pl.BlockSpec(memory_space=pltpu.SMEM),
        pl.BlockSpec((w, ATT_HEADS * HEAD_DIM), lambda bi, n: (q_row0 + bi * nqb + n, COL_AQ // 512)),
        pl.BlockSpec((n_lat, LANES), lambda bi, n: (lat0 + bi, COL_AK // LANES)),
        pl.BlockSpec((n_lat, LANES), lambda bi, n: (lat0 + bi, COL_AV // LANES)),
        pl.BlockSpec((n_ctx, LANES), lambda bi, n: (bi, COL_AK // LANES)),
        pl.BlockSpec((n_ctx, LANES), lambda bi, n: (bi, COL_AV // LANES)),
        pl.BlockSpec((w, LANES), lambda bi, n: (n if local else 0, 0)),
        pl.BlockSpec((w, LANES), lambda bi, n: (n if local else 0, 0)),
        pl.BlockSpec((n_lat, LANES), lambda bi, n: (0, 0)),
        pl.BlockSpec((n_lat, LANES), lambda bi, n: (0, 0)),
        pl.BlockSpec((None, 1, LANES), lambda bi, n: (l, 0, 0)),
        pl.BlockSpec((None, 1, LANES), lambda bi, n: (l, 0, 0)),
    ]
    scratch = [
        pltpu.VMEM((4, n_lat + 2 * w, LANES), BF16),
        pltpu.VMEM((4, n_lat + 2 * w, LANES), BF16),
        pltpu.VMEM((4, n_ctx, LANES), BF16),
        pltpu.VMEM((4, n_ctx, LANES), BF16),
    ]
    return pl.pallas_call(
        kern,
        out_shape=jax.ShapeDtypeStruct((b * n_q, ATT_HEADS * HEAD_DIM), BF16),
        grid=(b, nqb),
        in_specs=in_specs,
        out_specs=pl.BlockSpec((w, ATT_HEADS * HEAD_DIM), lambda bi, n: (bi * nqb + n, 0)),
        scratch_shapes=scratch,
        compiler_params=_cparams(("arbitrary", "arbitrary")),
        name="attn_ctx" if ctx_queries else "attn_lat",
    )(sink[l], p, p, p, p, p, rope_cos, rope_sin, rope_cos, rope_sin, q_norm_w, k_norm_w)


def _merge_kernel(x_ref, mod_ref, n2_ref, y0_ref, y3_ref,
                  sb_ref, sc_ref, sx_ref, scp_ref, sxp_ref, scn_ref, sxn_ref,
                  gu_ref, gv_ref, mg0_ref, mg1_ref, mg2_ref, mg3_ref,
                  scw_ref, lnw_ref, lnb_ref, sgw_ref, sgb_ref, wb_ref, wo_ref, rw_ref, rb_ref,
                  xo_ref, h2_ref, idx_ref, gate_ref, rank_ref, cnt_ref,
                  carry, *, tm, tc, n_ctx, n_lat):
    i = pl.program_id(0)
    r0 = i * tm

    row = lax.broadcasted_iota(I32, (tm, 1), 0)
    g_row = r0 + row
    in_ctx = g_row < tc
    seg_pos = jnp.where(in_ctx, g_row % n_ctx, (g_row - tc) % n_lat)
    seg_len = jnp.where(in_ctx, n_ctx, n_lat)
    cx = sc_ref[...].astype(F32) * sx_ref[...].astype(F32)
    cx_prev_halo = scp_ref[SUBLANES - 1:SUBLANES, :].astype(F32) * sxp_ref[SUBLANES - 1:SUBLANES, :].astype(F32)
    cx_next_halo = scn_ref[0:1, :].astype(F32) * sxn_ref[0:1, :].astype(F32)
    prev = jnp.where(row == 0, cx_prev_halo, pltpu.roll(cx, 1, 0))
    prev = jnp.where(seg_pos == 0, 0.0, prev)
    nxt = jnp.where(row == tm - 1, cx_next_halo, pltpu.roll(cx, tm - 1, 0))
    nxt = jnp.where(seg_pos == seg_len - 1, 0.0, nxt)
    scw = scw_ref[...]
    y1 = sb_ref[...].astype(F32) * (prev * scw[0:1, :] + cx * scw[1:2, :] + nxt * scw[2:3, :])

    inv_sqrt2 = 1.0 / math.sqrt(2.0)

    def gelu(t):
        return 0.5 * t * (1.0 + lax.erf(t * inv_sqrt2))

    u = gelu(gu_ref[...].astype(F32))
    v = gelu(gv_ref[...].astype(F32))
    mu = jnp.mean(v, axis=-1, keepdims=True)
    vc = v - mu
    v = vc * lax.rsqrt(jnp.mean(vc * vc, axis=-1, keepdims=True) + LN_EPS) * lnw_ref[...] + lnb_ref[...]
    vb = v.astype(BF16)
    gw = SG_WIDTH // SG_GROUPS
    chunks = []
    for c in range(tm // SG_CHUNK):
        groups = []
        for g in range(SG_GROUPS):
            groups.append(_dot(sgw_ref[g], vb[c * SG_CHUNK:(c + 1) * SG_CHUNK, g * gw:(g + 1) * gw]))
        chunks.append(jnp.concatenate(groups, axis=1) + sgb_ref[...])
    y2 = u * jnp.concatenate(chunks, axis=0)

    ys = (y0_ref[...], y1.astype(BF16), y2.astype(BF16), y3_ref[...])
    gates = (mg0_ref, mg1_ref, mg2_ref, mg3_ref)
    m = None
    for br in range(N_BRANCH):
        term = jax.nn.sigmoid(gates[br][...].astype(F32)) * _dot(ys[br], wb_ref[br])
        m = term if m is None else m + term
    y = _dot(m.astype(BF16), wo_ref[...])
    x_new = x_ref[...] + mod_ref[2:3, :] * y
    xo_ref[...] = x_new

    ms = jnp.mean(x_new * x_new, axis=-1, keepdims=True)
    h2 = x_new * lax.rsqrt(ms + NORM_EPS) * n2_ref[...] * (1.0 + mod_ref[4:5, :]) + mod_ref[3:4, :]
    half = D_MODEL // 2
    _store_planes(h2_ref, _pack_bf16_pair(h2[:, :half], h2[:, half:]))

    h2_hi = h2.astype(BF16)
    h2_lo = (h2 - h2_hi.astype(F32)).astype(BF16)
    prod = _dot(jnp.concatenate([h2_hi, h2_lo], axis=1), rw_ref[...])
    lane = lax.broadcasted_iota(I32, (tm, LANES), 1)
    logits = prod + pltpu.roll(prod, LANES - N_EXPERTS, 1)
    logits = jnp.where(lane < N_EXPERTS, logits, NEG_INF) + rb_ref[...]
    lane_f = lane.astype(F32)
    work = logits
    topv = jnp.full((tm, LANES), NEG_INF, F32)
    topi = jnp.zeros((tm, LANES), I32)
    onehot = jnp.zeros((tm, LANES), F32)
    firsts = []
    for k in range(TOP_K):
        mx = jnp.max(work, axis=-1, keepdims=True)
        first_f = jnp.min(jnp.where(work == mx, lane_f, float(LANES)), axis=-1, keepdims=True)
        first = first_f.astype(I32)
        hit = lane == first
        topv = jnp.where(lane == k, mx, topv)
        topi = jnp.where(lane == k, first, topi)
        onehot = jnp.where(hit, 1.0, onehot)
        work = jnp.where(hit, -jnp.inf, work)
        firsts.append(first)
    e = jnp.where(lane < TOP_K, jnp.exp(topv - jnp.max(topv, axis=-1, keepdims=True)), 0.0)
    gate = e / jnp.sum(e, axis=-1, keepdims=True)

    @pl.when(i == 0)
    def _init():
        carry[...] = jnp.zeros_like(carry)

    tri = (lax.broadcasted_iota(I32, (tm, tm), 0) > lax.broadcasted_iota(I32, (tm, tm), 1)).astype(BF16)
    before = carry[0:1, :] + _dot(tri, onehot.astype(BF16))
    rank = jnp.zeros((tm, LANES), F32)
    for k in range(TOP_K):
        r_k = jnp.sum(jnp.where(lane == firsts[k], before, 0.0), axis=-1, keepdims=True)
        rank = jnp.where(lane == k, r_k, rank)
    new_carry = carry[0:1, :] + jnp.sum(onehot, axis=0, keepdims=True)
    carry[...] = jnp.broadcast_to(new_carry, carry.shape)
    cnt_ref[...] = jnp.broadcast_to(new_carry, cnt_ref.shape).astype(I32)
    idx_ref[...] = topi[:, :SUBLANES]
    gate_ref[...] = gate[:, :SUBLANES]
    rank_ref[...] = rank[:, :SUBLANES].astype(I32)


def _merge(x, mod, norm2, y0, y3, p, w, l, geo):
    t, dm = x.shape
    tm = geo["tm_merge"]
    mod_row = geo["mod_row"]
    kern = functools.partial(_merge_kernel, tm=tm, tc=geo["tc"], n_ctx=geo["ctx"], n_lat=geo["lat"])
    halo = tm // SUBLANES
    n_halo = t // SUBLANES

    def col(c, width=512):
        return pl.BlockSpec((tm, width), lambda i: (i, c // width))

    def prev_halo(c):
        return pl.BlockSpec((SUBLANES, 512), lambda i: (jnp.maximum(i * halo - 1, 0), c // 512))

    def next_halo(c):
        return pl.BlockSpec((SUBLANES, 512), lambda i: (jnp.minimum((i + 1) * halo, n_halo - 1), c // 512))

    def layer(shape):
        nd = len(shape)
        return pl.BlockSpec((None,) + shape, lambda i: (l,) + (0,) * nd)

    in_specs = [
        pl.BlockSpec((tm, dm), lambda i: (i, 0)),
        pl.BlockSpec((None, None, N_MOD, dm), lambda i: (l, mod_row(i * tm), 0, 0)),
        layer((1, dm)),
        pl.BlockSpec((tm, 512), lambda i: (i, 0)),
        pl.BlockSpec((tm, 512), lambda i: (i, 0)),
        col(COL_SB), col(COL_SC), col(COL_SX),
        prev_halo(COL_SC), prev_halo(COL_SX), next_halo(COL_SC), next_halo(COL_SX),
        col(COL_GU), col(COL_GV),
        col(COL_MG, 1024), col(COL_MG + 1024, 1024), col(COL_MG + 2048, 1024), col(COL_MG + 3072, 1024),
        layer((SC_CONV, SC_WIDTH)), layer((1, SG_WIDTH)), layer((1, SG_WIDTH)),
        layer((SG_GROUPS, SG_CHUNK, SG_CHUNK)), layer((SG_CHUNK, SG_WIDTH)),
        layer((N_BRANCH, BRANCH_W, dm)), layer((dm, dm)), layer((2 * dm, LANES)), layer((1, LANES)),
    ]
    out_shape = (
        jax.ShapeDtypeStruct((t, dm), F32),
        jax.ShapeDtypeStruct((SC_PLANES, t, SC_ROW), U32),
        jax.ShapeDtypeStruct((t, SUBLANES), I32),
        jax.ShapeDtypeStruct((t, SUBLANES), F32),
        jax.ShapeDtypeStruct((t, SUBLANES), I32),
        jax.ShapeDtypeStruct((SUBLANES, LANES), I32),
    )
    out_specs = (
        pl.BlockSpec((tm, dm), lambda i: (i, 0)),
        pl.BlockSpec((SC_PLANES, tm, SC_ROW), lambda i: (0, i, 0)),
        pl.BlockSpec((tm, SUBLANES), lambda i: (i, 0)),
        pl.BlockSpec((tm, SUBLANES), lambda i: (i, 0)),
        pl.BlockSpec((tm, SUBLANES), lambda i: (i, 0)),
        pl.BlockSpec((SUBLANES, LANES), lambda i: (0, 0)),
    )
    return pl.pallas_call(
        kern,
        out_shape=out_shape,
        grid=(t // tm,),
        in_specs=in_specs,
        out_specs=out_specs,
        scratch_shapes=[pltpu.VMEM((SUBLANES, LANES), F32)],
        compiler_params=_cparams(("arbitrary",)),
        name="merge",
    )(x, mod, norm2, y0, y3, p, p, p, p, p, p, p, p, p, p, p, p, p,
      w["sc_conv"], w["sg_ln_w"], w["sg_ln_b"], w["sg_w"], w["sg_b"], w["w_branch"], w["w_out"],
      w["router_w"], w["router_b"])


def _sc_mesh():
    return plsc.VectorSubcoreMesh(core_axis_name="core", subcore_axis_name="subcore")


def _sc_scatter_rows(src, idx, n_out):
    n, width = src.shape

    @pl.kernel(out_type=jax.ShapeDtypeStruct((n_out, width), U32), mesh=_sc_mesh(), scratch_types=[])
    def scatter(x_hbm, i_hbm, o_hbm):
        def body(x_vmem, *i_vmem):
            for iv in i_vmem:
                pltpu.sync_copy(x_vmem, o_hbm.at[iv.at[0]])

        pltpu.emit_pipeline(
            body,
            grid=(n // SC_WINDOW,),
            in_specs=[pl.BlockSpec((SC_WINDOW, width), lambda i: (i, 0))]
            + [pl.BlockSpec((1, SC_WINDOW), functools.partial(lambda i, k: (k, i), k=k)) for k in range(TOP_K)],
            out_specs=[],
            core_axis_name=("core", "subcore"),
            dimension_semantics=(pltpu.PARALLEL,),
        )(x_hbm, *([i_hbm] * TOP_K))

    return scatter(src, idx)


def _sc_gather_rows(src, idx):
    n = idx.shape[0]
    width = src.shape[1]

    @pl.kernel(out_type=jax.ShapeDtypeStruct((n, width), U32), mesh=_sc_mesh(), scratch_types=[])
    def gather(x_hbm, i_hbm, o_hbm):
        def body(i_vmem, o_vmem):
            pltpu.sync_copy(x_hbm.at[i_vmem.at[0]], o_vmem)

        pltpu.emit_pipeline(
            body,
            grid=(n // SC_WINDOW,),
            in_specs=[pl.BlockSpec((1, SC_WINDOW), lambda i: (0, i))],
            out_specs=[pl.BlockSpec((SC_WINDOW, width), lambda i: (i, 0))],
            core_axis_name=("core", "subcore"),
            dimension_semantics=(pltpu.PARALLEL,),
        )(i_hbm, o_hbm)

    return gather(src, idx.reshape(1, n))


def _expert_kernel(be_ref, bv_ref, xs_ref, w1_ref, b1_ref, w2_ref, b2_ref, ys_ref):
    del be_ref
    valid = lax.broadcasted_iota(I32, (MOE_BM, 1), 0) < bv_ref[pl.program_id(0)]
    lo, hi = _unpack_bf16_pair(jnp.where(valid, _load_planes(xs_ref), jnp.uint32(0)))
    half = D_MODEL // 2
    h = (_dot(lo.astype(BF16), w1_ref[0:half, :]) + _dot(hi.astype(BF16), w1_ref[half:, :]) + b1_ref[...])
    hg = W1_GROUP // 2
    acts = []
    for g in range(2 * D_EXPERT // W1_GROUP):
        glu = jnp.minimum(h[:, g * W1_GROUP:g * W1_GROUP + hg], SWIGLU_LIMIT)
        lin = jnp.clip(h[:, g * W1_GROUP + hg:(g + 1) * W1_GROUP], -SWIGLU_LIMIT, SWIGLU_LIMIT)
        acts.append((glu * jax.nn.sigmoid(SWIGLU_ALPHA * glu) * (lin + 1.0)).astype(BF16))
    y = _dot(jnp.concatenate(acts, axis=1), w2_ref[...]) + b2_ref[...]
    _store_planes(ys_ref, _pack_bf16_pair(y[:, :half], y[:, half:]))


def _w1_regroup_kernel(w_ref, o_ref):
    hg = W1_GROUP // 2
    r = lax.broadcasted_iota(I32, (W1_GROUP, W1_GROUP), 0)
    c = lax.broadcasted_iota(I32, (W1_GROUP, W1_GROUP), 1)
    perm = (r == jnp.where(c < hg, 2 * c, 2 * (c - hg) + 1)).astype(BF16)
    for g in range(2 * D_EXPERT // W1_GROUP):
        sl = slice(g * W1_GROUP, (g + 1) * W1_GROUP)
        o_ref[:, sl] = _dot(w_ref[:, sl].astype(BF16), perm).astype(BF16)


def _w1_regroup(exp_w1):
    depth, ne, dm, dh = exp_w1.shape
    tr = 512
    out = pl.pallas_call(
        _w1_regroup_kernel,
        out_shape=jax.ShapeDtypeStruct((depth * ne, dm, dh), BF16),
        grid=(depth * ne, dm // tr),
        in_specs=[pl.BlockSpec((None, tr, dh), lambda e, i: (e, i, 0))],
        out_specs=pl.BlockSpec((None, tr, dh), lambda e, i: (e, i, 0)),
        compiler_params=_cparams(("arbitrary", "arbitrary")),
        name="w1_regroup",
    )(exp_w1.reshape(depth * ne, dm, dh))
    return out.reshape(depth, ne, dm, dh)


def _experts(xs, block_expert, block_valid, w1p, b1p, w2, b2, l):
    n_slots = xs.shape[1]
    bm = MOE_BM
    grid_spec = pltpu.PrefetchScalarGridSpec(
        num_scalar_prefetch=2,
        grid=(n_slots // bm,),
        in_specs=[
            pl.BlockSpec((SC_PLANES, bm, SC_ROW), lambda i, be, bv: (0, i, 0)),
            pl.BlockSpec((None, None, D_MODEL, 2 * D_EXPERT), lambda i, be, bv: (l, be[i], 0, 0)),
            pl.BlockSpec((None, None, 1, 2 * D_EXPERT), lambda i, be, bv: (l, be[i], 0, 0)),
            pl.BlockSpec((None, None, D_EXPERT, D_MODEL), lambda i, be, bv: (l, be[i], 0, 0)),
            pl.BlockSpec((None, None, 1, D_MODEL), lambda i, be, bv: (l, be[i], 0, 0)),
        ],
        out_specs=pl.BlockSpec((SC_PLANES, bm, SC_ROW), lambda i, be, bv: (0, i, 0)),
    )
    return pl.pallas_call(
        _expert_kernel,
        out_shape=jax.ShapeDtypeStruct((SC_PLANES, n_slots, SC_ROW), U32),
        grid_spec=grid_spec,
        compiler_params=_cparams(("arbitrary",)),
        name="moe_experts",
    )(block_expert, block_valid, xs, w1p, b1p, w2, b2)


def _combine_kernel(gate_ref, x_ref, mod_ref, yg_ref, xo_ref, *, tm):
    gate = gate_ref[...]
    half = D_MODEL // 2
    f_lo = jnp.zeros((tm, half), F32)
    f_hi = jnp.zeros((tm, half), F32)
    for k in range(TOP_K):
        lo, hi = _unpack_bf16_pair(_load_planes(yg_ref, k * SC_PLANES))
        g = gate[:, k:k + 1]
        f_lo = f_lo + g * lo
        f_hi = f_hi + g * hi
    g2 = mod_ref[5:6, :]
    xo_ref[:, :half] = x_ref[:, :half] + g2[:, :half] * f_lo
    xo_ref[:, half:] = x_ref[:, half:] + g2[:, half:] * f_hi


def _combine(x, mod, gate, yg, l, geo):
    t, dm = x.shape
    tm = geo["tm_moe"]
    mod_row = geo["mod_row"]
    return pl.pallas_call(
        functools.partial(_combine_kernel, tm=tm),
        out_shape=jax.ShapeDtypeStruct((t, dm), F32),
        grid=(t // tm,),
        in_specs=[
            pl.BlockSpec((tm, SUBLANES), lambda i: (i, 0)),
            pl.BlockSpec((tm, dm), lambda i: (i, 0)),
            pl.BlockSpec((None, None, N_MOD, dm), lambda i: (l, mod_row(i * tm), 0, 0)),
            pl.BlockSpec((TOP_K * SC_PLANES, tm, SC_ROW), lambda i: (0, i, 0)),
        ],
        out_specs=pl.BlockSpec((tm, dm), lambda i: (i, 0)),
        compiler_params=_cparams(("arbitrary",)),
        name="moe_combine",
    )(gate, x, mod, yg)


def _moe_plan(idx, rank, counts, n_blocks):
    bm = MOE_BM
    padded = (counts + bm - 1) // bm * bm
    pad_end = jnp.cumsum(padded)
    pad_start = pad_end - padded
    expert = idx[:, :TOP_K]
    onehot = expert[:, :, None] == jnp.arange(N_EXPERTS, dtype=I32)[None, None, :]
    dest = rank[:, :TOP_K] + jnp.sum(jnp.where(onehot, pad_start[None, None, :], 0), axis=-1)
    block_start = jnp.arange(n_blocks, dtype=I32) * bm
    block_expert = jnp.minimum(
        jnp.sum((block_start[:, None] >= pad_end[None, :]).astype(I32), axis=-1), N_EXPERTS - 1)
    block_valid = jnp.clip(counts[block_expert] - (block_start - pad_start[block_expert]), 0, bm)
    plane_off = jnp.arange(SC_PLANES, dtype=I32) * (n_blocks * bm)
    dest_rows = jnp.transpose(dest.astype(I32))[:, None, :] + plane_off[None, :, None]
    scatter_idx = dest_rows.reshape(TOP_K, -1)
    gather_idx = dest_rows.reshape(-1)
    return scatter_idx, gather_idx, block_expert.astype(I32), block_valid.astype(I32)


def _rope_tables(n_lat):
    rows = n_lat // GRID_W
    row = jnp.repeat(jnp.arange(rows, dtype=F32), GRID_W)
    colp = jnp.tile(jnp.arange(GRID_W, dtype=F32), rows)
    n_freq = HEAD_DIM // 4
    inv = ROPE_BASE ** (-jnp.arange(n_freq, dtype=F32) / n_freq)
    ang = jnp.concatenate([row[:, None] * inv, colp[:, None] * inv], axis=-1)
    cos, sin = jnp.cos(ang), jnp.sin(ang)
    cr, cc = cos[:, :n_freq], cos[:, n_freq:]
    sr, sc = sin[:, :n_freq], sin[:, n_freq:]
    cos64 = jnp.concatenate([cr, cr, cc, cc], axis=-1)
    sin64 = jnp.concatenate([-sr, sr, -sc, sc], axis=-1)
    return jnp.tile(cos64, (1, 2)), jnp.tile(sin64, (1, 2))


def _prep_weights(w_in, w_branch, w_out, router_w, router_b, sg_b, exp_w1, exp_b1, exp_w2, exp_b2,
                  attn_q_norm, attn_k_norm):
    depth, dm, _ = w_in.shape
    qkvz = 2048
    ab0, sc0, sg0, at0, mg0 = 2048, 2064, 3600, 4624, 5392
    pad = PROJ_COLS - (COL_AB + 16)
    w_p = jnp.concatenate([
        w_in[:, :, 0:qkvz],
        w_in[:, :, sc0:sg0],
        w_in[:, :, sg0:at0],
        w_in[:, :, at0:at0 + 512],
        w_in[:, :, mg0:mg0 + 4096],
        w_in[:, :, at0 + 512:at0 + 768],
        w_in[:, :, ab0:ab0 + 16],
        jnp.zeros((depth, dm, pad), w_in.dtype),
    ], axis=-1).astype(BF16)
    rw_hi = router_w.astype(BF16)
    rw_lo = (router_w - rw_hi.astype(F32)).astype(BF16)
    rw = jnp.concatenate([
        jnp.concatenate([rw_hi, rw_lo, jnp.zeros((depth, dm, LANES - 2 * N_EXPERTS), BF16)], axis=-1),
        jnp.concatenate([rw_hi, jnp.zeros((depth, dm, LANES - N_EXPERTS), BF16)], axis=-1),
    ], axis=1)
    rb = jnp.concatenate([router_b, jnp.full((depth, LANES - N_EXPERTS), NEG_INF, F32)], axis=-1)
    w1p = _w1_regroup(exp_w1)
    ne = exp_b1.shape[1]
    b1p = jnp.swapaxes(exp_b1.reshape(depth, ne, -1, W1_GROUP // 2, 2), -1, -2).reshape(depth, ne, -1)
    return {
        "w_in": w_p,
        "w_branch": w_branch.astype(BF16),
        "w_out": w_out.astype(BF16),
        "router_w": rw,
        "router_b": rb[:, None, :],
        "sg_b": jnp.repeat(jnp.swapaxes(sg_b, 1, 2), SG_WIDTH // SG_GROUPS, axis=2),
        "w1": w1p,
        "b1": b1p[:, :, None, :],
        "w2": exp_w2.astype(BF16),
        "b2": exp_b2[:, :, None, :],
        "q_norm": jnp.tile(attn_q_norm, (1, 2))[:, None, :],
        "k_norm": jnp.tile(attn_k_norm, (1, 2))[:, None, :],
    }


def kernel(x, c, ctx, c_ctx, norm1, norm2, w_mod, b_mod, w_in, gdn_conv, gdn_a_log, gdn_dt_bias, gdn_out_norm,
           sc_conv, sg_ln_w, sg_ln_b, sg_w, sg_b, attn_q_norm, attn_k_norm, attn_sink, w_branch, w_out,
           router_w, router_b, exp_w1, exp_b1, exp_w2, exp_b2):
    b, n_lat, dm = x.shape
    n_ctx = ctx.shape[1]
    depth = w_in.shape[0]
    tc = b * n_ctx
    t = tc + b * n_lat
    assert dm == D_MODEL and tc % n_lat == 0 and n_lat % 256 == 0 and n_ctx % 256 == 0
    tile_cap = math.gcd(tc, n_lat)

    def mod_row(r0):
        return jnp.where(r0 < tc, 0, 1 + (r0 - tc) // n_lat)

    geo = {
        "b": b, "ctx": n_ctx, "lat": n_lat, "tc": tc, "mod_row": mod_row,
        "tm_proj": min(1024, tile_cap), "tm_merge": min(512, tile_cap), "tm_moe": min(512, tile_cap),
    }

    wts = _prep_weights(w_in, w_branch, w_out, router_w, router_b, sg_b, exp_w1, exp_b1, exp_w2, exp_b2,
                        attn_q_norm, attn_k_norm)
    rope_cos, rope_sin = _rope_tables(n_lat)

    mod_rows = -(-(1 + b) // SUBLANES) * SUBLANES
    c_all = jnp.concatenate([c_ctx[None, :], c, jnp.zeros((mod_rows - 1 - b, dm), F32)], axis=0)
    mod = _modulation(c_all, w_mod, b_mod).reshape(depth, mod_rows, N_MOD, dm)

    xs_flat = jnp.concatenate([ctx.reshape(tc, dm), x.reshape(b * n_lat, dm)], axis=0)
    n_blocks = -(-(t * TOP_K) // MOE_BM) + N_EXPERTS
    n_slots = n_blocks * MOE_BM

    for l in range(depth):
        last = l == depth - 1
        p = _in_proj(xs_flat, mod, norm1[:, None, :], wts["w_in"], l, geo)
        g_lat, g_ctx = _gdn(p, gdn_conv, gdn_a_log, gdn_dt_bias, gdn_out_norm[:, None, :], l, geo)
        a_lat = _attention(p, attn_sink, wts["q_norm"], wts["k_norm"], rope_cos, rope_sin, l, geo, False)
        if last:
            a_ctx = jnp.zeros((tc, ATT_HEADS * HEAD_DIM), BF16)
        else:
            a_ctx = _attention(p, attn_sink, wts["q_norm"], wts["k_norm"], rope_cos, rope_sin, l, geo, True)
        y0 = jnp.concatenate([g_ctx, g_lat], axis=0)
        y3 = jnp.concatenate([a_ctx, a_lat], axis=0)
        layer_w = {
            "sc_conv": sc_conv, "sg_ln_w": sg_ln_w[:, None, :], "sg_ln_b": sg_ln_b[:, None, :], "sg_w": sg_w.astype(BF16),
            "sg_b": wts["sg_b"], "w_branch": wts["w_branch"], "w_out": wts["w_out"],
            "router_w": wts["router_w"], "router_b": wts["router_b"],
        }
        x_mid, h2p, idx, gate, rank, counts = _merge(xs_flat, mod, norm2[:, None, :], y0, y3, p, layer_w, l, geo)
        scatter_idx, gather_idx, block_expert, block_valid = _moe_plan(idx, rank, counts[0, :N_EXPERTS], n_blocks)
        xs_sorted = _sc_scatter_rows(h2p.reshape(SC_PLANES * t, SC_ROW), scatter_idx, SC_PLANES * n_slots)
        ys_sorted = _experts(xs_sorted.reshape(SC_PLANES, n_slots, SC_ROW), block_expert, block_valid,
                             wts["w1"], wts["b1"], wts["w2"], wts["b2"], l)
        yg = _sc_gather_rows(ys_sorted.reshape(SC_PLANES * n_slots, SC_ROW), gather_idx)
        xs_flat = _combine(x_mid, mod, gate, yg.reshape(TOP_K * SC_PLANES, t, SC_ROW), l, geo)

    return xs_flat[tc:].reshape(b, n_lat, dm)
```

```python
import functools
import math

import jax
import jax.numpy as jnp
from jax import lax
from jax.experimental import pallas as pl
from jax.experimental.pallas import tpu as pltpu
from jax.experimental.pallas import tpu_sc as plsc

F32 = jnp.float32
BF16 = jnp.bfloat16
I32 = jnp.int32
U32 = jnp.uint32
HIGHEST = lax.Precision.HIGHEST

D_MODEL = 1024
GRID_W = 64
N_MOD = 6
GDN_HEADS = 4
GDN_DK = 128
GDN_DV = 128
GDN_CONV = 5
SC_WIDTH = 512
SC_CONV = 3
SG_CHUNK = 128
SG_GROUPS = 4
SG_WIDTH = 512
ATT_HEADS = 8
ATT_KV_HEADS = 2
HEAD_DIM = 64
WINDOW = 128
ROPE_BASE = 10000.0
N_BRANCH = 4
BRANCH_W = 512
N_EXPERTS = 32
TOP_K = 4
D_EXPERT = 1024
SWIGLU_ALPHA = 1.702
SWIGLU_LIMIT = 7.0
NORM_EPS = 1e-6
LN_EPS = 1e-5
NEG_INF = -1e30

LANES = 128
SUBLANES = 8
VMEM_LIMIT_BYTES = 56 * 2**20

COL_Q = 0
COL_K = 512
COL_V = 1024
COL_Z = 1536
COL_SB = 2048
COL_SC = 2560
COL_SX = 3072
COL_GU = 3584
COL_GV = 4096
COL_AQ = 4608
COL_MG = 5120
COL_AK = 9216
COL_AV = 9344
COL_AB = 9472
PROJ_COLS = 10240
PROJ_TN = 2048
PROJ_CHUNK = 512

GDN_CHUNK = 128
GDN_SOLVE_BLOCK = 64
GDN_PREP_UNROLL = 4
MOE_BM = 512
W1_GROUP = 256
SC_ROW = 256
SC_WINDOW = 128
SC_PLANES = (D_MODEL // 2) // SC_ROW


def _cparams(sem):
    return pltpu.CompilerParams(dimension_semantics=sem, vmem_limit_bytes=VMEM_LIMIT_BYTES)


def _silu(x):
    return x * jax.nn.sigmoid(x)


def _dot(a, b, precision=None):
    return jnp.dot(a, b, preferred_element_type=F32, precision=precision)


def _dot_nt(a, b):
    return lax.dot_general(a, b, (((1,), (1,)), ((), ())), preferred_element_type=F32)


def _dot_tn(a, b):
    return lax.dot_general(a, b, (((0,), (0,)), ((), ())), preferred_element_type=F32)


def _aligned(start, multiple):
    return start if isinstance(start, int) else pl.multiple_of(start, multiple)


def _pack_bf16_pair(lo, hi):
    lo_b = pltpu.bitcast(lo.astype(BF16).astype(F32), U32)
    hi_b = pltpu.bitcast(hi.astype(BF16).astype(F32), U32)
    return (hi_b & jnp.uint32(0xFFFF0000)) | (lo_b >> 16)


def _unpack_bf16_pair(u):
    lo = pltpu.bitcast(u << 16, F32)
    hi = pltpu.bitcast(u & jnp.uint32(0xFFFF0000), F32)
    return lo, hi


def _store_planes(ref, packed):
    for p in range(SC_PLANES):
        ref[p] = packed[:, p * SC_ROW:(p + 1) * SC_ROW]


def _load_planes(ref, first=0):
    return jnp.concatenate([ref[first + p] for p in range(SC_PLANES)], axis=1)


def _mod_kernel(c_ref, w_ref, b_ref, o_ref):
    s = _silu(c_ref[...])
    o_ref[...] = _dot(s, w_ref[...], precision=HIGHEST) + b_ref[...]


def _modulation(c_all, w_mod, b_mod):
    depth, dm, nm = w_mod.shape
    rows = c_all.shape[0]
    tn = 1536
    return pl.pallas_call(
        _mod_kernel,
        out_shape=jax.ShapeDtypeStruct((depth, rows, nm), F32),
        grid=(depth, nm // tn),
        in_specs=[
            pl.BlockSpec((rows, dm), lambda l, j: (0, 0)),
            pl.BlockSpec((None, dm, tn), lambda l, j: (l, 0, j)),
            pl.BlockSpec((None, 1, tn), lambda l, j: (l, 0, j)),
        ],
        out_specs=pl.BlockSpec((None, rows, tn), lambda l, j: (l, 0, j)),
        compiler_params=_cparams(("arbitrary", "arbitrary")),
        name="modulation",
    )(c_all, w_mod, b_mod.reshape(depth, 1, nm))


def _in_proj_kernel(x_ref, mod_ref, nw_ref, w_ref, o_ref):
    x = x_ref[...]
    ms = jnp.mean(x * x, axis=-1, keepdims=True)
    y = x * lax.rsqrt(ms + NORM_EPS) * nw_ref[...]
    h = (y * (1.0 + mod_ref[1:2, :]) + mod_ref[0:1, :]).astype(BF16)
    for c in range(PROJ_TN // PROJ_CHUNK):
        sl = slice(c * PROJ_CHUNK, (c + 1) * PROJ_CHUNK)
        o_ref[:, sl] = _dot(h, w_ref[:, sl]).astype(BF16)


def _in_proj(x, mod, norm_w, w_p, l, geo):
    t, dm = x.shape
    tm = geo["tm_proj"]
    mod_row = geo["mod_row"]
    return pl.pallas_call(
        _in_proj_kernel,
        out_shape=jax.ShapeDtypeStruct((t, PROJ_COLS), BF16),
        grid=(PROJ_COLS // PROJ_TN, t // tm),
        in_specs=[
            pl.BlockSpec((tm, dm), lambda j, i: (i, 0)),
            pl.BlockSpec((None, None, N_MOD, dm), lambda j, i: (l, mod_row(i * tm), 0, 0)),
            pl.BlockSpec((None, 1, dm), lambda j, i: (l, 0, 0)),
            pl.BlockSpec((None, dm, PROJ_TN), lambda j, i: (l, 0, j)),
        ],
        out_specs=pl.BlockSpec((tm, PROJ_TN), lambda j, i: (i, j)),
        compiler_params=_cparams(("arbitrary", "arbitrary")),
        name="in_proj",
    )(x, mod, norm_w, w_p)


def _softplus(x):
    return jnp.maximum(x, 0.0) + jnp.log1p(jnp.exp(-jnp.abs(x)))


def _gdn_kernel(alog_ref, dtb_ref,
                ql_ref, kl_ref, vl_ref, zl_ref, abl_ref,
                qc_ref, kc_ref, vc_ref, zc_ref, abc_ref,
                cwq_ref, cwk_ref, cwv_ref, onorm_ref,
                yl_ref, yc_ref,
                xpad, qs, ks, vs, gsc, bsc,
                u_sc, w16, qk16, qd16, kdk16, glast, oacc,
                *, n_ctx, n_lat):
    cs = GDN_CHUNK
    hd = GDN_DK
    n_all = n_ctx + n_lat
    n_chunks = n_all // cs
    n_ctx_chunks = n_ctx // cs
    h = pl.program_id(1)
    pad = SUBLANES
    lat_off = n_ctx + 3 * pad

    zeros_pad = jnp.zeros((pad, hd), F32)
    xpad[0:pad, :] = zeros_pad
    xpad[pad + n_ctx:pad + n_ctx + 2 * pad, :] = jnp.zeros((2 * pad, hd), F32)
    xpad[lat_off + n_lat:lat_off + n_lat + pad, :] = zeros_pad

    half = GDN_CONV // 2
    tile = 256

    def conv_into(src_c, src_l, cw_ref, dst, mode):
        xpad[pad:pad + n_ctx, :] = src_c[...].astype(F32)
        xpad[lat_off:lat_off + n_lat, :] = src_l[...].astype(F32)
        w = cw_ref[...]
        for seg_off, dst_off, seg_len in ((pad, 0, n_ctx), (lat_off, n_ctx, n_lat)):
            for r0 in range(0, seg_len, tile):
                acc = jnp.zeros((tile, hd), F32)
                for j in range(GDN_CONV):
                    s = seg_off + r0 + j - half
                    acc = acc + xpad[s:s + tile, :] * w[j:j + 1, :]
                y = _silu(acc)
                if mode != "v":
                    y = y * lax.rsqrt(jnp.sum(y * y, axis=-1, keepdims=True) + NORM_EPS)
                if mode == "q":
                    y = y * (hd ** -0.5)
                dst[dst_off + r0:dst_off + r0 + tile, :] = y

    conv_into(qc_ref, ql_ref, cwq_ref, qs, "q")
    conv_into(kc_ref, kl_ref, cwk_ref, ks, "k")
    conv_into(vc_ref, vl_ref, cwv_ref, vs, "v")

    sel_r = lax.broadcasted_iota(I32, (LANES, LANES), 0)
    pos = lax.broadcasted_iota(I32, (tile, LANES), 0) % cs
    for d in range(2):
        col = d * GDN_HEADS + h
        sel_a = (sel_r == col).astype(BF16)
        sel_b = (sel_r == (2 * GDN_HEADS + col)).astype(BF16)
        neg_a = -jnp.exp(jnp.full((1, LANES), alog_ref[d, h], F32))
        dtb = dtb_ref[d, h]
        for src, r_off, r_len in ((abc_ref, 0, n_ctx), (abl_ref, n_ctx, n_lat)):
            for r0 in range(0, r_len, tile):
                ab = src[r0:r0 + tile, :]
                a_b = _dot(ab, sel_a)
                b_b = _dot(ab, sel_b)
                gc = neg_a * _softplus(a_b + dtb)
                s = 1
                while s < cs:
                    if d == 0:
                        gc = gc + jnp.where(pos >= s, pltpu.roll(gc, s, 0), 0.0)
                    else:
                        gc = gc + jnp.where(pos < cs - s, pltpu.roll(gc, tile - s, 0), 0.0)
                    s *= 2
                gsc[d, r_off + r0:r_off + r0 + tile, :] = gc
                bsc[d, r_off + r0:r_off + r0 + tile, :] = jax.nn.sigmoid(b_b)

    ri = lax.broadcasted_iota(I32, (cs, cs), 0)
    ci = lax.broadcasted_iota(I32, (cs, cs), 1)
    low = (ri >= ci)
    up = (ri <= ci)
    eye = (ri == ci).astype(F32)
    offdiag = (ri != ci).astype(F32)
    assert cs in (GDN_SOLVE_BLOCK, 2 * GDN_SOLVE_BLOCK)
    same_blk = ((ri // GDN_SOLVE_BLOCK) == (ci // GDN_SOLVE_BLOCK)).astype(F32)
    n_levels = int(math.log2(min(cs, GDN_SOLVE_BLOCK))) - 1

    def prep_group(first_chunk, count):
        chunk_ids = [first_chunk + uu for uu in range(count)]
        rows = [pl.ds(_aligned(c * cs, cs), cs) for c in chunk_ids]
        q = [qs[r, :] for r in rows]
        k = [ks[r, :] for r in rows]
        v = [vs[r, :] for r in rows]
        kb = [x.astype(BF16) for x in k]
        kkt = [_dot_nt(x, x) for x in kb]
        qkt = [_dot_nt(a.astype(BF16), b) for a, b in zip(q, kb)]
        probs = [(ui, d) for ui in range(count) for d in range(2)]
        gc_col, beta_b, dmat, t, p, n_off = {}, {}, {}, {}, {}, {}
        for ui, d in probs:
            gc = gsc[d, rows[ui], :]
            bt = bsc[d, rows[ui], :]
            gsq = gc[:, :cs]
            dm = jnp.where(low if d == 0 else up, jnp.exp(gsq - gsq.T), 0.0)
            n = -(kkt[ui] * bt[:, :cs] * dm * offdiag)
            gc_col[ui, d], beta_b[ui, d], dmat[ui, d] = gc, bt, dm
            n_diag = n * same_blk
            t[ui, d] = eye + n_diag
            p[ui, d] = n_diag.astype(BF16)
            n_off[ui, d] = (n - n_diag).astype(BF16)
        p = {key: _dot(x, x).astype(BF16) for key, x in p.items()}
        for lvl in range(n_levels):
            t_next = {key: t[key] + _dot(p[key], t[key].astype(BF16)) for key in probs}
            if lvl + 1 < n_levels:
                p = {key: _dot(x, x).astype(BF16) for key, x in p.items()}
            t = t_next
        tb = {key: x.astype(BF16) for key, x in t.items()}
        if cs > GDN_SOLVE_BLOCK:
            x_off = {key: _dot(tb[key], n_off[key]).astype(BF16) for key in probs}
            t = {key: t[key] + _dot(x_off[key], tb[key]) for key in probs}
            tb = {key: x.astype(BF16) for key, x in t.items()}
        eg = {key: jnp.exp(x) for key, x in gc_col.items()}
        u = {(ui, d): _dot(tb[ui, d], (v[ui] * beta_b[ui, d]).astype(BF16)) for ui, d in probs}
        w = {(ui, d): _dot(tb[ui, d], (k[ui] * beta_b[ui, d] * eg[ui, d]).astype(BF16)) for ui, d in probs}
        for ui, d in probs:
            r = rows[ui]
            last = cs - 1 if d == 0 else 0
            gl = gc_col[ui, d][last:last + 1, :]
            u_sc[d, r, :] = u[ui, d]
            w16[d, r, :] = w[ui, d].astype(BF16)
            qk16[d, r, :] = (qkt[ui] * dmat[ui, d]).astype(BF16)
            qd16[d, r, :] = (q[ui] * eg[ui, d]).astype(BF16)
            kdk16[d, r, :] = (k[ui] * jnp.exp(gl - gc_col[ui, d])).astype(BF16)
            g0 = _aligned(chunk_ids[ui] * SUBLANES, SUBLANES)
            glast[d, pl.ds(g0, SUBLANES), :] = jnp.broadcast_to(jnp.exp(gl), (SUBLANES, LANES))

    def prep(gi, carry):
        prep_group(gi * GDN_PREP_UNROLL, GDN_PREP_UNROLL)
        return carry

    n_groups = n_chunks // GDN_PREP_UNROLL
    lax.fori_loop(0, n_groups, prep, 0)
    if n_chunks % GDN_PREP_UNROLL:
        prep_group(n_groups * GDN_PREP_UNROLL, n_chunks % GDN_PREP_UNROLL)

    oacc[...] = jnp.zeros_like(oacc)

    def step(i, carry):
        c_b = jnp.where(i < n_ctx_chunks, n_ctx_chunks - 1 - i, n_chunks + n_ctx_chunks - 1 - i)
        cid = (i, c_b)
        rows = [pl.ds(pl.multiple_of(c * cs, cs), cs) for c in cid]
        sb = [s.astype(BF16) for s in carry]
        ws = [_dot(w16[d, rows[d], :], sb[d]) for d in range(2)]
        qs_s = [_dot(qd16[d, rows[d], :], sb[d]) for d in range(2)]
        vnb = [(u_sc[d, rows[d], :] - ws[d]).astype(BF16) for d in range(2)]
        kv = [_dot_tn(kdk16[d, rows[d], :], vnb[d]) for d in range(2)]
        o = [qs_s[d] + _dot(qk16[d, rows[d], :], vnb[d]) for d in range(2)]
        s_new = []
        for d in range(2):
            g0 = pl.multiple_of(cid[d] * SUBLANES, SUBLANES)
            s_new.append(carry[d] * glast[d, pl.ds(g0, 1), :] + kv[d])
            oacc[rows[d], :] = oacc[rows[d], :] + o[d]
        return tuple(s_new)

    s0 = jnp.zeros((hd, GDN_DV), F32)
    lax.fori_loop(0, n_chunks, step, (s0, s0))

    onw = onorm_ref[...]

    def out_gate(o, z):
        y = o * lax.rsqrt(jnp.mean(o * o, axis=-1, keepdims=True) + NORM_EPS) * onw
        return (y * _silu(z.astype(F32))).astype(BF16)

    yc_ref[...] = out_gate(oacc[0:n_ctx, :], zc_ref[...])
    for r0 in range(0, n_lat, tile):
        yl_ref[r0:r0 + tile, :] = out_gate(oacc[n_ctx + r0:n_ctx + r0 + tile, :], zl_ref[r0:r0 + tile, :])


def _gdn(p, conv_w, a_log, dt_bias, out_norm, l, geo):
    b, n_ctx, n_lat = geo["b"], geo["ctx"], geo["lat"]
    n_all = n_ctx + n_lat
    lat0 = geo["tc"] // n_lat
    hd = GDN_DK

    def lat_spec(col):
        return pl.BlockSpec((n_lat, hd), lambda bi, h: (lat0 + bi, col // hd + h))

    def ctx_spec(col):
        return pl.BlockSpec((n_ctx, hd), lambda bi, h: (bi, col // hd + h))

    def fixed_lat(col):
        return pl.BlockSpec((n_lat, hd), lambda bi, h: (lat0 + bi, col // hd))

    def fixed_ctx(col):
        return pl.BlockSpec((n_ctx, hd), lambda bi, h: (bi, col // hd))

    def cw_spec(col):
        return pl.BlockSpec((None, GDN_CONV, hd), lambda bi, h: (l, 0, col // hd + h))

    smem = pl.BlockSpec(memory_space=pltpu.SMEM)
    kern = functools.partial(_gdn_kernel, n_ctx=n_ctx, n_lat=n_lat)
    n_chunks = n_all // GDN_CHUNK
    assert n_ctx % GDN_CHUNK == 0 and n_lat % GDN_CHUNK == 0
    scratch = [
        pltpu.VMEM((n_all + 4 * SUBLANES, hd), F32),
        pltpu.VMEM((n_all, hd), F32),
        pltpu.VMEM((n_all, hd), F32),
        pltpu.VMEM((n_all, hd), F32),
        pltpu.VMEM((2, n_all, LANES), F32),
        pltpu.VMEM((2, n_all, LANES), F32),
        pltpu.VMEM((2, n_all, GDN_DV), F32),
        pltpu.VMEM((2, n_all, hd), BF16),
        pltpu.VMEM((2, n_all, GDN_CHUNK), BF16),
        pltpu.VMEM((2, n_all, hd), BF16),
        pltpu.VMEM((2, n_all, hd), BF16),
        pltpu.VMEM((2, n_chunks * SUBLANES, LANES), F32),
        pltpu.VMEM((n_all, GDN_DV), F32),
    ]
    y_lat, y_ctx = pl.pallas_call(
        kern,
        out_shape=(jax.ShapeDtypeStruct((b * n_lat, GDN_HEADS * GDN_DV), BF16),
                   jax.ShapeDtypeStruct((b * n_ctx, GDN_HEADS * GDN_DV), BF16)),
        grid=(b, GDN_HEADS),
        in_specs=[smem, smem,
                  lat_spec(COL_Q), lat_spec(COL_K), lat_spec(COL_V), lat_spec(COL_Z), fixed_lat(COL_AB),
                  ctx_spec(COL_Q), ctx_spec(COL_K), ctx_spec(COL_V), ctx_spec(COL_Z), fixed_ctx(COL_AB),
                  cw_spec(0), cw_spec(GDN_HEADS * hd), cw_spec(2 * GDN_HEADS * hd),
                  pl.BlockSpec((None, 1, GDN_DV), lambda bi, h: (l, 0, 0))],
        out_specs=(pl.BlockSpec((n_lat, GDN_DV), lambda bi, h: (bi, h)),
                   pl.BlockSpec((n_ctx, GDN_DV), lambda bi, h: (bi, h))),
        scratch_shapes=scratch,
        compiler_params=_cparams(("arbitrary", "arbitrary")),
        name="gdn",
    )(a_log[l], dt_bias[l], p, p, p, p, p, p, p, p, p, p, conv_w, conv_w, conv_w, out_norm)
    return y_lat, y_ctx


def _head_rms(x, bd, w):
    ss = _dot(x * x, bd, precision=HIGHEST) * (1.0 / HEAD_DIM)
    return x * lax.rsqrt(ss + NORM_EPS) * w


def _rope(x, cos, sin_signed, lane):
    nf = HEAD_DIM // 4
    partner = jnp.where((lane % (2 * nf)) < nf, pltpu.roll(x, LANES - nf, 1), pltpu.roll(x, nf, 1))
    return x * cos + partner * sin_signed


def _attn_kernel(sink_ref, q_ref, kq_ref, vq_ref, kc_ref, vc_ref,
                 cosq_ref, sinq_ref, cosk_ref, sink_k_ref, qw_ref, kw_ref,
                 o_ref, kp, vp, kcp, vcp, *, n_keys, n_ctx, local):
    n = pl.program_id(1)
    w = WINDOW
    lane = lax.broadcasted_iota(I32, (1, LANES), 1)
    lo_mask = (lane < HEAD_DIM).astype(F32)
    hi_mask = 1.0 - lo_mask
    bd_r = lax.broadcasted_iota(I32, (LANES, LANES), 0) // HEAD_DIM
    bd_c = lax.broadcasted_iota(I32, (LANES, LANES), 1) // HEAD_DIM
    bd = (bd_r == bd_c).astype(F32)
    kw = kw_ref[...]

    def store_variants(dst, row0, x):
        xr = pltpu.roll(x, HEAD_DIM, 1)
        rows = x.shape[0]
        dst[0, row0:row0 + rows, :] = (x * lo_mask).astype(BF16)
        dst[1, row0:row0 + rows, :] = (xr * hi_mask).astype(BF16)
        dst[2, row0:row0 + rows, :] = (xr * lo_mask).astype(BF16)
        dst[3, row0:row0 + rows, :] = (x * hi_mask).astype(BF16)

    @pl.when(n == 0)
    def _prepare_keys():
        tile = 256
        for r0 in range(0, n_ctx, tile):
            rr = min(tile, n_ctx - r0)
            kc = _head_rms(kc_ref[r0:r0 + rr, :].astype(F32), bd, kw)
            store_variants(kcp, r0, kc)
            store_variants(vcp, r0, vc_ref[r0:r0 + rr, :].astype(F32))
        if local:
            zero = jnp.zeros((w, LANES), BF16)
            for t in range(4):
                kp[t, 0:w, :] = zero
                kp[t, w + n_keys:2 * w + n_keys, :] = zero
                vp[t, 0:w, :] = zero
                vp[t, w + n_keys:2 * w + n_keys, :] = zero
            for r0 in range(0, n_keys, tile):
                k = _head_rms(kq_ref[r0:r0 + tile, :].astype(F32), bd, kw)
                k = _rope(k, cosk_ref[r0:r0 + tile, :], sink_k_ref[r0:r0 + tile, :], lane)
                store_variants(kp, w + r0, k)
                store_variants(vp, w + r0, vq_ref[r0:r0 + tile, :].astype(F32))

    qw = qw_ref[...]
    q2 = []
    for g in range(ATT_HEADS // 2):
        qg = _head_rms(q_ref[:, g * LANES:(g + 1) * LANES].astype(F32), bd, qw)
        if local:
            qg = _rope(qg, cosq_ref[...], sinq_ref[...], lane)
        q2.append((qg * (HEAD_DIM ** -0.5)).astype(BF16))

    rows2 = 2 * w
    row_i = lax.broadcasted_iota(I32, (rows2, 1), 0)
    first = row_i < w
    if local:
        n_blk = n_keys // w
        qq = lax.broadcasted_iota(I32, (rows2, 3 * w), 0) % w
        kcol = lax.broadcasted_iota(I32, (rows2, 3 * w), 1)
        kk = kcol % w
        blk = kcol // w
        valid = ((blk == 1)
                 | ((blk == 0) & (kk >= qq) & (n >= 1))
                 | ((blk == 2) & (kk <= qq) & (n < n_blk - 1)))
        k0 = pl.multiple_of(n * w, w)

    for j in range(ATT_KV_HEADS):
        lhs = jnp.concatenate([q2[2 * j], q2[2 * j + 1]], axis=0)
        pair_out = [None, None]
        for t in range(2):
            var = 2 * j + t
            sink = jnp.where(first, sink_ref[4 * j + t], sink_ref[4 * j + 2 + t])
            s_ctx = _dot_nt(lhs, kcp[var])
            m = jnp.maximum(jnp.max(s_ctx, axis=-1, keepdims=True), sink)
            if local:
                s_loc = _dot_nt(lhs, kp[var, pl.ds(k0, 3 * w), :])
                s_loc = jnp.where(valid, s_loc, NEG_INF)
                m = jnp.maximum(m, jnp.max(s_loc, axis=-1, keepdims=True))
            e_ctx = jnp.exp(s_ctx - m)
            den = jnp.sum(e_ctx, axis=-1, keepdims=True) + jnp.exp(sink - m)
            pv = _dot(e_ctx.astype(BF16), vcp[var])
            if local:
                e_loc = jnp.exp(s_loc - m)
                den = den + jnp.sum(e_loc, axis=-1, keepdims=True)
                pv = pv + _dot(e_loc.astype(BF16), vp[var, pl.ds(k0, 3 * w), :])
            o = pv / den
            for half_i in range(2):
                part = o[half_i * w:(half_i + 1) * w, :]
                pair_out[half_i] = part if pair_out[half_i] is None else pair_out[half_i] + part
        for half_i in range(2):
            c0 = (2 * j + half_i) * LANES
            o_ref[:, c0:c0 + LANES] = pair_out[half_i].astype(BF16)


def _attention(p, sink, q_norm_w, k_norm_w, rope_cos, rope_sin, l, geo, ctx_queries):
    b, n_ctx, n_lat = geo["b"], geo["ctx"], geo["lat"]
    w = WINDOW
    lat0 = geo["tc"] // n_lat
    local = not ctx_queries
    n_q = n_ctx if ctx_queries else n_lat
    nqb = n_q // w
    q_row0 = 0 if ctx_queries else geo["tc"] // w
    kern = functools.partial(_attn_kernel, n_keys=n_lat, n_ctx=n_ctx, local=local)
    in_specs = [
        pl.BlockSpec(memory_space=pltpu.SMEM),
        pl.BlockSpec((w, ATT_HEADS * HEAD_DIM), lambda bi, n: (q_row0 + bi * nqb + n, COL_AQ // 512)),
        pl.BlockSpec((n_lat, LANES), lambda bi, n: (lat0 + bi, COL_AK // LANES)),
        pl.BlockSpec((n_lat, LANES), lambda bi, n: (lat0 + bi, COL_AV // LANES)),
        pl.BlockSpec((n_ctx, LANES), lambda bi, n: (bi, COL_AK // LANES)),
        pl.BlockSpec((n_ctx, LANES), lambda bi, n: (bi, COL_AV // LANES)),
        pl.BlockSpec((w, LANES), lambda bi, n: (n if local else 0, 0)),
        pl.BlockSpec((w, LANES), lambda bi, n: (n if local else 0, 0)),
        pl.BlockSpec((n_lat, LANES), lambda bi, n: (0, 0)),
        pl.BlockSpec((n_lat, LANES), lambda bi, n: (0, 0)),
        pl.BlockSpec((None, 1, LANES), lambda bi, n: (l, 0, 0)),
        pl.BlockSpec((None, 1, LANES), lambda bi, n: (l, 0, 0)),
    ]
    scratch = [
        pltpu.VMEM((4, n_lat + 2 * w, LANES), BF16),
        pltpu.VMEM((4, n_lat + 2 * w, LANES), BF16),
        pltpu.VMEM((4, n_ctx, LANES), BF16),
        pltpu.VMEM((4, n_ctx, LANES), BF16),
    ]
    return pl.pallas_call(
        kern,
        out_shape=jax.ShapeDtypeStruct((b * n_q, ATT_HEADS * HEAD_DIM), BF16),
        grid=(b, nqb),
        in_specs=in_specs,
        out_specs=pl.BlockSpec((w, ATT_HEADS * HEAD_DIM), lambda bi, n: (bi * nqb + n, 0)),
        scratch_shapes=scratch,
        compiler_params=_cparams(("arbitrary", "arbitrary")),
        name="attn_ctx" if ctx_queries else "attn_lat",
    )(sink[l], p, p, p, p, p, rope_cos, rope_sin, rope_cos, rope_sin, q_norm_w, k_norm_w)


def _merge_kernel(x_ref, mod_ref, n2_ref, y0c_ref, y0l_ref, y3c_ref, y3l_ref,
                  sb_ref, sc_ref, sx_ref, scp_ref, sxp_ref, scn_ref, sxn_ref,
                  gu_ref, gv_ref, mg0_ref, mg1_ref, mg2_ref, mg3_ref,
                  scw_ref, lnw_ref, lnb_ref, sgw_ref, sgb_ref, wb_ref, wo_ref, rw_ref, rb_ref,
                  xo_ref, h2_ref, idx_ref, gate_ref, rank_ref, cnt_ref,
                  carry, *, tm, tc, n_ctx, n_lat):
    i = pl.program_id(0)
    r0 = i * tm

    row = lax.broadcasted_iota(I32, (tm, 1), 0)
    g_row = r0 + row
    in_ctx = g_row < tc
    seg_pos = jnp.where(in_ctx, g_row % n_ctx, (g_row - tc) % n_lat)
    seg_len = jnp.where(in_ctx, n_ctx, n_lat)
    cx = sc_ref[...].astype(F32) * sx_ref[...].astype(F32)
    cx_prev_halo = scp_ref[SUBLANES - 1:SUBLANES, :].astype(F32) * sxp_ref[SUBLANES - 1:SUBLANES, :].astype(F32)
    cx_next_halo = scn_ref[0:1, :].astype(F32) * sxn_ref[0:1, :].astype(F32)
    prev = jnp.where(row == 0, cx_prev_halo, pltpu.roll(cx, 1, 0))
    prev = jnp.where(seg_pos == 0, 0.0, prev)
    nxt = jnp.where(row == tm - 1, cx_next_halo, pltpu.roll(cx, tm - 1, 0))
    nxt = jnp.where(seg_pos == seg_len - 1, 0.0, nxt)
    scw = scw_ref[...]
    y1 = sb_ref[...].astype(F32) * (prev * scw[0:1, :] + cx * scw[1:2, :] + nxt * scw[2:3, :])

    inv_sqrt2 = 1.0 / math.sqrt(2.0)

    def gelu(t):
        return 0.5 * t * (1.0 + lax.erf(t * inv_sqrt2))

    u = gelu(gu_ref[...].astype(F32))
    v = gelu(gv_ref[...].astype(F32))
    mu = jnp.mean(v, axis=-1, keepdims=True)
    vc = v - mu
    v = vc * lax.rsqrt(jnp.mean(vc * vc, axis=-1, keepdims=True) + LN_EPS) * lnw_ref[...] + lnb_ref[...]
    vb = v.astype(BF16)
    gw = SG_WIDTH // SG_GROUPS
    chunks = []
    for c in range(tm // SG_CHUNK):
        groups = []
        for g in range(SG_GROUPS):
            groups.append(_dot(sgw_ref[g], vb[c * SG_CHUNK:(c + 1) * SG_CHUNK, g * gw:(g + 1) * gw]))
        chunks.append(jnp.concatenate(groups, axis=1) + sgb_ref[...])
    y2 = u * jnp.concatenate(chunks, axis=0)

    tile_in_ctx = r0 < tc
    y0 = jnp.where(tile_in_ctx, y0c_ref[...], y0l_ref[...])
    y3 = jnp.where(tile_in_ctx, y3c_ref[...], y3l_ref[...])
    ys = (y0, y1.astype(BF16), y2.astype(BF16), y3)
    gates = (mg0_ref, mg1_ref, mg2_ref, mg3_ref)
    m = None
    for br in range(N_BRANCH):
        term = jax.nn.sigmoid(gates[br][...].astype(F32)) * _dot(ys[br], wb_ref[br])
        m = term if m is None else m + term
    y = _dot(m.astype(BF16), wo_ref[...])
    x_new = x_ref[...] + mod_ref[2:3, :] * y
    xo_ref[...] = x_new

    ms = jnp.mean(x_new * x_new, axis=-1, keepdims=True)
    h2 = x_new * lax.rsqrt(ms + NORM_EPS) * n2_ref[...] * (1.0 + mod_ref[4:5, :]) + mod_ref[3:4, :]
    half = D_MODEL // 2
    _store_planes(h2_ref, _pack_bf16_pair(h2[:, :half], h2[:, half:]))

    h2_hi = h2.astype(BF16)
    h2_lo = (h2 - h2_hi.astype(F32)).astype(BF16)
    prod = _dot(jnp.concatenate([h2_hi, h2_lo], axis=1), rw_ref[...])
    lane = lax.broadcasted_iota(I32, (tm, LANES), 1)
    logits = prod + pltpu.roll(prod, LANES - N_EXPERTS, 1)
    logits = jnp.where(lane < N_EXPERTS, logits, NEG_INF) + rb_ref[...]
    lane_f = lane.astype(F32)
    work = logits
    topv = jnp.full((tm, LANES), NEG_INF, F32)
    topi = jnp.zeros((tm, LANES), I32)
    onehot = jnp.zeros((tm, LANES), F32)
    firsts = []
    for k in range(TOP_K):
        mx = jnp.max(work, axis=-1, keepdims=True)
        first_f = jnp.min(jnp.where(work == mx, lane_f, float(LANES)), axis=-1, keepdims=True)
        first = first_f.astype(I32)
        hit = lane == first
        topv = jnp.where(lane == k, mx, topv)
        topi = jnp.where(lane == k, first, topi)
        onehot = jnp.where(hit, 1.0, onehot)
        work = jnp.where(hit, -jnp.inf, work)
        firsts.append(first)
    e = jnp.where(lane < TOP_K, jnp.exp(topv - jnp.max(topv, axis=-1, keepdims=True)), 0.0)
    gate = e / jnp.sum(e, axis=-1, keepdims=True)

    @pl.when(i == 0)
    def _init():
        carry[...] = jnp.zeros_like(carry)

    tri = (lax.broadcasted_iota(I32, (tm, tm), 0) > lax.broadcasted_iota(I32, (tm, tm), 1)).astype(BF16)
    before = carry[0:1, :] + _dot(tri, onehot.astype(BF16))
    rank = jnp.zeros((tm, LANES), F32)
    for k in range(TOP_K):
        r_k = jnp.sum(jnp.where(lane == firsts[k], before, 0.0), axis=-1, keepdims=True)
        rank = jnp.where(lane == k, r_k, rank)
    new_carry = carry[0:1, :] + jnp.sum(onehot, axis=0, keepdims=True)
    carry[...] = jnp.broadcast_to(new_carry, carry.shape)
    cnt_ref[...] = jnp.broadcast_to(new_carry, cnt_ref.shape).astype(I32)
    idx_ref[...] = topi[:, :SUBLANES]
    gate_ref[...] = gate[:, :SUBLANES]
    rank_ref[...] = rank[:, :SUBLANES].astype(I32)


def _merge(x, mod, norm2, y0_ctx, y0_lat, y3_ctx, y3_lat, p, w, l, geo):
    t, dm = x.shape
    tm = geo["tm_merge"]
    mod_row = geo["mod_row"]
    kern = functools.partial(_merge_kernel, tm=tm, tc=geo["tc"], n_ctx=geo["ctx"], n_lat=geo["lat"])
    halo = tm // SUBLANES
    n_halo = t // SUBLANES

    def col(c, width=512):
        return pl.BlockSpec((tm, width), lambda i: (i, c // width))

    def prev_halo(c):
        return pl.BlockSpec((SUBLANES, 512), lambda i: (jnp.maximum(i * halo - 1, 0), c // 512))

    def next_halo(c):
        return pl.BlockSpec((SUBLANES, 512), lambda i: (jnp.minimum((i + 1) * halo, n_halo - 1), c // 512))

    def layer(shape):
        nd = len(shape)
        return pl.BlockSpec((None,) + shape, lambda i: (l,) + (0,) * nd)

    n_ctx_tiles = geo["tc"] // tm

    def ctx_rows():
        return pl.BlockSpec((tm, 512), lambda i: (jnp.minimum(i, n_ctx_tiles - 1), 0))

    def lat_rows():
        return pl.BlockSpec((tm, 512), lambda i: (jnp.maximum(i - n_ctx_tiles, 0), 0))

    in_specs = [
        pl.BlockSpec((tm, dm), lambda i: (i, 0)),
        pl.BlockSpec((None, None, N_MOD, dm), lambda i: (l, mod_row(i * tm), 0, 0)),
        layer((1, dm)),
        ctx_rows(), lat_rows(), ctx_rows(), lat_rows(),
        col(COL_SB), col(COL_SC), col(COL_SX),
        prev_halo(COL_SC), prev_halo(COL_SX), next_halo(COL_SC), next_halo(COL_SX),
        col(COL_GU), col(COL_GV),
        col(COL_MG, 1024), col(COL_MG + 1024, 1024), col(COL_MG + 2048, 1024), col(COL_MG + 3072, 1024),
        layer((SC_CONV, SC_WIDTH)), layer((1, SG_WIDTH)), layer((1, SG_WIDTH)),
        layer((SG_GROUPS, SG_CHUNK, SG_CHUNK)), layer((SG_CHUNK, SG_WIDTH)),
        layer((N_BRANCH, BRANCH_W, dm)), layer((dm, dm)), layer((2 * dm, LANES)), layer((1, LANES)),
    ]
    out_shape = (
        jax.ShapeDtypeStruct((t, dm), F32),
        jax.ShapeDtypeStruct((SC_PLANES, t, SC_ROW), U32),
        jax.ShapeDtypeStruct((t, SUBLANES), I32),
        jax.ShapeDtypeStruct((t, SUBLANES), F32),
        jax.ShapeDtypeStruct((t, SUBLANES), I32),
        jax.ShapeDtypeStruct((SUBLANES, LANES), I32),
    )
    out_specs = (
        pl.BlockSpec((tm, dm), lambda i: (i, 0)),
        pl.BlockSpec((SC_PLANES, tm, SC_ROW), lambda i: (0, i, 0)),
        pl.BlockSpec((tm, SUBLANES), lambda i: (i, 0)),
        pl.BlockSpec((tm, SUBLANES), lambda i: (i, 0)),
        pl.BlockSpec((tm, SUBLANES), lambda i: (i, 0)),
        pl.BlockSpec((SUBLANES, LANES), lambda i: (0, 0)),
    )
    return pl.pallas_call(
        kern,
        out_shape=out_shape,
        grid=(t // tm,),
        in_specs=in_specs,
        out_specs=out_specs,
        scratch_shapes=[pltpu.VMEM((SUBLANES, LANES), F32)],
        compiler_params=_cparams(("arbitrary",)),
        name="merge",
    )(x, mod, norm2, y0_ctx, y0_lat, y3_ctx, y3_lat, p, p, p, p, p, p, p, p, p, p, p, p, p,
      w["sc_conv"], w["sg_ln_w"], w["sg_ln_b"], w["sg_w"], w["sg_b"], w["w_branch"], w["w_out"],
      w["router_w"], w["router_b"])


def _sc_mesh():
    return plsc.VectorSubcoreMesh(core_axis_name="core", subcore_axis_name="subcore")


def _sc_scatter_rows(src, idx, n_out):
    n, width = src.shape

    @pl.kernel(out_type=jax.ShapeDtypeStruct((n_out, width), U32), mesh=_sc_mesh(), scratch_types=[])
    def scatter(x_hbm, i_hbm, o_hbm):
        def body(x_vmem, *i_vmem):
            for iv in i_vmem:
                pltpu.sync_copy(x_vmem, o_hbm.at[iv.at[0]])

        pltpu.emit_pipeline(
            body,
            grid=(n // SC_WINDOW,),
            in_specs=[pl.BlockSpec((SC_WINDOW, width), lambda i: (i, 0))]
            + [pl.BlockSpec((1, SC_WINDOW), functools.partial(lambda i, k: (k, i), k=k)) for k in range(TOP_K)],
            out_specs=[],
            core_axis_name=("core", "subcore"),
            dimension_semantics=(pltpu.PARALLEL,),
        )(x_hbm, *([i_hbm] * TOP_K))

    return scatter(src, idx)


def _sc_gather_rows(src, idx):
    n = idx.shape[0]
    width = src.shape[1]

    @pl.kernel(out_type=jax.ShapeDtypeStruct((n, width), U32), mesh=_sc_mesh(), scratch_types=[])
    def gather(x_hbm, i_hbm, o_hbm):
        def body(i_vmem, o_vmem):
            pltpu.sync_copy(x_hbm.at[i_vmem.at[0]], o_vmem)

        pltpu.emit_pipeline(
            body,
            grid=(n // SC_WINDOW,),
            in_specs=[pl.BlockSpec((1, SC_WINDOW), lambda i: (0, i))],
            out_specs=[pl.BlockSpec((SC_WINDOW, width), lambda i: (i, 0))],
            core_axis_name=("core", "subcore"),
            dimension_semantics=(pltpu.PARALLEL,),
        )(i_hbm, o_hbm)

    return gather(src, idx.reshape(1, n))


def _expert_kernel(be_ref, bv_ref, xs_ref, w1_ref, b1_ref, w2_ref, b2_ref, ys_ref):
    del be_ref
    valid = lax.broadcasted_iota(I32, (MOE_BM, 1), 0) < bv_ref[pl.program_id(0)]
    lo, hi = _unpack_bf16_pair(jnp.where(valid, _load_planes(xs_ref), jnp.uint32(0)))
    half = D_MODEL // 2
    h = (_dot(lo.astype(BF16), w1_ref[0:half, :]) + _dot(hi.astype(BF16), w1_ref[half:, :]) + b1_ref[...])
    hg = W1_GROUP // 2
    acts = []
    for g in range(2 * D_EXPERT // W1_GROUP):
        glu = jnp.minimum(h[:, g * W1_GROUP:g * W1_GROUP + hg], SWIGLU_LIMIT)
        lin = jnp.clip(h[:, g * W1_GROUP + hg:(g + 1) * W1_GROUP], -SWIGLU_LIMIT, SWIGLU_LIMIT)
        acts.append((glu * jax.nn.sigmoid(SWIGLU_ALPHA * glu) * (lin + 1.0)).astype(BF16))
    y = _dot(jnp.concatenate(acts, axis=1), w2_ref[...]) + b2_ref[...]
    _store_planes(ys_ref, _pack_bf16_pair(y[:, :half], y[:, half:]))


def _w1_regroup_kernel(w_ref, o_ref):
    hg = W1_GROUP // 2
    r = lax.broadcasted_iota(I32, (W1_GROUP, W1_GROUP), 0)
    c = lax.broadcasted_iota(I32, (W1_GROUP, W1_GROUP), 1)
    perm = (r == jnp.where(c < hg, 2 * c, 2 * (c - hg) + 1)).astype(BF16)
    for g in range(2 * D_EXPERT // W1_GROUP):
        sl = slice(g * W1_GROUP, (g + 1) * W1_GROUP)
        o_ref[:, sl] = _dot(w_ref[:, sl].astype(BF16), perm).astype(BF16)


def _w1_regroup(exp_w1):
    depth, ne, dm, dh = exp_w1.shape
    tr = 512
    out = pl.pallas_call(
        _w1_regroup_kernel,
        out_shape=jax.ShapeDtypeStruct((depth * ne, dm, dh), BF16),
        grid=(depth * ne, dm // tr),
        in_specs=[pl.BlockSpec((None, tr, dh), lambda e, i: (e, i, 0))],
        out_specs=pl.BlockSpec((None, tr, dh), lambda e, i: (e, i, 0)),
        compiler_params=_cparams(("arbitrary", "arbitrary")),
        name="w1_regroup",
    )(exp_w1.reshape(depth * ne, dm, dh))
    return out.reshape(depth, ne, dm, dh)


def _experts(xs, block_expert, block_valid, w1p, b1p, w2, b2, l):
    n_slots = xs.shape[1]
    bm = MOE_BM
    grid_spec = pltpu.PrefetchScalarGridSpec(
        num_scalar_prefetch=2,
        grid=(n_slots // bm,),
        in_specs=[
            pl.BlockSpec((SC_PLANES, bm, SC_ROW), lambda i, be, bv: (0, i, 0)),
            pl.BlockSpec((None, None, D_MODEL, 2 * D_EXPERT), lambda i, be, bv: (l, be[i], 0, 0)),
            pl.BlockSpec((None, None, 1, 2 * D_EXPERT), lambda i, be, bv: (l, be[i], 0, 0)),
            pl.BlockSpec((None, None, D_EXPERT, D_MODEL), lambda i, be, bv: (l, be[i], 0, 0)),
            pl.BlockSpec((None, None, 1, D_MODEL), lambda i, be, bv: (l, be[i], 0, 0)),
        ],
        out_specs=pl.BlockSpec((SC_PLANES, bm, SC_ROW), lambda i, be, bv: (0, i, 0)),
    )
    return pl.pallas_call(
        _expert_kernel,
        out_shape=jax.ShapeDtypeStruct((SC_PLANES, n_slots, SC_ROW), U32),
        grid_spec=grid_spec,
        compiler_params=_cparams(("arbitrary",)),
        name="moe_experts",
    )(block_expert, block_valid, xs, w1p, b1p, w2, b2)


def _combine_kernel(gate_ref, x_ref, mod_ref, yg_ref, xo_ref, *, tm):
    gate = gate_ref[...]
    half = D_MODEL // 2
    f_lo = jnp.zeros((tm, half), F32)
    f_hi = jnp.zeros((tm, half), F32)
    for k in range(TOP_K):
        lo, hi = _unpack_bf16_pair(_load_planes(yg_ref, k * SC_PLANES))
        g = gate[:, k:k + 1]
        f_lo = f_lo + g * lo
        f_hi = f_hi + g * hi
    g2 = mod_ref[5:6, :]
    xo_ref[:, :half] = x_ref[:, :half] + g2[:, :half] * f_lo
    xo_ref[:, half:] = x_ref[:, half:] + g2[:, half:] * f_hi


def _combine(x, mod, gate, yg, l, geo, first_row=0):
    t, dm = x.shape
    tm = geo["tm_moe"]
    mod_row = geo["mod_row"]
    i0 = first_row // tm
    return pl.pallas_call(
        functools.partial(_combine_kernel, tm=tm),
        out_shape=jax.ShapeDtypeStruct((t - first_row, dm), F32),
        grid=((t - first_row) // tm,),
        in_specs=[
            pl.BlockSpec((tm, SUBLANES), lambda i: (i0 + i, 0)),
            pl.BlockSpec((tm, dm), lambda i: (i0 + i, 0)),
            pl.BlockSpec((None, None, N_MOD, dm), lambda i: (l, mod_row((i0 + i) * tm), 0, 0)),
            pl.BlockSpec((TOP_K * SC_PLANES, tm, SC_ROW), lambda i: (0, i0 + i, 0)),
        ],
        out_specs=pl.BlockSpec((tm, dm), lambda i: (i, 0)),
        compiler_params=_cparams(("arbitrary",)),
        name="moe_combine",
    )(gate, x, mod, yg)


def _moe_plan(idx, rank, counts, n_blocks):
    bm = MOE_BM
    padded = (counts + bm - 1) // bm * bm
    pad_end = jnp.cumsum(padded)
    pad_start = pad_end - padded
    expert = idx[:, :TOP_K]
    onehot = expert[:, :, None] == jnp.arange(N_EXPERTS, dtype=I32)[None, None, :]
    dest = rank[:, :TOP_K] + jnp.sum(jnp.where(onehot, pad_start[None, None, :], 0), axis=-1)
    block_start = jnp.arange(n_blocks, dtype=I32) * bm
    block_expert = jnp.minimum(
        jnp.sum((block_start[:, None] >= pad_end[None, :]).astype(I32), axis=-1), N_EXPERTS - 1)
    block_valid = jnp.clip(counts[block_expert] - (block_start - pad_start[block_expert]), 0, bm)
    plane_off = jnp.arange(SC_PLANES, dtype=I32) * (n_blocks * bm)
    dest_rows = jnp.transpose(dest.astype(I32))[:, None, :] + plane_off[None, :, None]
    scatter_idx = dest_rows.reshape(TOP_K, -1)
    gather_idx = dest_rows.reshape(-1)
    return scatter_idx, gather_idx, block_expert.astype(I32), block_valid.astype(I32)


def _rope_tables(n_lat):
    rows = n_lat // GRID_W
    row = jnp.repeat(jnp.arange(rows, dtype=F32), GRID_W)
    colp = jnp.tile(jnp.arange(GRID_W, dtype=F32), rows)
    n_freq = HEAD_DIM // 4
    inv = ROPE_BASE ** (-jnp.arange(n_freq, dtype=F32) / n_freq)
    ang = jnp.concatenate([row[:, None] * inv, colp[:, None] * inv], axis=-1)
    cos, sin = jnp.cos(ang), jnp.sin(ang)
    cr, cc = cos[:, :n_freq], cos[:, n_freq:]
    sr, sc = sin[:, :n_freq], sin[:, n_freq:]
    cos64 = jnp.concatenate([cr, cr, cc, cc], axis=-1)
    sin64 = jnp.concatenate([-sr, sr, -sc, sc], axis=-1)
    return jnp.tile(cos64, (1, 2)), jnp.tile(sin64, (1, 2))


def _prep_weights(w_in, w_branch, w_out, router_w, router_b, sg_b, exp_w1, exp_b1, exp_w2, exp_b2,
                  attn_q_norm, attn_k_norm):
    depth, dm, _ = w_in.shape
    qkvz = 2048
    ab0, sc0, sg0, at0, mg0 = 2048, 2064, 3600, 4624, 5392
    pad = PROJ_COLS - (COL_AB + 16)
    w_p = jnp.concatenate([
        w_in[:, :, 0:qkvz],
        w_in[:, :, sc0:sg0],
        w_in[:, :, sg0:at0],
        w_in[:, :, at0:at0 + 512],
        w_in[:, :, mg0:mg0 + 4096],
        w_in[:, :, at0 + 512:at0 + 768],
        w_in[:, :, ab0:ab0 + 16],
        jnp.zeros((depth, dm, pad), w_in.dtype),
    ], axis=-1).astype(BF16)
    rw_hi = router_w.astype(BF16)
    rw_lo = (router_w - rw_hi.astype(F32)).astype(BF16)
    rw = jnp.concatenate([
        jnp.concatenate([rw_hi, rw_lo, jnp.zeros((depth, dm, LANES - 2 * N_EXPERTS), BF16)], axis=-1),
        jnp.concatenate([rw_hi, jnp.zeros((depth, dm, LANES - N_EXPERTS), BF16)], axis=-1),
    ], axis=1)
    rb = jnp.concatenate([router_b, jnp.full((depth, LANES - N_EXPERTS), NEG_INF, F32)], axis=-1)
    w1p = _w1_regroup(exp_w1)
    ne = exp_b1.shape[1]
    b1p = jnp.swapaxes(exp_b1.reshape(depth, ne, -1, W1_GROUP // 2, 2), -1, -2).reshape(depth, ne, -1)
    return {
        "w_in": w_p,
        "w_branch": w_branch.astype(BF16),
        "w_out": w_out.astype(BF16),
        "router_w": rw,
        "router_b": rb[:, None, :],
        "sg_b": jnp.repeat(jnp.swapaxes(sg_b, 1, 2), SG_WIDTH // SG_GROUPS, axis=2),
        "w1": w1p,
        "b1": b1p[:, :, None, :],
        "w2": exp_w2.astype(BF16),
        "b2": exp_b2[:, :, None, :],
        "q_norm": jnp.tile(attn_q_norm, (1, 2))[:, None, :],
        "k_norm": jnp.tile(attn_k_norm, (1, 2))[:, None, :],
    }


def kernel(x, c, ctx, c_ctx, norm1, norm2, w_mod, b_mod, w_in, gdn_conv, gdn_a_log, gdn_dt_bias, gdn_out_norm,
           sc_conv, sg_ln_w, sg_ln_b, sg_w, sg_b, attn_q_norm, attn_k_norm, attn_sink, w_branch, w_out,
           router_w, router_b, exp_w1, exp_b1, exp_w2, exp_b2):
    b, n_lat, dm = x.shape
    n_ctx = ctx.shape[1]
    depth = w_in.shape[0]
    tc = b * n_ctx
    t = tc + b * n_lat
    assert dm == D_MODEL and tc % n_lat == 0 and n_lat % 256 == 0 and n_ctx % 256 == 0
    tile_cap = math.gcd(tc, n_lat)

    def mod_row(r0):
        return jnp.where(r0 < tc, 0, 1 + (r0 - tc) // n_lat)

    geo = {
        "b": b, "ctx": n_ctx, "lat": n_lat, "tc": tc, "mod_row": mod_row,
        "tm_proj": min(1024, tile_cap), "tm_merge": min(512, tile_cap), "tm_moe": min(512, tile_cap),
    }

    wts = _prep_weights(w_in, w_branch, w_out, router_w, router_b, sg_b, exp_w1, exp_b1, exp_w2, exp_b2,
                        attn_q_norm, attn_k_norm)
    rope_cos, rope_sin = _rope_tables(n_lat)

    mod_rows = -(-(1 + b) // SUBLANES) * SUBLANES
    c_all = jnp.concatenate([c_ctx[None, :], c, jnp.zeros((mod_rows - 1 - b, dm), F32)], axis=0)
    mod = _modulation(c_all, w_mod, b_mod).reshape(depth, mod_rows, N_MOD, dm)

    xs_flat = jnp.concatenate([ctx.reshape(tc, dm), x.reshape(b * n_lat, dm)], axis=0)
    n_blocks = -(-(t * TOP_K) // MOE_BM) + N_EXPERTS
    n_slots = n_blocks * MOE_BM

    for l in range(depth):
        last = l == depth - 1
        p = _in_proj(xs_flat, mod, norm1[:, None, :], wts["w_in"], l, geo)
        g_lat, g_ctx = _gdn(p, gdn_conv, gdn_a_log, gdn_dt_bias, gdn_out_norm[:, None, :], l, geo)
        a_lat = _attention(p, attn_sink, wts["q_norm"], wts["k_norm"], rope_cos, rope_sin, l, geo, False)
        if last:
            a_ctx = jnp.zeros((tc, ATT_HEADS * HEAD_DIM), BF16)
        else:
            a_ctx = _attention(p, attn_sink, wts["q_norm"], wts["k_norm"], rope_cos, rope_sin, l, geo, True)
        layer_w = {
            "sc_conv": sc_conv, "sg_ln_w": sg_ln_w[:, None, :], "sg_ln_b": sg_ln_b[:, None, :], "sg_w": sg_w.astype(BF16),
            "sg_b": wts["sg_b"], "w_branch": wts["w_branch"], "w_out": wts["w_out"],
            "router_w": wts["router_w"], "router_b": wts["router_b"],
        }
        x_mid, h2p, idx, gate, rank, counts = _merge(xs_flat, mod, norm2[:, None, :], g_ctx, g_lat, a_ctx, a_lat, p,
                                                          layer_w, l, geo)
        scatter_idx, gather_idx, block_expert, block_valid = _moe_plan(idx, rank, counts[0, :N_EXPERTS], n_blocks)
        xs_sorted = _sc_scatter_rows(h2p.reshape(SC_PLANES * t, SC_ROW), scatter_idx, SC_PLANES * n_slots)
        ys_sorted = _experts(xs_sorted.reshape(SC_PLANES, n_slots, SC_ROW), block_expert, block_valid,
                             wts["w1"], wts["b1"], wts["w2"], wts["b2"], l)
        yg = _sc_gather_rows(ys_sorted.reshape(SC_PLANES * n_slots, SC_ROW), gather_idx)
        xs_flat = _combine(x_mid, mod, gate, yg.reshape(TOP_K * SC_PLANES, t, SC_ROW), l, geo,
                           first_row=tc if last else 0)

    return xs_flat.reshape(b, n_lat, dm)
```

```python
import functools
import math

import jax
import jax.numpy as jnp
from jax import lax
from jax.experimental import pallas as pl
from jax.experimental.pallas import tpu as pltpu
from jax.experimental.pallas import tpu_sc as plsc

F32 = jnp.float32
BF16 = jnp.bfloat16
I32 = jnp.int32
U32 = jnp.uint32
HIGHEST = lax.Precision.HIGHEST

D_MODEL = 1024
GRID_W = 64
N_MOD = 6
GDN_HEADS = 4
GDN_DK = 128
GDN_DV = 128
GDN_CONV = 5
SC_WIDTH = 512
SC_CONV = 3
SG_CHUNK = 128
SG_GROUPS = 4
SG_WIDTH = 512
ATT_HEADS = 8
ATT_KV_HEADS = 2
HEAD_DIM = 64
WINDOW = 128
ROPE_BASE = 10000.0
N_BRANCH = 4
BRANCH_W = 512
N_EXPERTS = 32
TOP_K = 4
D_EXPERT = 1024
SWIGLU_ALPHA = 1.702
SWIGLU_LIMIT = 7.0
NORM_EPS = 1e-6
LN_EPS = 1e-5
NEG_INF = -1e30

LANES = 128
SUBLANES = 8
VMEM_LIMIT_BYTES = 56 * 2**20

COL_Q = 0
COL_K = 512
COL_V = 1024
COL_Z = 1536
COL_SB = 2048
COL_SC = 2560
COL_SX = 3072
COL_GU = 3584
COL_GV = 4096
COL_AQ = 4608
COL_MG = 5120
COL_AK = 9216
COL_AV = 9344
COL_AB = 9472
PROJ_COLS = 10240
PROJ_TN = 2048
PROJ_CHUNK = 512

GDN_CHUNK = 128
GDN_SOLVE_BLOCK = 64
GDN_PREP_UNROLL = 8
MOE_BM = 512
W1_GROUP = 256
SC_ROW = 256
SC_WINDOW = 128
SC_PLANES = (D_MODEL // 2) // SC_ROW


def _cparams(sem):
    return pltpu.CompilerParams(dimension_semantics=sem, vmem_limit_bytes=VMEM_LIMIT_BYTES)


def _silu(x):
    return x * jax.nn.sigmoid(x)


def _dot(a, b, precision=None):
    return jnp.dot(a, b, preferred_element_type=F32, precision=precision)


def _dot_nt(a, b):
    return lax.dot_general(a, b, (((1,), (1,)), ((), ())), preferred_element_type=F32)


def _dot_tn(a, b):
    return lax.dot_general(a, b, (((0,), (0,)), ((), ())), preferred_element_type=F32)


def _aligned(start, multiple):
    return start if isinstance(start, int) else pl.multiple_of(start, multiple)


def _pack_bf16_pair(lo, hi):
    lo_b = pltpu.bitcast(lo.astype(BF16).astype(F32), U32)
    hi_b = pltpu.bitcast(hi.astype(BF16).astype(F32), U32)
    return (hi_b & jnp.uint32(0xFFFF0000)) | (lo_b >> 16)


def _unpack_bf16_pair(u):
    lo = pltpu.bitcast(u << 16, F32)
    hi = pltpu.bitcast(u & jnp.uint32(0xFFFF0000), F32)
    return lo, hi


def _store_planes(ref, packed):
    for p in range(SC_PLANES):
        ref[p] = packed[:, p * SC_ROW:(p + 1) * SC_ROW]


def _load_planes(ref, first=0):
    return jnp.concatenate([ref[first + p] for p in range(SC_PLANES)], axis=1)


def _mod_kernel(c_ref, w_ref, b_ref, o_ref):
    s = _silu(c_ref[...])
    o_ref[...] = _dot(s, w_ref[...], precision=HIGHEST) + b_ref[...]


def _modulation(c_all, w_mod, b_mod):
    depth, dm, nm = w_mod.shape
    rows = c_all.shape[0]
    tn = 1536
    return pl.pallas_call(
        _mod_kernel,
        out_shape=jax.ShapeDtypeStruct((depth, rows, nm), F32),
        grid=(depth, nm // tn),
        in_specs=[
            pl.BlockSpec((rows, dm), lambda l, j: (0, 0)),
            pl.BlockSpec((None, dm, tn), lambda l, j: (l, 0, j)),
            pl.BlockSpec((None, 1, tn), lambda l, j: (l, 0, j)),
        ],
        out_specs=pl.BlockSpec((None, rows, tn), lambda l, j: (l, 0, j)),
        compiler_params=_cparams(("arbitrary", "arbitrary")),
        name="modulation",
    )(c_all, w_mod, b_mod.reshape(depth, 1, nm))


def _in_proj_kernel(x_ref, mod_ref, nw_ref, w_ref, o_ref):
    x = x_ref[...]
    ms = jnp.mean(x * x, axis=-1, keepdims=True)
    y = x * lax.rsqrt(ms + NORM_EPS) * nw_ref[...]
    h = (y * (1.0 + mod_ref[1:2, :]) + mod_ref[0:1, :]).astype(BF16)
    for c in range(PROJ_TN // PROJ_CHUNK):
        sl = slice(c * PROJ_CHUNK, (c + 1) * PROJ_CHUNK)
        o_ref[:, sl] = _dot(h, w_ref[:, sl]).astype(BF16)


def _in_proj(x, mod, norm_w, w_p, l, geo):
    t, dm = x.shape
    tm = geo["tm_proj"]
    mod_row = geo["mod_row"]
    return pl.pallas_call(
        _in_proj_kernel,
        out_shape=jax.ShapeDtypeStruct((t, PROJ_COLS), BF16),
        grid=(PROJ_COLS // PROJ_TN, t // tm),
        in_specs=[
            pl.BlockSpec((tm, dm), lambda j, i: (i, 0)),
            pl.BlockSpec((None, None, N_MOD, dm), lambda j, i: (l, mod_row(i * tm), 0, 0)),
            pl.BlockSpec((None, 1, dm), lambda j, i: (l, 0, 0)),
            pl.BlockSpec((None, dm, PROJ_TN), lambda j, i: (l, 0, j)),
        ],
        out_specs=pl.BlockSpec((tm, PROJ_TN), lambda j, i: (i, j)),
        compiler_params=_cparams(("arbitrary", "arbitrary")),
        name="in_proj",
    )(x, mod, norm_w, w_p)


def _softplus(x):
    return jnp.maximum(x, 0.0) + jnp.log1p(jnp.exp(-jnp.abs(x)))


def _gdn_kernel(alog_ref, dtb_ref,
                ql_ref, kl_ref, vl_ref, zl_ref, abl_ref,
                qc_ref, kc_ref, vc_ref, zc_ref, abc_ref,
                cwq_ref, cwk_ref, cwv_ref, onorm_ref,
                yl_ref, yc_ref,
                xpad, qs, ks, vs, gsc, bsc,
                u_sc, w16, qk16, qd16, kdk16, glast, oacc,
                *, n_ctx, n_lat):
    cs = GDN_CHUNK
    hd = GDN_DK
    n_all = n_ctx + n_lat
    n_chunks = n_all // cs
    n_ctx_chunks = n_ctx // cs
    h = pl.program_id(1)
    pad = SUBLANES
    lat_off = n_ctx + 3 * pad

    zeros_pad = jnp.zeros((pad, hd), F32)
    xpad[0:pad, :] = zeros_pad
    xpad[pad + n_ctx:pad + n_ctx + 2 * pad, :] = jnp.zeros((2 * pad, hd), F32)
    xpad[lat_off + n_lat:lat_off + n_lat + pad, :] = zeros_pad

    half = GDN_CONV // 2
    tile = 256

    def conv_into(src_c, src_l, cw_ref, dst, mode):
        xpad[pad:pad + n_ctx, :] = src_c[...].astype(F32)
        xpad[lat_off:lat_off + n_lat, :] = src_l[...].astype(F32)
        w = cw_ref[...]
        for seg_off, dst_off, seg_len in ((pad, 0, n_ctx), (lat_off, n_ctx, n_lat)):
            for r0 in range(0, seg_len, tile):
                acc = jnp.zeros((tile, hd), F32)
                for j in range(GDN_CONV):
                    s = seg_off + r0 + j - half
                    acc = acc + xpad[s:s + tile, :] * w[j:j + 1, :]
                y = _silu(acc)
                if mode != "v":
                    y = y * lax.rsqrt(jnp.sum(y * y, axis=-1, keepdims=True) + NORM_EPS)
                if mode == "q":
                    y = y * (hd ** -0.5)
                dst[dst_off + r0:dst_off + r0 + tile, :] = y

    conv_into(qc_ref, ql_ref, cwq_ref, qs, "q")
    conv_into(kc_ref, kl_ref, cwk_ref, ks, "k")
    conv_into(vc_ref, vl_ref, cwv_ref, vs, "v")

    sel_r = lax.broadcasted_iota(I32, (LANES, LANES), 0)
    pos = lax.broadcasted_iota(I32, (tile, LANES), 0) % cs
    for d in range(2):
        col = d * GDN_HEADS + h
        sel_a = (sel_r == col).astype(BF16)
        sel_b = (sel_r == (2 * GDN_HEADS + col)).astype(BF16)
        neg_a = -jnp.exp(jnp.full((1, LANES), alog_ref[d, h], F32))
        dtb = dtb_ref[d, h]
        for src, r_off, r_len in ((abc_ref, 0, n_ctx), (abl_ref, n_ctx, n_lat)):
            for r0 in range(0, r_len, tile):
                ab = src[r0:r0 + tile, :]
                a_b = _dot(ab, sel_a)
                b_b = _dot(ab, sel_b)
                gc = neg_a * _softplus(a_b + dtb)
                s = 1
                while s < cs:
                    if d == 0:
                        gc = gc + jnp.where(pos >= s, pltpu.roll(gc, s, 0), 0.0)
                    else:
                        gc = gc + jnp.where(pos < cs - s, pltpu.roll(gc, tile - s, 0), 0.0)
                    s *= 2
                gsc[d, r_off + r0:r_off + r0 + tile, :] = gc
                bsc[d, r_off + r0:r_off + r0 + tile, :] = jax.nn.sigmoid(b_b)

    ri = lax.broadcasted_iota(I32, (cs, cs), 0)
    ci = lax.broadcasted_iota(I32, (cs, cs), 1)
    low = (ri >= ci)
    up = (ri <= ci)
    eye = (ri == ci).astype(F32)
    offdiag = (ri != ci).astype(F32)
    assert cs in (GDN_SOLVE_BLOCK, 2 * GDN_SOLVE_BLOCK)
    same_blk = ((ri // GDN_SOLVE_BLOCK) == (ci // GDN_SOLVE_BLOCK)).astype(F32)
    n_levels = int(math.log2(min(cs, GDN_SOLVE_BLOCK))) - 1

    def prep_group(first_chunk, count):
        chunk_ids = [first_chunk + uu for uu in range(count)]
        rows = [pl.ds(_aligned(c * cs, cs), cs) for c in chunk_ids]
        q = [qs[r, :] for r in rows]
        k = [ks[r, :] for r in rows]
        v = [vs[r, :] for r in rows]
        kb = [x.astype(BF16) for x in k]
        kkt = [_dot_nt(x, x) for x in kb]
        qkt = [_dot_nt(a.astype(BF16), b) for a, b in zip(q, kb)]
        probs = [(ui, d) for ui in range(count) for d in range(2)]
        gc_col, beta_b, dmat, t, p, n_off = {}, {}, {}, {}, {}, {}
        for ui, d in probs:
            gc = gsc[d, rows[ui], :]
            bt = bsc[d, rows[ui], :]
            gsq = gc[:, :cs]
            dm = jnp.where(low if d == 0 else up, jnp.exp(gsq - gsq.T), 0.0)
            n = -(kkt[ui] * bt[:, :cs] * dm * offdiag)
            gc_col[ui, d], beta_b[ui, d], dmat[ui, d] = gc, bt, dm
            n_diag = n * same_blk
            t[ui, d] = eye + n_diag
            p[ui, d] = n_diag.astype(BF16)
            n_off[ui, d] = (n - n_diag).astype(BF16)
        p = {key: _dot(x, x).astype(BF16) for key, x in p.items()}
        for lvl in range(n_levels):
            t_next = {key: t[key] + _dot(p[key], t[key].astype(BF16)) for key in probs}
            if lvl + 1 < n_levels:
                p = {key: _dot(x, x).astype(BF16) for key, x in p.items()}
            t = t_next
        tb = {key: x.astype(BF16) for key, x in t.items()}
        if cs > GDN_SOLVE_BLOCK:
            x_off = {key: _dot(tb[key], n_off[key]).astype(BF16) for key in probs}
            t = {key: t[key] + _dot(x_off[key], tb[key]) for key in probs}
            tb = {key: x.astype(BF16) for key, x in t.items()}
        eg = {key: jnp.exp(x) for key, x in gc_col.items()}
        u = {(ui, d): _dot(tb[ui, d], (v[ui] * beta_b[ui, d]).astype(BF16)) for ui, d in probs}
        w = {(ui, d): _dot(tb[ui, d], (k[ui] * beta_b[ui, d] * eg[ui, d]).astype(BF16)) for ui, d in probs}
        for ui, d in probs:
            r = rows[ui]
            last = cs - 1 if d == 0 else 0
            gl = gc_col[ui, d][last:last + 1, :]
            u_sc[d, r, :] = u[ui, d]
            w16[d, r, :] = w[ui, d].astype(BF16)
            qk16[d, r, :] = (qkt[ui] * dmat[ui, d]).astype(BF16)
            qd16[d, r, :] = (q[ui] * eg[ui, d]).astype(BF16)
            kdk16[d, r, :] = (k[ui] * jnp.exp(gl - gc_col[ui, d])).astype(BF16)
            g0 = _aligned(chunk_ids[ui] * SUBLANES, SUBLANES)
            glast[d, pl.ds(g0, SUBLANES), :] = jnp.broadcast_to(jnp.exp(gl), (SUBLANES, LANES))

    def prep(gi, carry):
        prep_group(gi * GDN_PREP_UNROLL, GDN_PREP_UNROLL)
        return carry

    n_groups = n_chunks // GDN_PREP_UNROLL
    lax.fori_loop(0, n_groups, prep, 0)
    if n_chunks % GDN_PREP_UNROLL:
        prep_group(n_groups * GDN_PREP_UNROLL, n_chunks % GDN_PREP_UNROLL)

    oacc[...] = jnp.zeros_like(oacc)

    def step(i, carry):
        c_b = jnp.where(i < n_ctx_chunks, n_ctx_chunks - 1 - i, n_chunks + n_ctx_chunks - 1 - i)
        cid = (i, c_b)
        rows = [pl.ds(pl.multiple_of(c * cs, cs), cs) for c in cid]
        sb = [s.astype(BF16) for s in carry]
        ws = [_dot(w16[d, rows[d], :], sb[d]) for d in range(2)]
        qs_s = [_dot(qd16[d, rows[d], :], sb[d]) for d in range(2)]
        vnb = [(u_sc[d, rows[d], :] - ws[d]).astype(BF16) for d in range(2)]
        o = [qs_s[d] + _dot(qk16[d, rows[d], :], vnb[d]) for d in range(2)]
        kv = [_dot_tn(kdk16[d, rows[d], :], vnb[d]) for d in range(2)]
        s_new = []
        for d in range(2):
            g0 = pl.multiple_of(cid[d] * SUBLANES, SUBLANES)
            s_new.append(carry[d] * glast[d, pl.ds(g0, 1), :] + kv[d])
            oacc[rows[d], :] = oacc[rows[d], :] + o[d]
        return tuple(s_new)

    s0 = jnp.zeros((hd, GDN_DV), F32)
    lax.fori_loop(0, n_chunks, step, (s0, s0))

    onw = onorm_ref[...]

    def out_gate(o, z):
        y = o * lax.rsqrt(jnp.mean(o * o, axis=-1, keepdims=True) + NORM_EPS) * onw
        return (y * _silu(z.astype(F32))).astype(BF16)

    yc_ref[...] = out_gate(oacc[0:n_ctx, :], zc_ref[...])
    for r0 in range(0, n_lat, tile):
        yl_ref[r0:r0 + tile, :] = out_gate(oacc[n_ctx + r0:n_ctx + r0 + tile, :], zl_ref[r0:r0 + tile, :])


def _gdn(p, conv_w, a_log, dt_bias, out_norm, l, geo):
    b, n_ctx, n_lat = geo["b"], geo["ctx"], geo["lat"]
    n_all = n_ctx + n_lat
    lat0 = geo["tc"] // n_lat
    hd = GDN_DK

    def lat_spec(col):
        return pl.BlockSpec((n_lat, hd), lambda bi, h: (lat0 + bi, col // hd + h))

    def ctx_spec(col):
        return pl.BlockSpec((n_ctx, hd), lambda bi, h: (bi, col // hd + h))

    def fixed_lat(col):
        return pl.BlockSpec((n_lat, hd), lambda bi, h: (lat0 + bi, col // hd))

    def fixed_ctx(col):
        return pl.BlockSpec((n_ctx, hd), lambda bi, h: (bi, col // hd))

    def cw_spec(col):
        return pl.BlockSpec((None, GDN_CONV, hd), lambda bi, h: (l, 0, col // hd + h))

    smem = pl.BlockSpec(memory_space=pltpu.SMEM)
    kern = functools.partial(_gdn_kernel, n_ctx=n_ctx, n_lat=n_lat)
    n_chunks = n_all // GDN_CHUNK
    assert n_ctx % GDN_CHUNK == 0 and n_lat % GDN_CHUNK == 0
    scratch = [
        pltpu.VMEM((n_all + 4 * SUBLANES, hd), F32),
        pltpu.VMEM((n_all, hd), F32),
        pltpu.VMEM((n_all, hd), F32),
        pltpu.VMEM((n_all, hd), F32),
        pltpu.VMEM((2, n_all, LANES), F32),
        pltpu.VMEM((2, n_all, LANES), F32),
        pltpu.VMEM((2, n_all, GDN_DV), F32),
        pltpu.VMEM((2, n_all, hd), BF16),
        pltpu.VMEM((2, n_all, GDN_CHUNK), BF16),
        pltpu.VMEM((2, n_all, hd), BF16),
        pltpu.VMEM((2, n_all, hd), BF16),
        pltpu.VMEM((2, n_chunks * SUBLANES, LANES), F32),
        pltpu.VMEM((n_all, GDN_DV), F32),
    ]
    y_lat, y_ctx = pl.pallas_call(
        kern,
        out_shape=(jax.ShapeDtypeStruct((b * n_lat, GDN_HEADS * GDN_DV), BF16),
                   jax.ShapeDtypeStruct((b * n_ctx, GDN_HEADS * GDN_DV), BF16)),
        grid=(b, GDN_HEADS),
        in_specs=[smem, smem,
                  lat_spec(COL_Q), lat_spec(COL_K), lat_spec(COL_V), lat_spec(COL_Z), fixed_lat(COL_AB),
                  ctx_spec(COL_Q), ctx_spec(COL_K), ctx_spec(COL_V), ctx_spec(COL_Z), fixed_ctx(COL_AB),
                  cw_spec(0), cw_spec(GDN_HEADS * hd), cw_spec(2 * GDN_HEADS * hd),
                  pl.BlockSpec((None, 1, GDN_DV), lambda bi, h: (l, 0, 0))],
        out_specs=(pl.BlockSpec((n_lat, GDN_DV), lambda bi, h: (bi, h)),
                   pl.BlockSpec((n_ctx, GDN_DV), lambda bi, h: (bi, h))),
        scratch_shapes=scratch,
        compiler_params=_cparams(("arbitrary", "arbitrary")),
        name="gdn",
    )(a_log[l], dt_bias[l], p, p, p, p, p, p, p, p, p, p, conv_w, conv_w, conv_w, out_norm)
    return y_lat, y_ctx


def _head_rms(x, bd, w):
    ss = _dot(x * x, bd, precision=HIGHEST) * (1.0 / HEAD_DIM)
    return x * lax.rsqrt(ss + NORM_EPS) * w


def _rope(x, cos, sin_signed, lane):
    nf = HEAD_DIM // 4
    partner = jnp.where((lane % (2 * nf)) < nf, pltpu.roll(x, LANES - nf, 1), pltpu.roll(x, nf, 1))
    return x * cos + partner * sin_signed


def _attn_kernel(sink_ref, q_ref, kq_ref, vq_ref, kc_ref, vc_ref,
                 cosq_ref, sinq_ref, cosk_ref, sink_k_ref, qw_ref, kw_ref,
                 o_ref, kp, vp, kcp, vcp, *, n_keys, n_ctx, local):
    n = pl.program_id(1)
    w = WINDOW
    lane = lax.broadcasted_iota(I32, (1, LANES), 1)
    lo_mask = (lane < HEAD_DIM).astype(F32)
    hi_mask = 1.0 - lo_mask
    bd_r = lax.broadcasted_iota(I32, (LANES, LANES), 0) // HEAD_DIM
    bd_c = lax.broadcasted_iota(I32, (LANES, LANES), 1) // HEAD_DIM
    bd = (bd_r == bd_c).astype(F32)
    kw = kw_ref[...]

    def store_variants(dst, row0, x):
        xr = pltpu.roll(x, HEAD_DIM, 1)
        rows = x.shape[0]
        dst[0, row0:row0 + rows, :] = (x * lo_mask).astype(BF16)
        dst[1, row0:row0 + rows, :] = (xr * hi_mask).astype(BF16)
        dst[2, row0:row0 + rows, :] = (xr * lo_mask).astype(BF16)
        dst[3, row0:row0 + rows, :] = (x * hi_mask).astype(BF16)

    @pl.when(n == 0)
    def _prepare_keys():
        tile = 256
        for r0 in range(0, n_ctx, tile):
            rr = min(tile, n_ctx - r0)
            kc = _head_rms(kc_ref[r0:r0 + rr, :].astype(F32), bd, kw)
            store_variants(kcp, r0, kc)
            store_variants(vcp, r0, vc_ref[r0:r0 + rr, :].astype(F32))
        if local:
            zero = jnp.zeros((w, LANES), BF16)
            for t in range(4):
                kp[t, 0:w, :] = zero
                kp[t, w + n_keys:2 * w + n_keys, :] = zero
                vp[t, 0:w, :] = zero
                vp[t, w + n_keys:2 * w + n_keys, :] = zero
            for r0 in range(0, n_keys, tile):
                k = _head_rms(kq_ref[r0:r0 + tile, :].astype(F32), bd, kw)
                k = _rope(k, cosk_ref[r0:r0 + tile, :], sink_k_ref[r0:r0 + tile, :], lane)
                store_variants(kp, w + r0, k)
                store_variants(vp, w + r0, vq_ref[r0:r0 + tile, :].astype(F32))

    qw = qw_ref[...]
    q2 = []
    for g in range(ATT_HEADS // 2):
        qg = _head_rms(q_ref[:, g * LANES:(g + 1) * LANES].astype(F32), bd, qw)
        if local:
            qg = _rope(qg, cosq_ref[...], sinq_ref[...], lane)
        q2.append((qg * (HEAD_DIM ** -0.5)).astype(BF16))

    rows2 = 2 * w
    row_i = lax.broadcasted_iota(I32, (rows2, 1), 0)
    first = row_i < w
    if local:
        n_blk = n_keys // w
        qq = lax.broadcasted_iota(I32, (rows2, 3 * w), 0) % w
        kcol = lax.broadcasted_iota(I32, (rows2, 3 * w), 1)
        kk = kcol % w
        blk = kcol // w
        valid = ((blk == 1)
                 | ((blk == 0) & (kk >= qq) & (n >= 1))
                 | ((blk == 2) & (kk <= qq) & (n < n_blk - 1)))
        k0 = pl.multiple_of(n * w, w)

    for j in range(ATT_KV_HEADS):
        lhs = jnp.concatenate([q2[2 * j], q2[2 * j + 1]], axis=0)
        pair_out = [None, None]
        for t in range(2):
            var = 2 * j + t
            sink = jnp.where(first, sink_ref[4 * j + t], sink_ref[4 * j + 2 + t])
            s_ctx = _dot_nt(lhs, kcp[var])
            m = jnp.maximum(jnp.max(s_ctx, axis=-1, keepdims=True), sink)
            if local:
                s_loc = _dot_nt(lhs, kp[var, pl.ds(k0, 3 * w), :])
                s_loc = jnp.where(valid, s_loc, NEG_INF)
                m = jnp.maximum(m, jnp.max(s_loc, axis=-1, keepdims=True))
            e_ctx = jnp.exp(s_ctx - m)
            den = jnp.sum(e_ctx, axis=-1, keepdims=True) + jnp.exp(sink - m)
            pv = _dot(e_ctx.astype(BF16), vcp[var])
            if local:
                e_loc = jnp.exp(s_loc - m)
                den = den + jnp.sum(e_loc, axis=-1, keepdims=True)
                pv = pv + _dot(e_loc.astype(BF16), vp[var, pl.ds(k0, 3 * w), :])
            o = pv / den
            for half_i in range(2):
                part = o[half_i * w:(half_i + 1) * w, :]
                pair_out[half_i] = part if pair_out[half_i] is None else pair_out[half_i] + part
        for half_i in range(2):
            c0 = (2 * j + half_i) * LANES
            o_ref[:, c0:c0 + LANES] = pair_out[half_i].astype(BF16)


def _attention(p, sink, q_norm_w, k_norm_w, rope_cos, rope_sin, l, geo, ctx_queries):
    b, n_ctx, n_lat = geo["b"], geo["ctx"], geo["lat"]
    w = WINDOW
    lat0 = geo["tc"] // n_lat
    local = not ctx_queries
    n_q = n_ctx if ctx_queries else n_lat
    nqb = n_q // w
    q_row0 = 0 if ctx_queries else geo["tc"] // w
    kern = functools.partial(_attn_kernel, n_keys=n_lat, n_ctx=n_ctx, local=local)
    in_specs = [
        pl.BlockSpec(memory_space=pltpu.SMEM),
        pl.BlockSpec((w, ATT_HEADS * HEAD_DIM), lambda bi, n: (q_row0 + bi * nqb + n, COL_AQ // 512)),
        pl.BlockSpec((n_lat, LANES), lambda bi, n: (lat0 + bi, COL_AK // LANES)),
        pl.BlockSpec((n_lat, LANES), lambda bi, n: (lat0 + bi, COL_AV // LANES)),
        pl.BlockSpec((n_ctx, LANES), lambda bi, n: (bi, COL_AK // LANES)),
        pl.BlockSpec((n_ctx, LANES), lambda bi, n: (bi, COL_AV // LANES)),
        pl.BlockSpec((w, LANES), lambda bi, n: (n if local else 0, 0)),
        pl.BlockSpec((w, LANES), lambda bi, n: (n if local else 0, 0)),
        pl.BlockSpec((n_lat, LANES), lambda bi, n: (0, 0)),
        pl.BlockSpec((n_lat, LANES), lambda bi, n: (0, 0)),
        pl.BlockSpec((None, 1, LANES), lambda bi, n: (l, 0, 0)),
        pl.BlockSpec((None, 1, LANES), lambda bi, n: (l, 0, 0)),
    ]
    scratch = [
        pltpu.VMEM((4, n_lat + 2 * w, LANES), BF16),
        pltpu.VMEM((4, n_lat + 2 * w, LANES), BF16),
        pltpu.VMEM((4, n_ctx, LANES), BF16),
        pltpu.VMEM((4, n_ctx, LANES), BF16),
    ]
    return pl.pallas_call(
        kern,
        out_shape=jax.ShapeDtypeStruct((b * n_q, ATT_HEADS * HEAD_DIM), BF16),
        grid=(b, nqb),
        in_specs=in_specs,
        out_specs=pl.BlockSpec((w, ATT_HEADS * HEAD_DIM), lambda bi, n: (bi * nqb + n, 0)),
        scratch_shapes=scratch,
        compiler_params=_cparams(("arbitrary", "arbitrary")),
        name="attn_ctx" if ctx_queries else "attn_lat",
    )(sink[l], p, p, p, p, p, rope_cos, rope_sin, rope_cos, rope_sin, q_norm_w, k_norm_w)


def _merge_kernel(x_ref, mod_ref, n2_ref, y0c_ref, y0l_ref, y3c_ref, y3l_ref,
                  sb_ref, sc_ref, sx_ref, scp_ref, sxp_ref, scn_ref, sxn_ref,
                  gu_ref, gv_ref, mg0_ref, mg1_ref, mg2_ref, mg3_ref,
                  scw_ref, lnw_ref, lnb_ref, sgw_ref, sgb_ref, wb_ref, wo_ref, rw_ref, rb_ref,
                  xo_ref, h2_ref, idx_ref, gate_ref, rank_ref, cnt_ref,
                  carry, *, tm, tc, n_ctx, n_lat):
    i = pl.program_id(0)
    r0 = i * tm

    row = lax.broadcasted_iota(I32, (tm, 1), 0)
    g_row = r0 + row
    in_ctx = g_row < tc
    seg_pos = jnp.where(in_ctx, g_row % n_ctx, (g_row - tc) % n_lat)
    seg_len = jnp.where(in_ctx, n_ctx, n_lat)
    cx = sc_ref[...].astype(F32) * sx_ref[...].astype(F32)
    cx_prev_halo = scp_ref[SUBLANES - 1:SUBLANES, :].astype(F32) * sxp_ref[SUBLANES - 1:SUBLANES, :].astype(F32)
    cx_next_halo = scn_ref[0:1, :].astype(F32) * sxn_ref[0:1, :].astype(F32)
    prev = jnp.where(row == 0, cx_prev_halo, pltpu.roll(cx, 1, 0))
    prev = jnp.where(seg_pos == 0, 0.0, prev)
    nxt = jnp.where(row == tm - 1, cx_next_halo, pltpu.roll(cx, tm - 1, 0))
    nxt = jnp.where(seg_pos == seg_len - 1, 0.0, nxt)
    scw = scw_ref[...]
    y1 = sb_ref[...].astype(F32) * (prev * scw[0:1, :] + cx * scw[1:2, :] + nxt * scw[2:3, :])

    inv_sqrt2 = 1.0 / math.sqrt(2.0)

    def gelu(t):
        return 0.5 * t * (1.0 + lax.erf(t * inv_sqrt2))

    u = gelu(gu_ref[...].astype(F32))
    v = gelu(gv_ref[...].astype(F32))
    mu = jnp.mean(v, axis=-1, keepdims=True)
    vc = v - mu
    v = vc * lax.rsqrt(jnp.mean(vc * vc, axis=-1, keepdims=True) + LN_EPS) * lnw_ref[...] + lnb_ref[...]
    vb = v.astype(BF16)
    gw = SG_WIDTH // SG_GROUPS
    chunks = []
    for c in range(tm // SG_CHUNK):
        groups = []
        for g in range(SG_GROUPS):
            groups.append(_dot(sgw_ref[g], vb[c * SG_CHUNK:(c + 1) * SG_CHUNK, g * gw:(g + 1) * gw]))
        chunks.append(jnp.concatenate(groups, axis=1) + sgb_ref[...])
    y2 = u * jnp.concatenate(chunks, axis=0)

    tile_in_ctx = r0 < tc
    y0 = jnp.where(tile_in_ctx, y0c_ref[...], y0l_ref[...])
    y3 = jnp.where(tile_in_ctx, y3c_ref[...], y3l_ref[...])
    ys = (y0, y1.astype(BF16), y2.astype(BF16), y3)
    gates = (mg0_ref, mg1_ref, mg2_ref, mg3_ref)
    m = None
    for br in range(N_BRANCH):
        term = jax.nn.sigmoid(gates[br][...].astype(F32)) * _dot(ys[br], wb_ref[br])
        m = term if m is None else m + term
    y = _dot(m.astype(BF16), wo_ref[...])
    x_new = x_ref[...] + mod_ref[2:3, :] * y
    xo_ref[...] = x_new

    ms = jnp.mean(x_new * x_new, axis=-1, keepdims=True)
    h2 = x_new * lax.rsqrt(ms + NORM_EPS) * n2_ref[...] * (1.0 + mod_ref[4:5, :]) + mod_ref[3:4, :]
    half = D_MODEL // 2
    _store_planes(h2_ref, _pack_bf16_pair(h2[:, :half], h2[:, half:]))

    h2_hi = h2.astype(BF16)
    h2_lo = (h2 - h2_hi.astype(F32)).astype(BF16)
    prod = _dot(jnp.concatenate([h2_hi, h2_lo], axis=1), rw_ref[...])
    lane = lax.broadcasted_iota(I32, (tm, LANES), 1)
    logits = prod + pltpu.roll(prod, LANES - N_EXPERTS, 1)
    logits = jnp.where(lane < N_EXPERTS, logits, NEG_INF) + rb_ref[...]
    lane_f = lane.astype(F32)
    work = logits
    topv = jnp.full((tm, LANES), NEG_INF, F32)
    topi = jnp.zeros((tm, LANES), I32)
    onehot = jnp.zeros((tm, LANES), F32)
    firsts = []
    for k in range(TOP_K):
        mx = jnp.max(work, axis=-1, keepdims=True)
        first_f = jnp.min(jnp.where(work == mx, lane_f, float(LANES)), axis=-1, keepdims=True)
        first = first_f.astype(I32)
        hit = lane == first
        topv = jnp.where(lane == k, mx, topv)
        topi = jnp.where(lane == k, first, topi)
        onehot = jnp.where(hit, 1.0, onehot)
        work = jnp.where(hit, -jnp.inf, work)
        firsts.append(first)
    e = jnp.where(lane < TOP_K, jnp.exp(topv - jnp.max(topv, axis=-1, keepdims=True)), 0.0)
    gate = e / jnp.sum(e, axis=-1, keepdims=True)

    @pl.when(i == 0)
    def _init():
        carry[...] = jnp.zeros_like(carry)

    tri = (lax.broadcasted_iota(I32, (tm, tm), 0) > lax.broadcasted_iota(I32, (tm, tm), 1)).astype(BF16)
    before = carry[0:1, :] + _dot(tri, onehot.astype(BF16))
    rank = jnp.zeros((tm, LANES), F32)
    for k in range(TOP_K):
        r_k = jnp.sum(jnp.where(lane == firsts[k], before, 0.0), axis=-1, keepdims=True)
        rank = jnp.where(lane == k, r_k, rank)
    new_carry = carry[0:1, :] + jnp.sum(onehot, axis=0, keepdims=True)
    carry[...] = jnp.broadcast_to(new_carry, carry.shape)
    cnt_ref[...] = jnp.broadcast_to(new_carry, cnt_ref.shape).astype(I32)
    idx_ref[...] = topi[:, :SUBLANES]
    gate_ref[...] = gate[:, :SUBLANES]
    rank_ref[...] = rank[:, :SUBLANES].astype(I32)


def _merge(x, mod, norm2, y0_ctx, y0_lat, y3_ctx, y3_lat, p, w, l, geo):
    t, dm = x.shape
    tm = geo["tm_merge"]
    mod_row = geo["mod_row"]
    kern = functools.partial(_merge_kernel, tm=tm, tc=geo["tc"], n_ctx=geo["ctx"], n_lat=geo["lat"])
    halo = tm // SUBLANES
    n_halo = t // SUBLANES

    def col(c, width=512):
        return pl.BlockSpec((tm, width), lambda i: (i, c // width))

    def prev_halo(c):
        return pl.BlockSpec((SUBLANES, 512), lambda i: (jnp.maximum(i * halo - 1, 0), c // 512))

    def next_halo(c):
        return pl.BlockSpec((SUBLANES, 512), lambda i: (jnp.minimum((i + 1) * halo, n_halo - 1), c // 512))

    def layer(shape):
        nd = len(shape)
        return pl.BlockSpec((None,) + shape, lambda i: (l,) + (0,) * nd)

    n_ctx_tiles = geo["tc"] // tm

    def ctx_rows():
        return pl.BlockSpec((tm, 512), lambda i: (jnp.minimum(i, n_ctx_tiles - 1), 0))

    def lat_rows():
        return pl.BlockSpec((tm, 512), lambda i: (jnp.maximum(i - n_ctx_tiles, 0), 0))

    in_specs = [
        pl.BlockSpec((tm, dm), lambda i: (i, 0)),
        pl.BlockSpec((None, None, N_MOD, dm), lambda i: (l, mod_row(i * tm), 0, 0)),
        layer((1, dm)),
        ctx_rows(), lat_rows(), ctx_rows(), lat_rows(),
        col(COL_SB), col(COL_SC), col(COL_SX),
        prev_halo(COL_SC), prev_halo(COL_SX), next_halo(COL_SC), next_halo(COL_SX),
        col(COL_GU), col(COL_GV),
        col(COL_MG, 1024), col(COL_MG + 1024, 1024), col(COL_MG + 2048, 1024), col(COL_MG + 3072, 1024),
        layer((SC_CONV, SC_WIDTH)), layer((1, SG_WIDTH)), layer((1, SG_WIDTH)),
        layer((SG_GROUPS, SG_CHUNK, SG_CHUNK)), layer((SG_CHUNK, SG_WIDTH)),
        layer((N_BRANCH, BRANCH_W, dm)), layer((dm, dm)), layer((2 * dm, LANES)), layer((1, LANES)),
    ]
    out_shape = (
        jax.ShapeDtypeStruct((t, dm), F32),
        jax.ShapeDtypeStruct((SC_PLANES, t, SC_ROW), U32),
        jax.ShapeDtypeStruct((t, SUBLANES), I32),
        jax.ShapeDtypeStruct((t, SUBLANES), F32),
        jax.ShapeDtypeStruct((t, SUBLANES), I32),
        jax.ShapeDtypeStruct((SUBLANES, LANES), I32),
    )
    out_specs = (
        pl.BlockSpec((tm, dm), lambda i: (i, 0)),
        pl.BlockSpec((SC_PLANES, tm, SC_ROW), lambda i: (0, i, 0)),
        pl.BlockSpec((tm, SUBLANES), lambda i: (i, 0)),
        pl.BlockSpec((tm, SUBLANES), lambda i: (i, 0)),
        pl.BlockSpec((tm, SUBLANES), lambda i: (i, 0)),
        pl.BlockSpec((SUBLANES, LANES), lambda i: (0, 0)),
    )
    return pl.pallas_call(
        kern,
        out_shape=out_shape,
        grid=(t // tm,),
        in_specs=in_specs,
        out_specs=out_specs,
        scratch_shapes=[pltpu.VMEM((SUBLANES, LANES), F32)],
        compiler_params=_cparams(("arbitrary",)),
        name="merge",
    )(x, mod, norm2, y0_ctx, y0_lat, y3_ctx, y3_lat, p, p, p, p, p, p, p, p, p, p, p, p, p,
      w["sc_conv"], w["sg_ln_w"], w["sg_ln_b"], w["sg_w"], w["sg_b"], w["w_branch"], w["w_out"],
      w["router_w"], w["router_b"])


def _sc_mesh():
    return plsc.VectorSubcoreMesh(core_axis_name="core", subcore_axis_name="subcore")


def _sc_scatter_rows(src, idx, n_out):
    n, width = src.shape

    @pl.kernel(out_type=jax.ShapeDtypeStruct((n_out, width), U32), mesh=_sc_mesh(), scratch_types=[])
    def scatter(x_hbm, i_hbm, o_hbm):
        def body(x_vmem, *i_vmem):
            for iv in i_vmem:
                pltpu.sync_copy(x_vmem, o_hbm.at[iv.at[0]])

        pltpu.emit_pipeline(
            body,
            grid=(n // SC_WINDOW,),
            in_specs=[pl.BlockSpec((SC_WINDOW, width), lambda i: (i, 0))]
            + [pl.BlockSpec((1, SC_WINDOW), functools.partial(lambda i, k: (k, i), k=k)) for k in range(TOP_K)],
            out_specs=[],
            core_axis_name=("core", "subcore"),
            dimension_semantics=(pltpu.PARALLEL,),
        )(x_hbm, *([i_hbm] * TOP_K))

    return scatter(src, idx)


def _sc_gather_rows(src, idx):
    n = idx.shape[0]
    width = src.shape[1]

    @pl.kernel(out_type=jax.ShapeDtypeStruct((n, width), U32), mesh=_sc_mesh(), scratch_types=[])
    def gather(x_hbm, i_hbm, o_hbm):
        def body(i_vmem, o_vmem):
            pltpu.sync_copy(x_hbm.at[i_vmem.at[0]], o_vmem)

        pltpu.emit_pipeline(
            body,
            grid=(n // SC_WINDOW,),
            in_specs=[pl.BlockSpec((1, SC_WINDOW), lambda i: (0, i))],
            out_specs=[pl.BlockSpec((SC_WINDOW, width), lambda i: (i, 0))],
            core_axis_name=("core", "subcore"),
            dimension_semantics=(pltpu.PARALLEL,),
        )(i_hbm, o_hbm)

    return gather(src, idx.reshape(1, n))


def _expert_kernel(be_ref, bv_ref, xs_ref, w1_ref, b1_ref, w2_ref, b2_ref, ys_ref):
    del be_ref
    valid = lax.broadcasted_iota(I32, (MOE_BM, 1), 0) < bv_ref[pl.program_id(0)]
    lo, hi = _unpack_bf16_pair(jnp.where(valid, _load_planes(xs_ref), jnp.uint32(0)))
    half = D_MODEL // 2
    h = (_dot(lo.astype(BF16), w1_ref[0:half, :]) + _dot(hi.astype(BF16), w1_ref[half:, :]) + b1_ref[...])
    hg = W1_GROUP // 2
    acts = []
    for g in range(2 * D_EXPERT // W1_GROUP):
        glu = jnp.minimum(h[:, g * W1_GROUP:g * W1_GROUP + hg], SWIGLU_LIMIT)
        lin = jnp.clip(h[:, g * W1_GROUP + hg:(g + 1) * W1_GROUP], -SWIGLU_LIMIT, SWIGLU_LIMIT)
        acts.append((glu * jax.nn.sigmoid(SWIGLU_ALPHA * glu) * (lin + 1.0)).astype(BF16))
    y = _dot(jnp.concatenate(acts, axis=1), w2_ref[...]) + b2_ref[...]
    _store_planes(ys_ref, _pack_bf16_pair(y[:, :half], y[:, half:]))


def _w1_regroup_kernel(w_ref, o_ref):
    hg = W1_GROUP // 2
    r = lax.broadcasted_iota(I32, (W1_GROUP, W1_GROUP), 0)
    c = lax.broadcasted_iota(I32, (W1_GROUP, W1_GROUP), 1)
    perm = (r == jnp.where(c < hg, 2 * c, 2 * (c - hg) + 1)).astype(BF16)
    for g in range(2 * D_EXPERT // W1_GROUP):
        sl = slice(g * W1_GROUP, (g + 1) * W1_GROUP)
        o_ref[:, sl] = _dot(w_ref[:, sl].astype(BF16), perm).astype(BF16)


def _w1_regroup(exp_w1):
    depth, ne, dm, dh = exp_w1.shape
    tr = 512
    out = pl.pallas_call(
        _w1_regroup_kernel,
        out_shape=jax.ShapeDtypeStruct((depth * ne, dm, dh), BF16),
        grid=(depth * ne, dm // tr),
        in_specs=[pl.BlockSpec((None, tr, dh), lambda e, i: (e, i, 0))],
        out_specs=pl.BlockSpec((None, tr, dh), lambda e, i: (e, i, 0)),
        compiler_params=_cparams(("arbitrary", "arbitrary")),
        name="w1_regroup",
    )(exp_w1.reshape(depth * ne, dm, dh))
    return out.reshape(depth, ne, dm, dh)


def _experts(xs, block_expert, block_valid, w1p, b1p, w2, b2, l):
    n_slots = xs.shape[1]
    bm = MOE_BM
    grid_spec = pltpu.PrefetchScalarGridSpec(
        num_scalar_prefetch=2,
        grid=(n_slots // bm,),
        in_specs=[
            pl.BlockSpec((SC_PLANES, bm, SC_ROW), lambda i, be, bv: (0, i, 0)),
            pl.BlockSpec((None, None, D_MODEL, 2 * D_EXPERT), lambda i, be, bv: (l, be[i], 0, 0)),
            pl.BlockSpec((None, None, 1, 2 * D_EXPERT), lambda i, be, bv: (l, be[i], 0, 0)),
            pl.BlockSpec((None, None, D_EXPERT, D_MODEL), lambda i, be, bv: (l, be[i], 0, 0)),
            pl.BlockSpec((None, None, 1, D_MODEL), lambda i, be, bv: (l, be[i], 0, 0)),
        ],
        out_specs=pl.BlockSpec((SC_PLANES, bm, SC_ROW), lambda i, be, bv: (0, i, 0)),
    )
    return pl.pallas_call(
        _expert_kernel,
        out_shape=jax.ShapeDtypeStruct((SC_PLANES, n_slots, SC_ROW), U32),
        grid_spec=grid_spec,
        compiler_params=_cparams(("arbitrary",)),
        name="moe_experts",
    )(block_expert, block_valid, xs, w1p, b1p, w2, b2)


def _combine_kernel(gate_ref, x_ref, mod_ref, yg_ref, xo_ref, *, tm):
    gate = gate_ref[...]
    half = D_MODEL // 2
    f_lo = jnp.zeros((tm, half), F32)
    f_hi = jnp.zeros((tm, half), F32)
    for k in range(TOP_K):
        lo, hi = _unpack_bf16_pair(_load_planes(yg_ref, k * SC_PLANES))
        g = gate[:, k:k + 1]
        f_lo = f_lo + g * lo
        f_hi = f_hi + g * hi
    g2 = mod_ref[5:6, :]
    xo_ref[:, :half] = x_ref[:, :half] + g2[:, :half] * f_lo
    xo_ref[:, half:] = x_ref[:, half:] + g2[:, half:] * f_hi


def _combine(x, mod, gate, yg, l, geo, first_row=0):
    t, dm = x.shape
    tm = geo["tm_moe"]
    mod_row = geo["mod_row"]
    i0 = first_row // tm
    return pl.pallas_call(
        functools.partial(_combine_kernel, tm=tm),
        out_shape=jax.ShapeDtypeStruct((t - first_row, dm), F32),
        grid=((t - first_row) // tm,),
        in_specs=[
            pl.BlockSpec((tm, SUBLANES), lambda i: (i0 + i, 0)),
            pl.BlockSpec((tm, dm), lambda i: (i0 + i, 0)),
            pl.BlockSpec((None, None, N_MOD, dm), lambda i: (l, mod_row((i0 + i) * tm), 0, 0)),
            pl.BlockSpec((TOP_K * SC_PLANES, tm, SC_ROW), lambda i: (0, i0 + i, 0)),
        ],
        out_specs=pl.BlockSpec((tm, dm), lambda i: (i, 0)),
        compiler_params=_cparams(("arbitrary",)),
        name="moe_combine",
    )(gate, x, mod, yg)


def _moe_plan(idx, rank, counts, n_blocks):
    bm = MOE_BM
    padded = (counts + bm - 1) // bm * bm
    pad_end = jnp.cumsum(padded)
    pad_start = pad_end - padded
    expert = idx[:, :TOP_K]
    onehot = expert[:, :, None] == jnp.arange(N_EXPERTS, dtype=I32)[None, None, :]
    dest = rank[:, :TOP_K] + jnp.sum(jnp.where(onehot, pad_start[None, None, :], 0), axis=-1)
    block_start = jnp.arange(n_blocks, dtype=I32) * bm
    block_expert = jnp.minimum(
        jnp.sum((block_start[:, None] >= pad_end[None, :]).astype(I32), axis=-1), N_EXPERTS - 1)
    block_valid = jnp.clip(counts[block_expert] - (block_start - pad_start[block_expert]), 0, bm)
    plane_off = jnp.arange(SC_PLANES, dtype=I32) * (n_blocks * bm)
    dest_rows = jnp.transpose(dest.astype(I32))[:, None, :] + plane_off[None, :, None]
    scatter_idx = dest_rows.reshape(TOP_K, -1)
    gather_idx = dest_rows.reshape(-1)
    return scatter_idx, gather_idx, block_expert.astype(I32), block_valid.astype(I32)


def _rope_tables(n_lat):
    rows = n_lat // GRID_W
    row = jnp.repeat(jnp.arange(rows, dtype=F32), GRID_W)
    colp = jnp.tile(jnp.arange(GRID_W, dtype=F32), rows)
    n_freq = HEAD_DIM // 4
    inv = ROPE_BASE ** (-jnp.arange(n_freq, dtype=F32) / n_freq)
    ang = jnp.concatenate([row[:, None] * inv, colp[:, None] * inv], axis=-1)
    cos, sin = jnp.cos(ang), jnp.sin(ang)
    cr, cc = cos[:, :n_freq], cos[:, n_freq:]
    sr, sc = sin[:, :n_freq], sin[:, n_freq:]
    cos64 = jnp.concatenate([cr, cr, cc, cc], axis=-1)
    sin64 = jnp.concatenate([-sr, sr, -sc, sc], axis=-1)
    return jnp.tile(cos64, (1, 2)), jnp.tile(sin64, (1, 2))


def _prep_weights(w_in, w_branch, w_out, router_w, router_b, sg_b, exp_w1, exp_b1, exp_w2, exp_b2,
                  attn_q_norm, attn_k_norm):
    depth, dm, _ = w_in.shape
    qkvz = 2048
    ab0, sc0, sg0, at0, mg0 = 2048, 2064, 3600, 4624, 5392
    pad = PROJ_COLS - (COL_AB + 16)
    w_p = jnp.concatenate([
        w_in[:, :, 0:qkvz],
        w_in[:, :, sc0:sg0],
        w_in[:, :, sg0:at0],
        w_in[:, :, at0:at0 + 512],
        w_in[:, :, mg0:mg0 + 4096],
        w_in[:, :, at0 + 512:at0 + 768],
        w_in[:, :, ab0:ab0 + 16],
        jnp.zeros((depth, dm, pad), w_in.dtype),
    ], axis=-1).astype(BF16)
    rw_hi = router_w.astype(BF16)
    rw_lo = (router_w - rw_hi.astype(F32)).astype(BF16)
    rw = jnp.concatenate([
        jnp.concatenate([rw_hi, rw_lo, jnp.zeros((depth, dm, LANES - 2 * N_EXPERTS), BF16)], axis=-1),
        jnp.concatenate([rw_hi, jnp.zeros((depth, dm, LANES - N_EXPERTS), BF16)], axis=-1),
    ], axis=1)
    rb = jnp.concatenate([router_b, jnp.full((depth, LANES - N_EXPERTS), NEG_INF, F32)], axis=-1)
    w1p = _w1_regroup(exp_w1)
    ne = exp_b1.shape[1]
    b1p = jnp.swapaxes(exp_b1.reshape(depth, ne, -1, W1_GROUP // 2, 2), -1, -2).reshape(depth, ne, -1)
    return {
        "w_in": w_p,
        "w_branch": w_branch.astype(BF16),
        "w_out": w_out.astype(BF16),
        "router_w": rw,
        "router_b": rb[:, None, :],
        "sg_b": jnp.repeat(jnp.swapaxes(sg_b, 1, 2), SG_WIDTH // SG_GROUPS, axis=2),
        "w1": w1p,
        "b1": b1p[:, :, None, :],
        "w2": exp_w2.astype(BF16),
        "b2": exp_b2[:, :, None, :],
        "q_norm": jnp.tile(attn_q_norm, (1, 2))[:, None, :],
        "k_norm": jnp.tile(attn_k_norm, (1, 2))[:, None, :],
    }


def kernel(x, c, ctx, c_ctx, norm1, norm2, w_mod, b_mod, w_in, gdn_conv, gdn_a_log, gdn_dt_bias, gdn_out_norm,
           sc_conv, sg_ln_w, sg_ln_b, sg_w, sg_b, attn_q_norm, attn_k_norm, attn_sink, w_branch, w_out,
           router_w, router_b, exp_w1, exp_b1, exp_w2, exp_b2):
    b, n_lat, dm = x.shape
    n_ctx = ctx.shape[1]
    depth = w_in.shape[0]
    tc = b * n_ctx
    t = tc + b * n_lat
    assert dm == D_MODEL and tc % n_lat == 0 and n_lat % 256 == 0 and n_ctx % 256 == 0
    tile_cap = math.gcd(tc, n_lat)

    def mod_row(r0):
        return jnp.where(r0 < tc, 0, 1 + (r0 - tc) // n_lat)

    geo = {
        "b": b, "ctx": n_ctx, "lat": n_lat, "tc": tc, "mod_row": mod_row,
        "tm_proj": min(1024, tile_cap), "tm_merge": min(512, tile_cap), "tm_moe": min(512, tile_cap),
    }

    wts = _prep_weights(w_in, w_branch, w_out, router_w, router_b, sg_b, exp_w1, exp_b1, exp_w2, exp_b2,
                        attn_q_norm, attn_k_norm)
    rope_cos, rope_sin = _rope_tables(n_lat)

    mod_rows = -(-(1 + b) // SUBLANES) * SUBLANES
    c_all = jnp.concatenate([c_ctx[None, :], c, jnp.zeros((mod_rows - 1 - b, dm), F32)], axis=0)
    mod = _modulation(c_all, w_mod, b_mod).reshape(depth, mod_rows, N_MOD, dm)

    xs_flat = jnp.concatenate([ctx.reshape(tc, dm), x.reshape(b * n_lat, dm)], axis=0)
    n_blocks = -(-(t * TOP_K) // MOE_BM) + N_EXPERTS
    n_slots = n_blocks * MOE_BM

    for l in range(depth):
        last = l == depth - 1
        p = _in_proj(xs_flat, mod, norm1[:, None, :], wts["w_in"], l, geo)
        g_lat, g_ctx = _gdn(p, gdn_conv, gdn_a_log, gdn_dt_bias, gdn_out_norm[:, None, :], l, geo)
        a_lat = _attention(p, attn_sink, wts["q_norm"], wts["k_norm"], rope_cos, rope_sin, l, geo, False)
        if last:
            a_ctx = jnp.zeros((tc, ATT_HEADS * HEAD_DIM), BF16)
        else:
            a_ctx = _attention(p, attn_sink, wts["q_norm"], wts["k_norm"], rope_cos, rope_sin, l, geo, True)
        layer_w = {
            "sc_conv": sc_conv, "sg_ln_w": sg_ln_w[:, None, :], "sg_ln_b": sg_ln_b[:, None, :], "sg_w": sg_w.astype(BF16),
            "sg_b": wts["sg_b"], "w_branch": wts["w_branch"], "w_out": wts["w_out"],
            "router_w": wts["router_w"], "router_b": wts["router_b"],
        }
        x_mid, h2p, idx, gate, rank, counts = _merge(xs_flat, mod, norm2[:, None, :], g_ctx, g_lat, a_ctx, a_lat, p,
                                                          layer_w, l, geo)
        scatter_idx, gather_idx, block_expert, block_valid = _moe_plan(idx, rank, counts[0, :N_EXPERTS], n_blocks)
        xs_sorted = _sc_scatter_rows(h2p.reshape(SC_PLANES * t, SC_ROW), scatter_idx, SC_PLANES * n_slots)
        ys_sorted = _experts(xs_sorted.reshape(SC_PLANES, n_slots, SC_ROW), block_expert, block_valid,
                             wts["w1"], wts["b1"], wts["w2"], wts["b2"], l)
        yg = _sc_gather_rows(ys_sorted.reshape(SC_PLANES * n_slots, SC_ROW), gather_idx)
        xs_flat = _combine(x_mid, mod, gate, yg.reshape(TOP_K * SC_PLANES, t, SC_ROW), l, geo,
                           first_row=tc if last else 0)

    return xs_flat.reshape(b, n_lat, dm)
```

```python
import functools
import math

import jax
import jax.numpy as jnp
from jax import lax
from jax.experimental import pallas as pl
from jax.experimental.pallas import tpu as pltpu
from jax.experimental.pallas import tpu_sc as plsc

F32 = jnp.float32
BF16 = jnp.bfloat16
I32 = jnp.int32
U32 = jnp.uint32
HIGHEST = lax.Precision.HIGHEST

D_MODEL = 1024
GRID_W = 64
N_MOD = 6
GDN_HEADS = 4
GDN_DK = 128
GDN_DV = 128
GDN_CONV = 5
SC_WIDTH = 512
SC_CONV = 3
SG_CHUNK = 128
SG_GROUPS = 4
SG_WIDTH = 512
ATT_HEADS = 8
ATT_KV_HEADS = 2
HEAD_DIM = 64
WINDOW = 128
ROPE_BASE = 10000.0
N_BRANCH = 4
BRANCH_W = 512
N_EXPERTS = 32
TOP_K = 4
D_EXPERT = 1024
SWIGLU_ALPHA = 1.702
SWIGLU_LIMIT = 7.0
NORM_EPS = 1e-6
LN_EPS = 1e-5
NEG_INF = -1e30

LANES = 128
SUBLANES = 8
VMEM_LIMIT_BYTES = 56 * 2**20

COL_Q = 0
COL_K = 512
COL_V = 1024
COL_Z = 1536
COL_SB = 2048
COL_SC = 2560
COL_SX = 3072
COL_GU = 3584
COL_GV = 4096
COL_AQ = 4608
COL_MG = 5120
COL_AK = 9216
COL_AV = 9344
COL_AB = 9472
PROJ_COLS = 9728
PROJ_TN = 4864
PROJ_CHUNK = 256

GDN_CHUNK = 128
GDN_SOLVE_BLOCK = 64
GDN_PREP_UNROLL = 8
MOE_BM = 512
W1_GROUP = 256
SC_ROW = 256
SC_WINDOW = 128
SC_PLANES = (D_MODEL // 2) // SC_ROW


def _cparams(sem):
    return pltpu.CompilerParams(dimension_semantics=sem, vmem_limit_bytes=VMEM_LIMIT_BYTES)


def _silu(x):
    return x * jax.nn.sigmoid(x)


def _dot(a, b, precision=None):
    return jnp.dot(a, b, preferred_element_type=F32, precision=precision)


def _dot_nt(a, b):
    return lax.dot_general(a, b, (((1,), (1,)), ((), ())), preferred_element_type=F32)


def _dot_tn(a, b):
    return lax.dot_general(a, b, (((0,), (0,)), ((), ())), preferred_element_type=F32)


def _aligned(start, multiple):
    return start if isinstance(start, int) else pl.multiple_of(start, multiple)


def _pack_bf16_pair(lo, hi):
    lo_b = pltpu.bitcast(lo.astype(BF16).astype(F32), U32)
    hi_b = pltpu.bitcast(hi.astype(BF16).astype(F32), U32)
    return (hi_b & jnp.uint32(0xFFFF0000)) | (lo_b >> 16)


def _unpack_bf16_pair(u):
    lo = pltpu.bitcast(u << 16, F32)
    hi = pltpu.bitcast(u & jnp.uint32(0xFFFF0000), F32)
    return lo, hi


def _store_planes(ref, packed):
    for p in range(SC_PLANES):
        ref[p] = packed[:, p * SC_ROW:(p + 1) * SC_ROW]


def _load_planes(ref, first=0):
    return jnp.concatenate([ref[first + p] for p in range(SC_PLANES)], axis=1)


def _mod_kernel(c_ref, w_ref, b_ref, o_ref):
    s = _silu(c_ref[...])
    o_ref[...] = _dot(s, w_ref[...], precision=HIGHEST) + b_ref[...]


def _modulation(c_all, w_mod, b_mod):
    depth, dm, nm = w_mod.shape
    rows = c_all.shape[0]
    tn = 1536
    return pl.pallas_call(
        _mod_kernel,
        out_shape=jax.ShapeDtypeStruct((depth, rows, nm), F32),
        grid=(depth, nm // tn),
        in_specs=[
            pl.BlockSpec((rows, dm), lambda l, j: (0, 0)),
            pl.BlockSpec((None, dm, tn), lambda l, j: (l, 0, j)),
            pl.BlockSpec((None, 1, tn), lambda l, j: (l, 0, j)),
        ],
        out_specs=pl.BlockSpec((None, rows, tn), lambda l, j: (l, 0, j)),
        compiler_params=_cparams(("arbitrary", "arbitrary")),
        name="modulation",
    )(c_all, w_mod, b_mod.reshape(depth, 1, nm))


def _in_proj_kernel(x_ref, mod_ref, nw_ref, w_ref, o_ref):
    x = x_ref[...]
    ms = jnp.mean(x * x, axis=-1, keepdims=True)
    y = x * lax.rsqrt(ms + NORM_EPS) * nw_ref[...]
    h = (y * (1.0 + mod_ref[1:2, :]) + mod_ref[0:1, :]).astype(BF16)
    for c in range(PROJ_TN // PROJ_CHUNK):
        sl = slice(c * PROJ_CHUNK, (c + 1) * PROJ_CHUNK)
        o_ref[:, sl] = _dot(h, w_ref[:, sl]).astype(BF16)


def _in_proj(x, mod, norm_w, w_p, l, geo):
    t, dm = x.shape
    tm = geo["tm_proj"]
    mod_row = geo["mod_row"]
    return pl.pallas_call(
        _in_proj_kernel,
        out_shape=jax.ShapeDtypeStruct((t, PROJ_COLS), BF16),
        grid=(PROJ_COLS // PROJ_TN, t // tm),
        in_specs=[
            pl.BlockSpec((tm, dm), lambda j, i: (i, 0)),
            pl.BlockSpec((None, None, N_MOD, dm), lambda j, i: (l, mod_row(i * tm), 0, 0)),
            pl.BlockSpec((None, 1, dm), lambda j, i: (l, 0, 0)),
            pl.BlockSpec((None, dm, PROJ_TN), lambda j, i: (l, 0, j)),
        ],
        out_specs=pl.BlockSpec((tm, PROJ_TN), lambda j, i: (i, j)),
        compiler_params=_cparams(("arbitrary", "arbitrary")),
        name="in_proj",
    )(x, mod, norm_w, w_p)


def _softplus(x):
    return jnp.maximum(x, 0.0) + jnp.log1p(jnp.exp(-jnp.abs(x)))


def _gdn_kernel(alog_ref, dtb_ref,
                ql_ref, kl_ref, vl_ref, zl_ref, abl_ref,
                qc_ref, kc_ref, vc_ref, zc_ref, abc_ref,
                cwq_ref, cwk_ref, cwv_ref, onorm_ref,
                yl_ref, yc_ref,
                xpad, qs, ks, vs, gsc, bsc,
                u_sc, w16, qk16, qd16, kdk16, glast, oacc,
                *, n_ctx, n_lat):
    cs = GDN_CHUNK
    hd = GDN_DK
    n_all = n_ctx + n_lat
    n_chunks = n_all // cs
    n_ctx_chunks = n_ctx // cs
    h = pl.program_id(1)
    pad = SUBLANES
    lat_off = n_ctx + 3 * pad

    zeros_pad = jnp.zeros((pad, hd), F32)
    xpad[0:pad, :] = zeros_pad
    xpad[pad + n_ctx:pad + n_ctx + 2 * pad, :] = jnp.zeros((2 * pad, hd), F32)
    xpad[lat_off + n_lat:lat_off + n_lat + pad, :] = zeros_pad

    half = GDN_CONV // 2
    tile = 256

    def conv_into(src_c, src_l, cw_ref, dst, mode):
        xpad[pad:pad + n_ctx, :] = src_c[...].astype(F32)
        xpad[lat_off:lat_off + n_lat, :] = src_l[...].astype(F32)
        w = cw_ref[...]
        for seg_off, dst_off, seg_len in ((pad, 0, n_ctx), (lat_off, n_ctx, n_lat)):
            for r0 in range(0, seg_len, tile):
                acc = jnp.zeros((tile, hd), F32)
                for j in range(GDN_CONV):
                    s = seg_off + r0 + j - half
                    acc = acc + xpad[s:s + tile, :] * w[j:j + 1, :]
                y = _silu(acc)
                if mode != "v":
                    y = y * lax.rsqrt(jnp.sum(y * y, axis=-1, keepdims=True) + NORM_EPS)
                if mode == "q":
                    y = y * (hd ** -0.5)
                dst[dst_off + r0:dst_off + r0 + tile, :] = y

    conv_into(qc_ref, ql_ref, cwq_ref, qs, "q")
    conv_into(kc_ref, kl_ref, cwk_ref, ks, "k")
    conv_into(vc_ref, vl_ref, cwv_ref, vs, "v")

    sel_r = lax.broadcasted_iota(I32, (LANES, LANES), 0)
    pos = lax.broadcasted_iota(I32, (tile, LANES), 0) % cs
    for d in range(2):
        col = d * GDN_HEADS + h
        sel_a = (sel_r == col).astype(BF16)
        sel_b = (sel_r == (2 * GDN_HEADS + col)).astype(BF16)
        neg_a = -jnp.exp(jnp.full((1, LANES), alog_ref[d, h], F32))
        dtb = dtb_ref[d, h]
        for src, r_off, r_len in ((abc_ref, 0, n_ctx), (abl_ref, n_ctx, n_lat)):
            for r0 in range(0, r_len, tile):
                ab = src[r0:r0 + tile, :]
                a_b = _dot(ab, sel_a)
                b_b = _dot(ab, sel_b)
                gc = neg_a * _softplus(a_b + dtb)
                s = 1
                while s < cs:
                    if d == 0:
                        gc = gc + jnp.where(pos >= s, pltpu.roll(gc, s, 0), 0.0)
                    else:
                        gc = gc + jnp.where(pos < cs - s, pltpu.roll(gc, tile - s, 0), 0.0)
                    s *= 2
                gsc[d, r_off + r0:r_off + r0 + tile, :] = gc
                bsc[d, r_off + r0:r_off + r0 + tile, :] = jax.nn.sigmoid(b_b)

    ri = lax.broadcasted_iota(I32, (cs, cs), 0)
    ci = lax.broadcasted_iota(I32, (cs, cs), 1)
    low = (ri >= ci)
    up = (ri <= ci)
    eye = (ri == ci).astype(F32)
    offdiag = (ri != ci).astype(F32)
    assert cs in (GDN_SOLVE_BLOCK, 2 * GDN_SOLVE_BLOCK)
    same_blk = ((ri // GDN_SOLVE_BLOCK) == (ci // GDN_SOLVE_BLOCK)).astype(F32)
    n_levels = int(math.log2(min(cs, GDN_SOLVE_BLOCK))) - 1

    def prep_group(first_chunk, count):
        chunk_ids = [first_chunk + uu for uu in range(count)]
        rows = [pl.ds(_aligned(c * cs, cs), cs) for c in chunk_ids]
        q = [qs[r, :] for r in rows]
        k = [ks[r, :] for r in rows]
        v = [vs[r, :] for r in rows]
        kb = [x.astype(BF16) for x in k]
        kkt = [_dot_nt(x, x) for x in kb]
        qkt = [_dot_nt(a.astype(BF16), b) for a, b in zip(q, kb)]
        probs = [(ui, d) for ui in range(count) for d in range(2)]
        gc_col, beta_b, dmat, t, p, n_off = {}, {}, {}, {}, {}, {}
        for ui, d in probs:
            gc = gsc[d, rows[ui], :]
            bt = bsc[d, rows[ui], :]
            gsq = gc[:, :cs]
            dm = jnp.where(low if d == 0 else up, jnp.exp(gsq - gsq.T), 0.0)
            n = -(kkt[ui] * bt[:, :cs] * dm * offdiag)
            gc_col[ui, d], beta_b[ui, d], dmat[ui, d] = gc, bt, dm
            n_diag = n * same_blk
            t[ui, d] = eye + n_diag
            p[ui, d] = n_diag.astype(BF16)
            n_off[ui, d] = (n - n_diag).astype(BF16)
        p = {key: _dot(x, x).astype(BF16) for key, x in p.items()}
        for lvl in range(n_levels):
            t_next = {key: t[key] + _dot(p[key], t[key].astype(BF16)) for key in probs}
            if lvl + 1 < n_levels:
                p = {key: _dot(x, x).astype(BF16) for key, x in p.items()}
            t = t_next
        tb = {key: x.astype(BF16) for key, x in t.items()}
        if cs > GDN_SOLVE_BLOCK:
            x_off = {key: _dot(tb[key], n_off[key]).astype(BF16) for key in probs}
            t = {key: t[key] + _dot(x_off[key], tb[key]) for key in probs}
            tb = {key: x.astype(BF16) for key, x in t.items()}
        eg = {key: jnp.exp(x) for key, x in gc_col.items()}
        u = {(ui, d): _dot(tb[ui, d], (v[ui] * beta_b[ui, d]).astype(BF16)) for ui, d in probs}
        w = {(ui, d): _dot(tb[ui, d], (k[ui] * beta_b[ui, d] * eg[ui, d]).astype(BF16)) for ui, d in probs}
        for ui, d in probs:
            r = rows[ui]
            last = cs - 1 if d == 0 else 0
            gl = gc_col[ui, d][last:last + 1, :]
            u_sc[d, r, :] = u[ui, d]
            w16[d, r, :] = w[ui, d].astype(BF16)
            qk16[d, r, :] = (qkt[ui] * dmat[ui, d]).astype(BF16)
            qd16[d, r, :] = (q[ui] * eg[ui, d]).astype(BF16)
            kdk16[d, r, :] = (k[ui] * jnp.exp(gl - gc_col[ui, d])).astype(BF16)
            g0 = _aligned(chunk_ids[ui] * SUBLANES, SUBLANES)
            glast[d, pl.ds(g0, SUBLANES), :] = jnp.broadcast_to(jnp.exp(gl), (SUBLANES, LANES))

    def prep(gi, carry):
        prep_group(gi * GDN_PREP_UNROLL, GDN_PREP_UNROLL)
        return carry

    n_groups = n_chunks // GDN_PREP_UNROLL
    lax.fori_loop(0, n_groups, prep, 0)
    if n_chunks % GDN_PREP_UNROLL:
        prep_group(n_groups * GDN_PREP_UNROLL, n_chunks % GDN_PREP_UNROLL)

    oacc[...] = jnp.zeros_like(oacc)

    def step(i, carry):
        c_b = jnp.where(i < n_ctx_chunks, n_ctx_chunks - 1 - i, n_chunks + n_ctx_chunks - 1 - i)
        cid = (i, c_b)
        rows = [pl.ds(pl.multiple_of(c * cs, cs), cs) for c in cid]
        sb = [s.astype(BF16) for s in carry]
        ws = [_dot(w16[d, rows[d], :], sb[d]) for d in range(2)]
        qs_s = [_dot(qd16[d, rows[d], :], sb[d]) for d in range(2)]
        vnb = [(u_sc[d, rows[d], :] - ws[d]).astype(BF16) for d in range(2)]
        o = [qs_s[d] + _dot(qk16[d, rows[d], :], vnb[d]) for d in range(2)]
        kv = [_dot_tn(kdk16[d, rows[d], :], vnb[d]) for d in range(2)]
        s_new = []
        for d in range(2):
            g0 = pl.multiple_of(cid[d] * SUBLANES, SUBLANES)
            s_new.append(carry[d] * glast[d, pl.ds(g0, 1), :] + kv[d])
            oacc[rows[d], :] = oacc[rows[d], :] + o[d]
        return tuple(s_new)

    s0 = jnp.zeros((hd, GDN_DV), F32)
    lax.fori_loop(0, n_chunks, step, (s0, s0))

    onw = onorm_ref[...]

    def out_gate(o, z):
        y = o * lax.rsqrt(jnp.mean(o * o, axis=-1, keepdims=True) + NORM_EPS) * onw
        return (y * _silu(z.astype(F32))).astype(BF16)

    yc_ref[...] = out_gate(oacc[0:n_ctx, :], zc_ref[...])
    for r0 in range(0, n_lat, tile):
        yl_ref[r0:r0 + tile, :] = out_gate(oacc[n_ctx + r0:n_ctx + r0 + tile, :], zl_ref[r0:r0 + tile, :])


def _gdn(p, conv_w, a_log, dt_bias, out_norm, l, geo):
    b, n_ctx, n_lat = geo["b"], geo["ctx"], geo["lat"]
    n_all = n_ctx + n_lat
    lat0 = geo["tc"] // n_lat
    hd = GDN_DK

    def lat_spec(col):
        return pl.BlockSpec((n_lat, hd), lambda bi, h: (lat0 + bi, col // hd + h))

    def ctx_spec(col):
        return pl.BlockSpec((n_ctx, hd), lambda bi, h: (bi, col // hd + h))

    def fixed_lat(col):
        return pl.BlockSpec((n_lat, hd), lambda bi, h: (lat0 + bi, col // hd))

    def fixed_ctx(col):
        return pl.BlockSpec((n_ctx, hd), lambda bi, h: (bi, col // hd))

    def cw_spec(col):
        return pl.BlockSpec((None, GDN_CONV, hd), lambda bi, h: (l, 0, col // hd + h))

    smem = pl.BlockSpec(memory_space=pltpu.SMEM)
    kern = functools.partial(_gdn_kernel, n_ctx=n_ctx, n_lat=n_lat)
    n_chunks = n_all // GDN_CHUNK
    assert n_ctx % GDN_CHUNK == 0 and n_lat % GDN_CHUNK == 0
    scratch = [
        pltpu.VMEM((n_all + 4 * SUBLANES, hd), F32),
        pltpu.VMEM((n_all, hd), F32),
        pltpu.VMEM((n_all, hd), F32),
        pltpu.VMEM((n_all, hd), F32),
        pltpu.VMEM((2, n_all, LANES), F32),
        pltpu.VMEM((2, n_all, LANES), F32),
        pltpu.VMEM((2, n_all, GDN_DV), F32),
        pltpu.VMEM((2, n_all, hd), BF16),
        pltpu.VMEM((2, n_all, GDN_CHUNK), BF16),
        pltpu.VMEM((2, n_all, hd), BF16),
        pltpu.VMEM((2, n_all, hd), BF16),
        pltpu.VMEM((2, n_chunks * SUBLANES, LANES), F32),
        pltpu.VMEM((n_all, GDN_DV), F32),
    ]
    y_lat, y_ctx = pl.pallas_call(
        kern,
        out_shape=(jax.ShapeDtypeStruct((b * n_lat, GDN_HEADS * GDN_DV), BF16),
                   jax.ShapeDtypeStruct((b * n_ctx, GDN_HEADS * GDN_DV), BF16)),
        grid=(b, GDN_HEADS),
        in_specs=[smem, smem,
                  lat_spec(COL_Q), lat_spec(COL_K), lat_spec(COL_V), lat_spec(COL_Z), fixed_lat(COL_AB),
                  ctx_spec(COL_Q), ctx_spec(COL_K), ctx_spec(COL_V), ctx_spec(COL_Z), fixed_ctx(COL_AB),
                  cw_spec(0), cw_spec(GDN_HEADS * hd), cw_spec(2 * GDN_HEADS * hd),
                  pl.BlockSpec((None, 1, GDN_DV), lambda bi, h: (l, 0, 0))],
        out_specs=(pl.BlockSpec((n_lat, GDN_DV), lambda bi, h: (bi, h)),
                   pl.BlockSpec((n_ctx, GDN_DV), lambda bi, h: (bi, h))),
        scratch_shapes=scratch,
        compiler_params=_cparams(("arbitrary", "arbitrary")),
        name="gdn",
    )(a_log[l], dt_bias[l], p, p, p, p, p, p, p, p, p, p, conv_w, conv_w, conv_w, out_norm)
    return y_lat, y_ctx


def _head_rms(x, bd, w):
    ss = _dot(x * x, bd, precision=HIGHEST) * (1.0 / HEAD_DIM)
    return x * lax.rsqrt(ss + NORM_EPS) * w


def _rope(x, cos, sin_signed, lane):
    nf = HEAD_DIM // 4
    partner = jnp.where((lane % (2 * nf)) < nf, pltpu.roll(x, LANES - nf, 1), pltpu.roll(x, nf, 1))
    return x * cos + partner * sin_signed


def _attn_kernel(sink_ref, q_ref, kq_ref, vq_ref, kc_ref, vc_ref,
                 cosq_ref, sinq_ref, cosk_ref, sink_k_ref, qw_ref, kw_ref,
                 o_ref, kp, vp, kcp, vcp, *, n_keys, n_ctx, local):
    n = pl.program_id(1)
    w = WINDOW
    lane = lax.broadcasted_iota(I32, (1, LANES), 1)
    lo_mask = (lane < HEAD_DIM).astype(F32)
    hi_mask = 1.0 - lo_mask
    bd_r = lax.broadcasted_iota(I32, (LANES, LANES), 0) // HEAD_DIM
    bd_c = lax.broadcasted_iota(I32, (LANES, LANES), 1) // HEAD_DIM
    bd = (bd_r == bd_c).astype(F32)
    kw = kw_ref[...]

    def store_variants(dst, row0, x):
        xr = pltpu.roll(x, HEAD_DIM, 1)
        rows = x.shape[0]
        dst[0, row0:row0 + rows, :] = (x * lo_mask).astype(BF16)
        dst[1, row0:row0 + rows, :] = (xr * hi_mask).astype(BF16)
        dst[2, row0:row0 + rows, :] = (xr * lo_mask).astype(BF16)
        dst[3, row0:row0 + rows, :] = (x * hi_mask).astype(BF16)

    @pl.when(n == 0)
    def _prepare_keys():
        tile = 256
        for r0 in range(0, n_ctx, tile):
            rr = min(tile, n_ctx - r0)
            kc = _head_rms(kc_ref[r0:r0 + rr, :].astype(F32), bd, kw)
            store_variants(kcp, r0, kc)
            store_variants(vcp, r0, vc_ref[r0:r0 + rr, :].astype(F32))
        if local:
            zero = jnp.zeros((w, LANES), BF16)
            for t in range(4):
                kp[t, 0:w, :] = zero
                kp[t, w + n_keys:2 * w + n_keys, :] = zero
                vp[t, 0:w, :] = zero
                vp[t, w + n_keys:2 * w + n_keys, :] = zero
            for r0 in range(0, n_keys, tile):
                k = _head_rms(kq_ref[r0:r0 + tile, :].astype(F32), bd, kw)
                k = _rope(k, cosk_ref[r0:r0 + tile, :], sink_k_ref[r0:r0 + tile, :], lane)
                store_variants(kp, w + r0, k)
                store_variants(vp, w + r0, vq_ref[r0:r0 + tile, :].astype(F32))

    qw = qw_ref[...]
    q2 = []
    for g in range(ATT_HEADS // 2):
        qg = _head_rms(q_ref[:, g * LANES:(g + 1) * LANES].astype(F32), bd, qw)
        if local:
            qg = _rope(qg, cosq_ref[...], sinq_ref[...], lane)
        q2.append((qg * (HEAD_DIM ** -0.5)).astype(BF16))

    rows2 = 2 * w
    row_i = lax.broadcasted_iota(I32, (rows2, 1), 0)
    first = row_i < w
    if local:
        n_blk = n_keys // w
        qq = lax.broadcasted_iota(I32, (rows2, 3 * w), 0) % w
        kcol = lax.broadcasted_iota(I32, (rows2, 3 * w), 1)
        kk = kcol % w
        blk = kcol // w
        valid = ((blk == 1)
                 | ((blk == 0) & (kk >= qq) & (n >= 1))
                 | ((blk == 2) & (kk <= qq) & (n < n_blk - 1)))
        k0 = pl.multiple_of(n * w, w)

    for j in range(ATT_KV_HEADS):
        lhs = jnp.concatenate([q2[2 * j], q2[2 * j + 1]], axis=0)
        pair_out = [None, None]
        for t in range(2):
            var = 2 * j + t
            sink = jnp.where(first, sink_ref[4 * j + t], sink_ref[4 * j + 2 + t])
            s_ctx = _dot_nt(lhs, kcp[var])
            m = jnp.maximum(jnp.max(s_ctx, axis=-1, keepdims=True), sink)
            if local:
                s_loc = _dot_nt(lhs, kp[var, pl.ds(k0, 3 * w), :])
                s_loc = jnp.where(valid, s_loc, NEG_INF)
                m = jnp.maximum(m, jnp.max(s_loc, axis=-1, keepdims=True))
            e_ctx = jnp.exp(s_ctx - m)
            den = jnp.sum(e_ctx, axis=-1, keepdims=True) + jnp.exp(sink - m)
            pv = _dot(e_ctx.astype(BF16), vcp[var])
            if local:
                e_loc = jnp.exp(s_loc - m)
                den = den + jnp.sum(e_loc, axis=-1, keepdims=True)
                pv = pv + _dot(e_loc.astype(BF16), vp[var, pl.ds(k0, 3 * w), :])
            o = pv / den
            for half_i in range(2):
                part = o[half_i * w:(half_i + 1) * w, :]
                pair_out[half_i] = part if pair_out[half_i] is None else pair_out[half_i] + part
        for half_i in range(2):
            c0 = (2 * j + half_i) * LANES
            o_ref[:, c0:c0 + LANES] = pair_out[half_i].astype(BF16)


def _attention(p, sink, q_norm_w, k_norm_w, rope_cos, rope_sin, l, geo, ctx_queries):
    b, n_ctx, n_lat = geo["b"], geo["ctx"], geo["lat"]
    w = WINDOW
    lat0 = geo["tc"] // n_lat
    local = not ctx_queries
    n_q = n_ctx if ctx_queries else n_lat
    nqb = n_q // w
    q_row0 = 0 if ctx_queries else geo["tc"] // w
    kern = functools.partial(_attn_kernel, n_keys=n_lat, n_ctx=n_ctx, local=local)
    in_specs = [
        pl.BlockSpec(memory_space=pltpu.SMEM),
        pl.BlockSpec((w, ATT_HEADS * HEAD_DIM), lambda bi, n: (q_row0 + bi * nqb + n, COL_AQ // 512)),
        pl.BlockSpec((n_lat, LANES), lambda bi, n: (lat0 + bi, COL_AK // LANES)),
        pl.BlockSpec((n_lat, LANES), lambda bi, n: (lat0 + bi, COL_AV // LANES)),
        pl.BlockSpec((n_ctx, LANES), lambda bi, n: (bi, COL_AK // LANES)),
        pl.BlockSpec((n_ctx, LANES), lambda bi, n: (bi, COL_AV // LANES)),
        pl.BlockSpec((w, LANES), lambda bi, n: (n if local else 0, 0)),
        pl.BlockSpec((w, LANES), lambda bi, n: (n if local else 0, 0)),
        pl.BlockSpec((n_lat, LANES), lambda bi, n: (0, 0)),
        pl.BlockSpec((n_lat, LANES), lambda bi, n: (0, 0)),
        pl.BlockSpec((None, 1, LANES), lambda bi, n: (l, 0, 0)),
        pl.BlockSpec((None, 1, LANES), lambda bi, n: (l, 0, 0)),
    ]
    scratch = [
        pltpu.VMEM((4, n_lat + 2 * w, LANES), BF16),
        pltpu.VMEM((4, n_lat + 2 * w, LANES), BF16),
        pltpu.VMEM((4, n_ctx, LANES), BF16),
        pltpu.VMEM((4, n_ctx, LANES), BF16),
    ]
    return pl.pallas_call(
        kern,
        out_shape=jax.ShapeDtypeStruct((b * n_q, ATT_HEADS * HEAD_DIM), BF16),
        grid=(b, nqb),
        in_specs=in_specs,
        out_specs=pl.BlockSpec((w, ATT_HEADS * HEAD_DIM), lambda bi, n: (bi * nqb + n, 0)),
        scratch_shapes=scratch,
        compiler_params=_cparams(("arbitrary", "arbitrary")),
        name="attn_ctx" if ctx_queries else "attn_lat",
    )(sink[l], p, p, p, p, p, rope_cos, rope_sin, rope_cos, rope_sin, q_norm_w, k_norm_w)


def _merge_kernel(x_ref, mod_ref, n2_ref, y0c_ref, y0l_ref, y3c_ref, y3l_ref,
                  sb_ref, sc_ref, sx_ref, scp_ref, sxp_ref, scn_ref, sxn_ref,
                  gu_ref, gv_ref, mg0_ref, mg1_ref, mg2_ref, mg3_ref,
                  scw_ref, lnw_ref, lnb_ref, sgw_ref, sgb_ref, wb_ref, wo_ref, rw_ref, rb_ref,
                  xo_ref, h2_ref, idx_ref, gate_ref, rank_ref, cnt_ref,
                  carry, *, tm, tc, n_ctx, n_lat):
    i = pl.program_id(0)
    r0 = i * tm

    row = lax.broadcasted_iota(I32, (tm, 1), 0)
    g_row = r0 + row
    in_ctx = g_row < tc
    seg_pos = jnp.where(in_ctx, g_row % n_ctx, (g_row - tc) % n_lat)
    seg_len = jnp.where(in_ctx, n_ctx, n_lat)
    cx = sc_ref[...].astype(F32) * sx_ref[...].astype(F32)
    cx_prev_halo = scp_ref[SUBLANES - 1:SUBLANES, :].astype(F32) * sxp_ref[SUBLANES - 1:SUBLANES, :].astype(F32)
    cx_next_halo = scn_ref[0:1, :].astype(F32) * sxn_ref[0:1, :].astype(F32)
    prev = jnp.where(row == 0, cx_prev_halo, pltpu.roll(cx, 1, 0))
    prev = jnp.where(seg_pos == 0, 0.0, prev)
    nxt = jnp.where(row == tm - 1, cx_next_halo, pltpu.roll(cx, tm - 1, 0))
    nxt = jnp.where(seg_pos == seg_len - 1, 0.0, nxt)
    scw = scw_ref[...]
    y1 = sb_ref[...].astype(F32) * (prev * scw[0:1, :] + cx * scw[1:2, :] + nxt * scw[2:3, :])

    inv_sqrt2 = 1.0 / math.sqrt(2.0)

    def gelu(t):
        return 0.5 * t * (1.0 + lax.erf(t * inv_sqrt2))

    u = gelu(gu_ref[...].astype(F32))
    v = gelu(gv_ref[...].astype(F32))
    mu = jnp.mean(v, axis=-1, keepdims=True)
    vc = v - mu
    v = vc * lax.rsqrt(jnp.mean(vc * vc, axis=-1, keepdims=True) + LN_EPS) * lnw_ref[...] + lnb_ref[...]
    vb = v.astype(BF16)
    gw = SG_WIDTH // SG_GROUPS
    chunks = []
    for c in range(tm // SG_CHUNK):
        groups = []
        for g in range(SG_GROUPS):
            groups.append(_dot(sgw_ref[g], vb[c * SG_CHUNK:(c + 1) * SG_CHUNK, g * gw:(g + 1) * gw]))
        chunks.append(jnp.concatenate(groups, axis=1) + sgb_ref[...])
    y2 = u * jnp.concatenate(chunks, axis=0)

    tile_in_ctx = r0 < tc
    y0 = jnp.where(tile_in_ctx, y0c_ref[...], y0l_ref[...])
    y3 = jnp.where(tile_in_ctx, y3c_ref[...], y3l_ref[...])
    ys = (y0, y1.astype(BF16), y2.astype(BF16), y3)
    gates = (mg0_ref, mg1_ref, mg2_ref, mg3_ref)
    m = None
    for br in range(N_BRANCH):
        term = jax.nn.sigmoid(gates[br][...].astype(F32)) * _dot(ys[br], wb_ref[br])
        m = term if m is None else m + term
    y = _dot(m.astype(BF16), wo_ref[...])
    x_new = x_ref[...] + mod_ref[2:3, :] * y
    xo_ref[...] = x_new

    ms = jnp.mean(x_new * x_new, axis=-1, keepdims=True)
    h2 = x_new * lax.rsqrt(ms + NORM_EPS) * n2_ref[...] * (1.0 + mod_ref[4:5, :]) + mod_ref[3:4, :]
    half = D_MODEL // 2
    _store_planes(h2_ref, _pack_bf16_pair(h2[:, :half], h2[:, half:]))

    h2_hi = h2.astype(BF16)
    h2_lo = (h2 - h2_hi.astype(F32)).astype(BF16)
    prod = _dot(jnp.concatenate([h2_hi, h2_lo], axis=1), rw_ref[...])
    lane = lax.broadcasted_iota(I32, (tm, LANES), 1)
    logits = prod + pltpu.roll(prod, LANES - N_EXPERTS, 1)
    logits = jnp.where(lane < N_EXPERTS, logits, NEG_INF) + rb_ref[...]
    lane_f = lane.astype(F32)
    work = logits
    topv = jnp.full((tm, LANES), NEG_INF, F32)
    topi = jnp.zeros((tm, LANES), I32)
    onehot = jnp.zeros((tm, LANES), F32)
    firsts = []
    for k in range(TOP_K):
        mx = jnp.max(work, axis=-1, keepdims=True)
        first_f = jnp.min(jnp.where(work == mx, lane_f, float(LANES)), axis=-1, keepdims=True)
        first = first_f.astype(I32)
        hit = lane == first
        topv = jnp.where(lane == k, mx, topv)
        topi = jnp.where(lane == k, first, topi)
        onehot = jnp.where(hit, 1.0, onehot)
        work = jnp.where(hit, -jnp.inf, work)
        firsts.append(first)
    e = jnp.where(lane < TOP_K, jnp.exp(topv - jnp.max(topv, axis=-1, keepdims=True)), 0.0)
    gate = e / jnp.sum(e, axis=-1, keepdims=True)

    @pl.when(i == 0)
    def _init():
        carry[...] = jnp.zeros_like(carry)

    tri = (lax.broadcasted_iota(I32, (tm, tm), 0) > lax.broadcasted_iota(I32, (tm, tm), 1)).astype(BF16)
    before = carry[0:1, :] + _dot(tri, onehot.astype(BF16))
    rank = jnp.zeros((tm, LANES), F32)
    for k in range(TOP_K):
        r_k = jnp.sum(jnp.where(lane == firsts[k], before, 0.0), axis=-1, keepdims=True)
        rank = jnp.where(lane == k, r_k, rank)
    new_carry = carry[0:1, :] + jnp.sum(onehot, axis=0, keepdims=True)
    carry[...] = jnp.broadcast_to(new_carry, carry.shape)
    cnt_ref[...] = jnp.broadcast_to(new_carry, cnt_ref.shape).astype(I32)
    idx_ref[...] = topi[:, :SUBLANES]
    gate_ref[...] = gate[:, :SUBLANES]
    rank_ref[...] = rank[:, :SUBLANES].astype(I32)


def _merge(x, mod, norm2, y0_ctx, y0_lat, y3_ctx, y3_lat, p, w, l, geo):
    t, dm = x.shape
    tm = geo["tm_merge"]
    mod_row = geo["mod_row"]
    kern = functools.partial(_merge_kernel, tm=tm, tc=geo["tc"], n_ctx=geo["ctx"], n_lat=geo["lat"])
    halo = tm // SUBLANES
    n_halo = t // SUBLANES

    def col(c, width=512):
        return pl.BlockSpec((tm, width), lambda i: (i, c // width))

    def prev_halo(c):
        return pl.BlockSpec((SUBLANES, 512), lambda i: (jnp.maximum(i * halo - 1, 0), c // 512))

    def next_halo(c):
        return pl.BlockSpec((SUBLANES, 512), lambda i: (jnp.minimum((i + 1) * halo, n_halo - 1), c // 512))

    def layer(shape):
        nd = len(shape)
        return pl.BlockSpec((None,) + shape, lambda i: (l,) + (0,) * nd)

    n_ctx_tiles = geo["tc"] // tm

    def ctx_rows():
        return pl.BlockSpec((tm, 512), lambda i: (jnp.minimum(i, n_ctx_tiles - 1), 0))

    def lat_rows():
        return pl.BlockSpec((tm, 512), lambda i: (jnp.maximum(i - n_ctx_tiles, 0), 0))

    in_specs = [
        pl.BlockSpec((tm, dm), lambda i: (i, 0)),
        pl.BlockSpec((None, None, N_MOD, dm), lambda i: (l, mod_row(i * tm), 0, 0)),
        layer((1, dm)),
        ctx_rows(), lat_rows(), ctx_rows(), lat_rows(),
        col(COL_SB), col(COL_SC), col(COL_SX),
        prev_halo(COL_SC), prev_halo(COL_SX), next_halo(COL_SC), next_halo(COL_SX),
        col(COL_GU), col(COL_GV),
        col(COL_MG, 1024), col(COL_MG + 1024, 1024), col(COL_MG + 2048, 1024), col(COL_MG + 3072, 1024),
        layer((SC_CONV, SC_WIDTH)), layer((1, SG_WIDTH)), layer((1, SG_WIDTH)),
        layer((SG_GROUPS, SG_CHUNK, SG_CHUNK)), layer((SG_CHUNK, SG_WIDTH)),
        layer((N_BRANCH, BRANCH_W, dm)), layer((dm, dm)), layer((2 * dm, LANES)), layer((1, LANES)),
    ]
    out_shape = (
        jax.ShapeDtypeStruct((t, dm), F32),
        jax.ShapeDtypeStruct((SC_PLANES, t, SC_ROW), U32),
        jax.ShapeDtypeStruct((t, SUBLANES), I32),
        jax.ShapeDtypeStruct((t, SUBLANES), F32),
        jax.ShapeDtypeStruct((t, SUBLANES), I32),
        jax.ShapeDtypeStruct((SUBLANES, LANES), I32),
    )
    out_specs = (
        pl.BlockSpec((tm, dm), lambda i: (i, 0)),
        pl.BlockSpec((SC_PLANES, tm, SC_ROW), lambda i: (0, i, 0)),
        pl.BlockSpec((tm, SUBLANES), lambda i: (i, 0)),
        pl.BlockSpec((tm, SUBLANES), lambda i: (i, 0)),
        pl.BlockSpec((tm, SUBLANES), lambda i: (i, 0)),
        pl.BlockSpec((SUBLANES, LANES), lambda i: (0, 0)),
    )
    return pl.pallas_call(
        kern,
        out_shape=out_shape,
        grid=(t // tm,),
        in_specs=in_specs,
        out_specs=out_specs,
        scratch_shapes=[pltpu.VMEM((SUBLANES, LANES), F32)],
        compiler_params=_cparams(("arbitrary",)),
        name="merge",
    )(x, mod, norm2, y0_ctx, y0_lat, y3_ctx, y3_lat, p, p, p, p, p, p, p, p, p, p, p, p, p,
      w["sc_conv"], w["sg_ln_w"], w["sg_ln_b"], w["sg_w"], w["sg_b"], w["w_branch"], w["w_out"],
      w["router_w"], w["router_b"])


def _sc_mesh():
    return plsc.VectorSubcoreMesh(core_axis_name="core", subcore_axis_name="subcore")


def _sc_scatter_rows(src, idx, n_out):
    n, width = src.shape

    @pl.kernel(out_type=jax.ShapeDtypeStruct((n_out, width), U32), mesh=_sc_mesh(), scratch_types=[])
    def scatter(x_hbm, i_hbm, o_hbm):
        def body(x_vmem, *i_vmem):
            for iv in i_vmem:
                pltpu.sync_copy(x_vmem, o_hbm.at[iv.at[0]])

        pltpu.emit_pipeline(
            body,
            grid=(n // SC_WINDOW,),
            in_specs=[pl.BlockSpec((SC_WINDOW, width), lambda i: (i, 0))]
            + [pl.BlockSpec((1, SC_WINDOW), functools.partial(lambda i, k: (k, i), k=k)) for k in range(TOP_K)],
            out_specs=[],
            core_axis_name=("core", "subcore"),
            dimension_semantics=(pltpu.PARALLEL,),
        )(x_hbm, *([i_hbm] * TOP_K))

    return scatter(src, idx)


def _sc_gather_rows(src, idx):
    n = idx.shape[0]
    width = src.shape[1]

    @pl.kernel(out_type=jax.ShapeDtypeStruct((n, width), U32), mesh=_sc_mesh(), scratch_types=[])
    def gather(x_hbm, i_hbm, o_hbm):
        def body(i_vmem, o_vmem):
            pltpu.sync_copy(x_hbm.at[i_vmem.at[0]], o_vmem)

        pltpu.emit_pipeline(
            body,
            grid=(n // SC_WINDOW,),
            in_specs=[pl.BlockSpec((1, SC_WINDOW), lambda i: (0, i))],
            out_specs=[pl.BlockSpec((SC_WINDOW, width), lambda i: (i, 0))],
            core_axis_name=("core", "subcore"),
            dimension_semantics=(pltpu.PARALLEL,),
        )(i_hbm, o_hbm)

    return gather(src, idx.reshape(1, n))


def _expert_kernel(be_ref, bv_ref, xs_ref, w1_ref, b1_ref, w2_ref, b2_ref, ys_ref):
    del be_ref
    valid = lax.broadcasted_iota(I32, (MOE_BM, 1), 0) < bv_ref[pl.program_id(0)]
    lo, hi = _unpack_bf16_pair(jnp.where(valid, _load_planes(xs_ref), jnp.uint32(0)))
    half = D_MODEL // 2
    h = (_dot(lo.astype(BF16), w1_ref[0:half, :]) + _dot(hi.astype(BF16), w1_ref[half:, :]) + b1_ref[...])
    hg = W1_GROUP // 2
    acts = []
    for g in range(2 * D_EXPERT // W1_GROUP):
        glu = jnp.minimum(h[:, g * W1_GROUP:g * W1_GROUP + hg], SWIGLU_LIMIT)
        lin = jnp.clip(h[:, g * W1_GROUP + hg:(g + 1) * W1_GROUP], -SWIGLU_LIMIT, SWIGLU_LIMIT)
        acts.append((glu * jax.nn.sigmoid(SWIGLU_ALPHA * glu) * (lin + 1.0)).astype(BF16))
    y = _dot(jnp.concatenate(acts, axis=1), w2_ref[...]) + b2_ref[...]
    _store_planes(ys_ref, _pack_bf16_pair(y[:, :half], y[:, half:]))


def _w1_regroup_kernel(w_ref, o_ref):
    hg = W1_GROUP // 2
    r = lax.broadcasted_iota(I32, (W1_GROUP, W1_GROUP), 0)
    c = lax.broadcasted_iota(I32, (W1_GROUP, W1_GROUP), 1)
    perm = (r == jnp.where(c < hg, 2 * c, 2 * (c - hg) + 1)).astype(BF16)
    for g in range(2 * D_EXPERT // W1_GROUP):
        sl = slice(g * W1_GROUP, (g + 1) * W1_GROUP)
        o_ref[:, sl] = _dot(w_ref[:, sl].astype(BF16), perm).astype(BF16)


def _w1_regroup(exp_w1):
    depth, ne, dm, dh = exp_w1.shape
    tr = 512
    out = pl.pallas_call(
        _w1_regroup_kernel,
        out_shape=jax.ShapeDtypeStruct((depth * ne, dm, dh), BF16),
        grid=(depth * ne, dm // tr),
        in_specs=[pl.BlockSpec((None, tr, dh), lambda e, i: (e, i, 0))],
        out_specs=pl.BlockSpec((None, tr, dh), lambda e, i: (e, i, 0)),
        compiler_params=_cparams(("arbitrary", "arbitrary")),
        name="w1_regroup",
    )(exp_w1.reshape(depth * ne, dm, dh))
    return out.reshape(depth, ne, dm, dh)


def _experts(xs, block_expert, block_valid, w1p, b1p, w2, b2, l):
    n_slots = xs.shape[1]
    bm = MOE_BM
    grid_spec = pltpu.PrefetchScalarGridSpec(
        num_scalar_prefetch=2,
        grid=(n_slots // bm,),
        in_specs=[
            pl.BlockSpec((SC_PLANES, bm, SC_ROW), lambda i, be, bv: (0, i, 0)),
            pl.BlockSpec((None, None, D_MODEL, 2 * D_EXPERT), lambda i, be, bv: (l, be[i], 0, 0)),
            pl.BlockSpec((None, None, 1, 2 * D_EXPERT), lambda i, be, bv: (l, be[i], 0, 0)),
            pl.BlockSpec((None, None, D_EXPERT, D_MODEL), lambda i, be, bv: (l, be[i], 0, 0)),
            pl.BlockSpec((None, None, 1, D_MODEL), lambda i, be, bv: (l, be[i], 0, 0)),
        ],
        out_specs=pl.BlockSpec((SC_PLANES, bm, SC_ROW), lambda i, be, bv: (0, i, 0)),
    )
    return pl.pallas_call(
        _expert_kernel,
        out_shape=jax.ShapeDtypeStruct((SC_PLANES, n_slots, SC_ROW), U32),
        grid_spec=grid_spec,
        compiler_params=_cparams(("arbitrary",)),
        name="moe_experts",
    )(block_expert, block_valid, xs, w1p, b1p, w2, b2)


def _combine_kernel(gate_ref, x_ref, mod_ref, yg_ref, xo_ref, *, tm):
    gate = gate_ref[...]
    half = D_MODEL // 2
    f_lo = jnp.zeros((tm, half), F32)
    f_hi = jnp.zeros((tm, half), F32)
    for k in range(TOP_K):
        lo, hi = _unpack_bf16_pair(_load_planes(yg_ref, k * SC_PLANES))
        g = gate[:, k:k + 1]
        f_lo = f_lo + g * lo
        f_hi = f_hi + g * hi
    g2 = mod_ref[5:6, :]
    xo_ref[:, :half] = x_ref[:, :half] + g2[:, :half] * f_lo
    xo_ref[:, half:] = x_ref[:, half:] + g2[:, half:] * f_hi


def _combine(x, mod, gate, yg, l, geo, first_row=0):
    t, dm = x.shape
    tm = geo["tm_moe"]
    mod_row = geo["mod_row"]
    i0 = first_row // tm
    return pl.pallas_call(
        functools.partial(_combine_kernel, tm=tm),
        out_shape=jax.ShapeDtypeStruct((t - first_row, dm), F32),
        grid=((t - first_row) // tm,),
        in_specs=[
            pl.BlockSpec((tm, SUBLANES), lambda i: (i0 + i, 0)),
            pl.BlockSpec((tm, dm), lambda i: (i0 + i, 0)),
            pl.BlockSpec((None, None, N_MOD, dm), lambda i: (l, mod_row((i0 + i) * tm), 0, 0)),
            pl.BlockSpec((TOP_K * SC_PLANES, tm, SC_ROW), lambda i: (0, i0 + i, 0)),
        ],
        out_specs=pl.BlockSpec((tm, dm), lambda i: (i, 0)),
        compiler_params=_cparams(("arbitrary",)),
        name="moe_combine",
    )(gate, x, mod, yg)


def _moe_plan(idx, rank, counts, n_blocks):
    bm = MOE_BM
    padded = (counts + bm - 1) // bm * bm
    pad_end = jnp.cumsum(padded)
    pad_start = pad_end - padded
    expert = idx[:, :TOP_K]
    onehot = expert[:, :, None] == jnp.arange(N_EXPERTS, dtype=I32)[None, None, :]
    dest = rank[:, :TOP_K] + jnp.sum(jnp.where(onehot, pad_start[None, None, :], 0), axis=-1)
    block_start = jnp.arange(n_blocks, dtype=I32) * bm
    block_expert = jnp.minimum(
        jnp.sum((block_start[:, None] >= pad_end[None, :]).astype(I32), axis=-1), N_EXPERTS - 1)
    block_valid = jnp.clip(counts[block_expert] - (block_start - pad_start[block_expert]), 0, bm)
    plane_off = jnp.arange(SC_PLANES, dtype=I32) * (n_blocks * bm)
    dest_rows = jnp.transpose(dest.astype(I32))[:, None, :] + plane_off[None, :, None]
    scatter_idx = dest_rows.reshape(TOP_K, -1)
    gather_idx = dest_rows.reshape(-1)
    return scatter_idx, gather_idx, block_expert.astype(I32), block_valid.astype(I32)


def _rope_tables(n_lat):
    rows = n_lat // GRID_W
    row = jnp.repeat(jnp.arange(rows, dtype=F32), GRID_W)
    colp = jnp.tile(jnp.arange(GRID_W, dtype=F32), rows)
    n_freq = HEAD_DIM // 4
    inv = ROPE_BASE ** (-jnp.arange(n_freq, dtype=F32) / n_freq)
    ang = jnp.concatenate([row[:, None] * inv, colp[:, None] * inv], axis=-1)
    cos, sin = jnp.cos(ang), jnp.sin(ang)
    cr, cc = cos[:, :n_freq], cos[:, n_freq:]
    sr, sc = sin[:, :n_freq], sin[:, n_freq:]
    cos64 = jnp.concatenate([cr, cr, cc, cc], axis=-1)
    sin64 = jnp.concatenate([-sr, sr, -sc, sc], axis=-1)
    return jnp.tile(cos64, (1, 2)), jnp.tile(sin64, (1, 2))


def _prep_weights(w_in, w_branch, w_out, router_w, router_b, sg_b, exp_w1, exp_b1, exp_w2, exp_b2,
                  attn_q_norm, attn_k_norm):
    depth, dm, _ = w_in.shape
    qkvz = 2048
    ab0, sc0, sg0, at0, mg0 = 2048, 2064, 3600, 4624, 5392
    pad = PROJ_COLS - (COL_AB + 16)
    w_p = jnp.concatenate([
        w_in[:, :, 0:qkvz],
        w_in[:, :, sc0:sg0],
        w_in[:, :, sg0:at0],
        w_in[:, :, at0:at0 + 512],
        w_in[:, :, mg0:mg0 + 4096],
        w_in[:, :, at0 + 512:at0 + 768],
        w_in[:, :, ab0:ab0 + 16],
        jnp.zeros((depth, dm, pad), w_in.dtype),
    ], axis=-1).astype(BF16)
    rw_hi = router_w.astype(BF16)
    rw_lo = (router_w - rw_hi.astype(F32)).astype(BF16)
    rw = jnp.concatenate([
        jnp.concatenate([rw_hi, rw_lo, jnp.zeros((depth, dm, LANES - 2 * N_EXPERTS), BF16)], axis=-1),
        jnp.concatenate([rw_hi, jnp.zeros((depth, dm, LANES - N_EXPERTS), BF16)], axis=-1),
    ], axis=1)
    rb = jnp.concatenate([router_b, jnp.full((depth, LANES - N_EXPERTS), NEG_INF, F32)], axis=-1)
    w1p = _w1_regroup(exp_w1)
    ne = exp_b1.shape[1]
    b1p = jnp.swapaxes(exp_b1.reshape(depth, ne, -1, W1_GROUP // 2, 2), -1, -2).reshape(depth, ne, -1)
    return {
        "w_in": w_p,
        "w_branch": w_branch.astype(BF16),
        "w_out": w_out.astype(BF16),
        "router_w": rw,
        "router_b": rb[:, None, :],
        "sg_b": jnp.repeat(jnp.swapaxes(sg_b, 1, 2), SG_WIDTH // SG_GROUPS, axis=2),
        "w1": w1p,
        "b1": b1p[:, :, None, :],
        "w2": exp_w2.astype(BF16),
        "b2": exp_b2[:, :, None, :],
        "q_norm": jnp.tile(attn_q_norm, (1, 2))[:, None, :],
        "k_norm": jnp.tile(attn_k_norm, (1, 2))[:, None, :],
    }


def kernel(x, c, ctx, c_ctx, norm1, norm2, w_mod, b_mod, w_in, gdn_conv, gdn_a_log, gdn_dt_bias, gdn_out_norm,
           sc_conv, sg_ln_w, sg_ln_b, sg_w, sg_b, attn_q_norm, attn_k_norm, attn_sink, w_branch, w_out,
           router_w, router_b, exp_w1, exp_b1, exp_w2, exp_b2):
    b, n_lat, dm = x.shape
    n_ctx = ctx.shape[1]
    depth = w_in.shape[0]
    tc = b * n_ctx
    t = tc + b * n_lat
    assert dm == D_MODEL and tc % n_lat == 0 and n_lat % 256 == 0 and n_ctx % 256 == 0
    tile_cap = math.gcd(tc, n_lat)

    def mod_row(r0):
        return jnp.where(r0 < tc, 0, 1 + (r0 - tc) // n_lat)

    geo = {
        "b": b, "ctx": n_ctx, "lat": n_lat, "tc": tc, "mod_row": mod_row,
        "tm_proj": min(512, tile_cap), "tm_merge": min(512, tile_cap), "tm_moe": min(512, tile_cap),
    }

    wts = _prep_weights(w_in, w_branch, w_out, router_w, router_b, sg_b, exp_w1, exp_b1, exp_w2, exp_b2,
                        attn_q_norm, attn_k_norm)
    rope_cos, rope_sin = _rope_tables(n_lat)

    mod_rows = -(-(1 + b) // SUBLANES) * SUBLANES
    c_all = jnp.concatenate([c_ctx[None, :], c, jnp.zeros((mod_rows - 1 - b, dm), F32)], axis=0)
    mod = _modulation(c_all, w_mod, b_mod).reshape(depth, mod_rows, N_MOD, dm)

    xs_flat = jnp.concatenate([ctx.reshape(tc, dm), x.reshape(b * n_lat, dm)], axis=0)
    n_blocks = -(-(t * TOP_K) // MOE_BM) + N_EXPERTS
    n_slots = n_blocks * MOE_BM

    for l in range(depth):
        last = l == depth - 1
        p = _in_proj(xs_flat, mod, norm1[:, None, :], wts["w_in"], l, geo)
        g_lat, g_ctx = _gdn(p, gdn_conv, gdn_a_log, gdn_dt_bias, gdn_out_norm[:, None, :], l, geo)
        a_lat = _attention(p, attn_sink, wts["q_norm"], wts["k_norm"], rope_cos, rope_sin, l, geo, False)
        if last:
            a_ctx = jnp.zeros((tc, ATT_HEADS * HEAD_DIM), BF16)
        else:
            a_ctx = _attention(p, attn_sink, wts["q_norm"], wts["k_norm"], rope_cos, rope_sin, l, geo, True)
        layer_w = {
            "sc_conv": sc_conv, "sg_ln_w": sg_ln_w[:, None, :], "sg_ln_b": sg_ln_b[:, None, :], "sg_w": sg_w.astype(BF16),
            "sg_b": wts["sg_b"], "w_branch": wts["w_branch"], "w_out": wts["w_out"],
            "router_w": wts["router_w"], "router_b": wts["router_b"],
        }
        x_mid, h2p, idx, gate, rank, counts = _merge(xs_flat, mod, norm2[:, None, :], g_ctx, g_lat, a_ctx, a_lat, p,
                                                          layer_w, l, geo)
        scatter_idx, gather_idx, block_expert, block_valid = _moe_plan(idx, rank, counts[0, :N_EXPERTS], n_blocks)
        xs_sorted = _sc_scatter_rows(h2p.reshape(SC_PLANES * t, SC_ROW), scatter_idx, SC_PLANES * n_slots)
        ys_sorted = _experts(xs_sorted.reshape(SC_PLANES, n_slots, SC_ROW), block_expert, block_valid,
                             wts["w1"], wts["b1"], wts["w2"], wts["b2"], l)
        yg = _sc_gather_rows(ys_sorted.reshape(SC_PLANES * n_slots, SC_ROW), gather_idx)
        xs_flat = _combine(x_mid, mod, gate, yg.reshape(TOP_K * SC_PLANES, t, SC_ROW), l, geo,
                           first_row=tc if last else 0)

    return xs_flat.reshape(b, n_lat, dm)
```
